```python
import math
import jax, jax.numpy as jnp
from jax import lax
import numpy as np

D_MODEL = 1024
BATCH = 4
SEQ = 4096
DEPTH = 1
DEC_BATCH = 16
DEC_SEQ = 16
PAST_LEN = 2048

CHUNK = 64
N_LEFT_CHUNKS = 8
BAND = N_LEFT_CHUNKS * CHUNK
GDN_HEADS = 8
GDN_DK = 128
GDN_DV = 128
GDN_CONV = 4
ATT_HEADS = 8
ATT_DH = 64
REL_CLIP = 128
D_FF = 2816
FFN_CONV = 3
PLE_DIM = 256
DEEPNORM_ALPHA = (2.0 * DEPTH) ** 0.25
DEEPNORM_BETA = (8.0 * DEPTH) ** -0.25
LN_EPS = 1e-5
RMS_EPS = 1e-6
L2_EPS = 1e-6

GDN_QK = GDN_HEADS * GDN_DK
GDN_V = GDN_HEADS * GDN_DV
GDN_CONV_CH = 2 * GDN_QK + GDN_V
ATT_W = ATT_HEADS * ATT_DH
IN_SIZES = (GDN_QK, GDN_QK, GDN_V, GDN_V, GDN_HEADS, GDN_HEADS, ATT_W, ATT_W, ATT_W, D_MODEL, D_MODEL)
D_IN = 2 * GDN_QK + 2 * GDN_V + 2 * GDN_HEADS + 3 * ATT_W + 2 * D_MODEL

kernel_name = 'hybrid_gdn_chunkband_convffn_stream_step'


def _layernorm(x, g, b):
    xf = x.astype(jnp.float32)
    mu = jnp.mean(xf, -1, keepdims=True)
    xc = xf - mu
    var = jnp.mean(xc * xc, -1, keepdims=True)
    return (xc * lax.rsqrt(var + LN_EPS) * g.astype(jnp.float32) + b.astype(jnp.float32)).astype(x.dtype)


def _l2norm(x):
    xf = x.astype(jnp.float32)
    return xf * lax.rsqrt(jnp.sum(xf * xf, -1, keepdims=True) + L2_EPS)


def _split_cols(h, sizes):
    outs = []
    start = 0
    for s in sizes:
        outs.append(h[..., start:start + s])
        start += s
    return outs


def _causal_dwconv(x, w, prev):
    t = x.shape[1]
    width = w.shape[0]
    xp = jnp.concatenate([prev.astype(x.dtype), x], axis=1)
    y = xp[:, 0:t] * w[0]
    for i in range(1, width):
        y = y + xp[:, i:i + t] * w[i]
    return y, xp[:, t:]


def _gated_delta_chunked(q, k, v, g, beta, s0):
    b, t, h, dk = q.shape
    dv = v.shape[-1]
    c = min(CHUNK, t)
    n = t // c
    f32 = jnp.float32

    def to_blocks(a):
        return a.astype(f32).reshape(b, n, c, h, -1).transpose(1, 0, 3, 2, 4)

    qc, kc, vc = to_blocks(q), to_blocks(k), to_blocks(v)
    gc = g.astype(f32).reshape(b, n, c, h).transpose(1, 0, 3, 2)
    bc = beta.astype(f32).reshape(b, n, c, h).transpose(1, 0, 3, 2)
    decay = jnp.cumsum(gc, axis=-1)
    causal = jnp.tril(jnp.ones((c, c), dtype=bool))
    strict = jnp.tril(jnp.ones((c, c), dtype=bool), k=-1)
    diff = decay[..., :, None] - decay[..., None, :]
    lmask = jnp.exp(jnp.where(causal, diff, -jnp.inf))
    kb = kc * bc[..., None]
    m = jnp.where(strict, jnp.einsum('nbhid,nbhjd->nbhij', kb, kc) * lmask, 0.0)
    a_mat = m + jnp.eye(c, dtype=f32)
    rhs = jnp.concatenate([vc * bc[..., None], kb * jnp.exp(decay)[..., None]], axis=-1)
    sol = lax.linalg.triangular_solve(a_mat, rhs, left_side=True, lower=True, unit_diagonal=True)
    u, w = sol[..., :dv], sol[..., dv:]
    qk = jnp.einsum('nbhid,nbhjd->nbhij', qc, kc) * lmask

    def step(s, inp):
        qn, kn, un, wn, qkn, dn = inp
        v_new = un - jnp.einsum('bhcd,bhde->bhce', wn, s)
        o = (jnp.einsum('bhcd,bhde->bhce', qn * jnp.exp(dn)[..., None], s)
             + jnp.einsum('bhij,bhje->bhie', qkn, v_new))
        dl = dn[..., -1:]
        s = s * jnp.exp(dl)[..., None] + jnp.einsum('bhcd,bhce->bhde', kn * jnp.exp(dl - dn)[..., None], v_new)
        return s, o

    s_fin, o = lax.scan(step, s0.astype(f32), (qc, kc, u, w, qk, decay))
    o = o.transpose(1, 0, 3, 2, 4).reshape(b, t, h, dv)
    return o, s_fin.astype(s0.dtype)


def _rel_bias(table, n_q, n_past, n_k):
    rel = jnp.arange(n_q)[:, None] + n_past - jnp.arange(n_k)[None, :]
    idx = jnp.clip(rel, -REL_CLIP, REL_CLIP) + REL_CLIP
    return table[:, idx].astype(jnp.float32)


def _band_attention_prompt(q, k, v, table):
    b, t, h, dh = q.shape
    n = t // CHUNK
    width = BAND + CHUNK
    pad = ((0, 0), (BAND, 0), (0, 0), (0, 0))
    kp = jnp.pad(k, pad)
    vp = jnp.pad(v, pad)
    bias = _rel_bias(table, CHUNK, BAND, width)
    qc = q.reshape(b, n, CHUNK, h, dh).transpose(1, 0, 2, 3, 4)
    scale = dh ** -0.5

    def one_chunk(args):
        ci, qn = args
        start = ci * CHUNK
        kn = lax.dynamic_slice_in_dim(kp, start, width, axis=1)
        vn = lax.dynamic_slice_in_dim(vp, start, width, axis=1)
        s = jnp.einsum('bqhd,bkhd->bhqk', qn, kn).astype(jnp.float32) * scale + bias
        valid = (start - BAND + jnp.arange(width)) >= 0
        s = jnp.where(valid, s, -jnp.inf)
        pr = jax.nn.softmax(s, axis=-1).astype(v.dtype)
        return jnp.einsum('bhqk,bkhd->bqhd', pr, vn)

    o = lax.map(one_chunk, (jnp.arange(n), qc))
    return o.transpose(1, 0, 2, 3, 4).reshape(b, t, h, dh)


def _band_attention_step(q, k, v, k_prev, v_prev, table):
    n_past = k_prev.shape[1]
    kk = jnp.concatenate([k_prev.astype(k.dtype), k], axis=1)
    vv = jnp.concatenate([v_prev.astype(v.dtype), v], axis=1)
    bias = _rel_bias(table, q.shape[1], n_past, kk.shape[1])
    s = jnp.einsum('bqhd,bkhd->bhqk', q, kk).astype(jnp.float32) * (q.shape[-1] ** -0.5) + bias
    pr = jax.nn.softmax(s, axis=-1).astype(v.dtype)
    return jnp.einsum('bhqk,bkhd->bqhd', pr, vv)


def _conv_ffn(x, w_up, conv_w, w_down, prev):
    gate, val = jnp.split(x @ w_up, 2, axis=-1)
    gate, new_prev = _causal_dwconv(gate, conv_w, prev)
    return (jax.nn.gelu(gate) * val) @ w_down, new_prev


def _layer(x, pe, gconv_prev, s_gdn, k_prev, v_prev, fconv_prev, lw):
    (w_in, gdn_conv_w, gdn_a_log, gdn_dt_bias, gdn_norm_w, att_rel_bias, w_branch_a, w_branch_b,
     w_out, ln1_g, ln1_b, w_ffn_up, ffn_conv_w, w_ffn_down, w_ple, w_ple_gate, ln2_g, ln2_b) = lw
    b, t, _ = x.shape
    h = x @ w_in
    q_a, k_a, v_a, z_a, a_a, b_a, q_b, k_b, v_b, g_a, g_b = _split_cols(h, IN_SIZES)

    qkv, gconv_new = _causal_dwconv(jnp.concatenate([q_a, k_a, v_a], axis=-1), gdn_conv_w, gconv_prev)
    qkv = jax.nn.silu(qkv)
    q_a, k_a, v_a = _split_cols(qkv, (GDN_QK, GDN_QK, GDN_V))
    q_a = _l2norm(q_a.reshape(b, t, GDN_HEADS, GDN_DK)) * (GDN_DK ** -0.5)
    k_a = _l2norm(k_a.reshape(b, t, GDN_HEADS, GDN_DK))
    v_a = v_a.reshape(b, t, GDN_HEADS, GDN_DV)
    beta = jax.nn.sigmoid(b_a.astype(jnp.float32))
    g_log = -jnp.exp(gdn_a_log.astype(jnp.float32)) * jax.nn.softplus(
        a_a.astype(jnp.float32) + gdn_dt_bias.astype(jnp.float32))
    o_a, s_gdn_new = _gated_delta_chunked(q_a, k_a, v_a, g_log, beta, s_gdn)
    o_a = (o_a * lax.rsqrt(jnp.mean(o_a * o_a, -1, keepdims=True) + RMS_EPS) * gdn_norm_w.astype(jnp.float32)
           * jax.nn.silu(z_a.reshape(b, t, GDN_HEADS, GDN_DV).astype(jnp.float32)))
    y_a = o_a.reshape(b, t, GDN_V).astype(x.dtype) @ w_branch_a

    q_b = q_b.reshape(b, t, ATT_HEADS, ATT_DH)
    k_b = k_b.reshape(b, t, ATT_HEADS, ATT_DH)
    v_b = v_b.reshape(b, t, ATT_HEADS, ATT_DH)
    if k_prev is None:
        o_b = _band_attention_prompt(q_b, k_b, v_b, att_rel_bias)
        keep = min(BAND, t)
        k_new, v_new = k_b[:, t - keep:], v_b[:, t - keep:]
    else:
        o_b = _band_attention_step(q_b, k_b, v_b, k_prev, v_prev, att_rel_bias)
        k_new, v_new = k_b, v_b
    y_b = o_b.reshape(b, t, ATT_W) @ w_branch_b

    mixed = jax.nn.sigmoid(g_a) * y_a + jax.nn.sigmoid(g_b) * y_b
    x1 = _layernorm(DEEPNORM_ALPHA * x + mixed @ w_out, ln1_g, ln1_b)

    f, fconv_new = _conv_ffn(x1, w_ffn_up, ffn_conv_w, w_ffn_down, fconv_prev)
    r = DEEPNORM_ALPHA * x1 + f
    r = r + jax.nn.sigmoid(r @ w_ple_gate) * (pe @ w_ple)
    x2 = _layernorm(r, ln2_g, ln2_b)
    return x2, (gconv_new, s_gdn_new, k_new, v_new, fconv_new)


def setup_inputs(seed: int = 0) -> dict:
    key = jax.random.key(seed)
    ks = iter(jax.random.split(key, 48))
    f32 = jnp.float32

    def nrm(shape, scale):
        return jax.random.normal(next(ks), shape, f32) * scale

    att_cache = min(BAND, PAST_LEN)
    s_in = D_MODEL ** -0.5
    beta_s = DEEPNORM_BETA
    x_prompt = nrm((BATCH, SEQ, D_MODEL), 1.0)
    x_sample = nrm((DEC_BATCH, DEC_SEQ, D_MODEL), 1.0)
    state_gdn_conv = nrm((DEPTH, DEC_BATCH, GDN_CONV - 1, GDN_CONV_CH), 1.0)
    state_gdn = nrm((DEPTH, DEC_BATCH, GDN_HEADS, GDN_DK, GDN_DV), GDN_DK ** -0.5)
    cache_attn_k = nrm((DEPTH, DEC_BATCH, att_cache, ATT_HEADS, ATT_DH), 1.0)
    cache_attn_v = nrm((DEPTH, DEC_BATCH, att_cache, ATT_HEADS, ATT_DH), beta_s)
    state_ffn_conv = nrm((DEPTH, DEC_BATCH, FFN_CONV - 1, D_FF), beta_s)
    p_prompt = nrm((DEPTH, BATCH, SEQ, PLE_DIM), 1.0)
    p_sample = nrm((DEPTH, DEC_BATCH, DEC_SEQ, PLE_DIM), 1.0)
    ln_in_g = 1.0 + nrm((D_MODEL,), 0.02)
    ln_in_b = nrm((D_MODEL,), 0.02)
    w_in = jnp.concatenate([
        nrm((DEPTH, D_MODEL, GDN_QK), s_in),
        nrm((DEPTH, D_MODEL, GDN_QK), s_in),
        nrm((DEPTH, D_MODEL, GDN_V), s_in * beta_s),
        nrm((DEPTH, D_MODEL, GDN_V), s_in),
        nrm((DEPTH, D_MODEL, GDN_HEADS), s_in),
        nrm((DEPTH, D_MODEL, GDN_HEADS), s_in),
        nrm((DEPTH, D_MODEL, ATT_W), s_in),
        nrm((DEPTH, D_MODEL, ATT_W), s_in),
        nrm((DEPTH, D_MODEL, ATT_W), s_in * beta_s),
        nrm((DEPTH, D_MODEL, D_MODEL), s_in),
        nrm((DEPTH, D_MODEL, D_MODEL), s_in),
    ], axis=-1)
    gdn_conv_w = nrm((DEPTH, GDN_CONV, GDN_CONV_CH), GDN_CONV ** -0.5)
    gdn_a_log = jnp.log(jax.random.uniform(next(ks), (DEPTH, GDN_HEADS), f32, 1.0, 16.0))
    dt = jnp.exp(jax.random.uniform(next(ks), (DEPTH, GDN_HEADS), f32, math.log(1e-3), math.log(0.1)))
    gdn_dt_bias = dt + jnp.log(-jnp.expm1(-dt))
    gdn_norm_w = 1.0 + nrm((DEPTH, GDN_DV), 0.02)
    att_rel_bias = nrm((DEPTH, ATT_HEADS, 2 * REL_CLIP + 1), 0.1)
    w_branch_a = nrm((DEPTH, GDN_V, D_MODEL), GDN_V ** -0.5)
    w_branch_b = nrm((DEPTH, ATT_W, D_MODEL), ATT_W ** -0.5)
    w_out = nrm((DEPTH, D_MODEL, D_MODEL), s_in * beta_s)
    ln1_g = 1.0 + nrm((DEPTH, D_MODEL), 0.02)
    ln1_b = nrm((DEPTH, D_MODEL), 0.02)
    w_ffn_up = nrm((DEPTH, D_MODEL, 2 * D_FF), s_in * beta_s)
    ffn_conv_w = nrm((DEPTH, FFN_CONV, D_FF), FFN_CONV ** -0.5)
    w_ffn_down = nrm((DEPTH, D_FF, D_MODEL), D_FF ** -0.5 * beta_s)
    w_ple = nrm((DEPTH, PLE_DIM, D_MODEL), PLE_DIM ** -0.5 * beta_s)
    w_ple_gate = nrm((DEPTH, D_MODEL, D_MODEL), s_in)
    ln2_g = 1.0 + nrm((DEPTH, D_MODEL), 0.02)
    ln2_b = nrm((DEPTH, D_MODEL), 0.02)
    return {'x_prompt': x_prompt, 'x_sample': x_sample,
            'state_gdn_conv': state_gdn_conv, 'state_gdn': state_gdn,
            'cache_attn_k': cache_attn_k, 'cache_attn_v': cache_attn_v, 'state_ffn_conv': state_ffn_conv,
            'p_prompt': p_prompt, 'p_sample': p_sample,
            'ln_in_g': ln_in_g, 'ln_in_b': ln_in_b, 'w_in': w_in,
            'gdn_conv_w': gdn_conv_w, 'gdn_a_log': gdn_a_log, 'gdn_dt_bias': gdn_dt_bias, 'gdn_norm_w': gdn_norm_w,
            'att_rel_bias': att_rel_bias, 'w_branch_a': w_branch_a, 'w_branch_b': w_branch_b, 'w_out': w_out,
            'ln1_g': ln1_g, 'ln1_b': ln1_b, 'w_ffn_up': w_ffn_up, 'ffn_conv_w': ffn_conv_w, 'w_ffn_down': w_ffn_down,
            'w_ple': w_ple, 'w_ple_gate': w_ple_gate, 'ln2_g': ln2_g, 'ln2_b': ln2_b}


def reference(x_prompt, x_sample, state_gdn_conv, state_gdn, cache_attn_k, cache_attn_v, state_ffn_conv,
              p_prompt, p_sample, ln_in_g, ln_in_b, w_in, gdn_conv_w, gdn_a_log, gdn_dt_bias, gdn_norm_w,
              att_rel_bias, w_branch_a, w_branch_b, w_out, ln1_g, ln1_b, w_ffn_up, ffn_conv_w, w_ffn_down,
              w_ple, w_ple_gate, ln2_g, ln2_b):
    xp = _layernorm(x_prompt, ln_in_g, ln_in_b)
    xs = _layernorm(x_sample, ln_in_g, ln_in_b)
    bp = xp.shape[0]
    new_p, new_s = [], []
    for i in range(DEPTH):
        lw = (w_in[i], gdn_conv_w[i], gdn_a_log[i], gdn_dt_bias[i], gdn_norm_w[i], att_rel_bias[i],
              w_branch_a[i], w_branch_b[i], w_out[i], ln1_g[i], ln1_b[i], w_ffn_up[i], ffn_conv_w[i],
              w_ffn_down[i], w_ple[i], w_ple_gate[i], ln2_g[i], ln2_b[i])
        xp, st_p = _layer(xp, p_prompt[i],
                          jnp.zeros((bp, GDN_CONV - 1, GDN_CONV_CH), xp.dtype),
                          jnp.zeros((bp, GDN_HEADS, GDN_DK, GDN_DV), jnp.float32),
                          None, None,
                          jnp.zeros((bp, FFN_CONV - 1, D_FF), xp.dtype), lw)
        xs, st_s = _layer(xs, p_sample[i], state_gdn_conv[i], state_gdn[i], cache_attn_k[i], cache_attn_v[i],
                          state_ffn_conv[i], lw)
        new_p.append(st_p)
        new_s.append(st_s)
    p_gconv, p_gdn, p_k, p_v, p_fconv = [jnp.stack(a) for a in zip(*new_p)]
    s_gconv, s_gdn, s_k, s_v, s_fconv = [jnp.stack(a) for a in zip(*new_s)]
    return (xp, xs, p_gconv, p_gdn, p_k, p_v, p_fconv, s_gconv, s_gdn, s_k, s_v, s_fconv)
```

```python
import functools
import math

import jax
import jax.numpy as jnp
from jax import lax
from jax.experimental import pallas as pl
from jax.experimental.pallas import tpu as pltpu

F32 = jnp.float32
BF16 = jnp.bfloat16

D_MODEL = 1024
CHUNK = 64
BAND = 512
GDN_HEADS = 8
GDN_DK = 128
GDN_DV = 128
GDN_CONV = 4
ATT_HEADS = 8
ATT_DH = 64
REL_CLIP = 128
D_FF = 2816
FFN_CONV = 3
PLE_DIM = 256
DEEPNORM_ALPHA = 2.0 ** 0.25
LN_EPS = 1e-5
RMS_EPS = 1e-6
L2_EPS = 1e-6

GDN_W = GDN_HEADS * GDN_DK
GDN_CONV_CH = 3 * GDN_W
ATT_W = ATT_HEADS * ATT_DH
LANES = 128
SUBLANES = 8

COL_QKV = 0
COL_Z = 3 * GDN_W
COL_GA = 4 * GDN_W
COL_GB = 5 * GDN_W
COL_AQ = 6 * GDN_W
COL_AK = COL_AQ + ATT_W
COL_AV = COL_AK + ATT_W
H_COLS = COL_AV + ATT_W

VMEM_LIMIT = 56 * 1024 * 1024


def _cparams(sem):
    return pltpu.CompilerParams(dimension_semantics=sem, vmem_limit_bytes=VMEM_LIMIT)


def _sigmoid(x):
    return 1.0 / (1.0 + jnp.exp(-x))


def _silu(x):
    return x * _sigmoid(x)


def _layernorm_rows(x, g, b):
    mu = jnp.mean(x, axis=-1, keepdims=True)
    xc = x - mu
    var = jnp.mean(xc * xc, axis=-1, keepdims=True)
    return xc * lax.rsqrt(var + LN_EPS) * g + b


def _dot(a, b):
    return jnp.dot(a.astype(BF16), b.astype(BF16), preferred_element_type=F32)


def _dot_nt(a, b):
    return lax.dot_general(a.astype(BF16), b.astype(BF16), (((1,), (1,)), ((), ())),
                           preferred_element_type=F32)


def _dot_tn(a, b):
    return lax.dot_general(a.astype(BF16), b.astype(BF16), (((0,), (0,)), ((), ())),
                           preferred_element_type=F32)


def _split3(x):
    x1 = x.astype(BF16)
    r1 = x - x1.astype(F32)
    x2 = r1.astype(BF16)
    x3 = (r1 - x2.astype(F32)).astype(BF16)
    return x1, x2, x3


def _in_proj_kernel(x_ref, g_ref, b_ref, w_ref, wab_ref, h_ref, ab_ref, xn_ref, *, sub):
    j = pl.program_id(1)
    tm = x_ref.shape[0]

    @pl.when(j == 0)
    def _():
        def body(i, c):
            r = pl.multiple_of(i * sub, sub)
            xn = _layernorm_rows(x_ref[pl.ds(r, sub), :], g_ref[...], b_ref[...])
            xn_ref[pl.ds(r, sub), :] = xn.astype(BF16)
            return c
        lax.fori_loop(0, tm // sub, body, 0)
        ab_ref[...] = jnp.dot(xn_ref[...], wab_ref[...], preferred_element_type=F32)

    h_ref[...] = jnp.dot(xn_ref[...], w_ref[...], preferred_element_type=F32).astype(BF16)


def _in_proj(x2d, ln_g, ln_b, w_main, w_ab, *, tm, tn):
    rows = x2d.shape[0]
    assert rows % tm == 0 and H_COLS % tn == 0
    sub = min(tm, 128)
    return pl.pallas_call(
        functools.partial(_in_proj_kernel, sub=sub),
        grid=(rows // tm, H_COLS // tn),
        in_specs=[
            pl.BlockSpec((tm, D_MODEL), lambda i, j: (i, 0)),
            pl.BlockSpec((1, D_MODEL), lambda i, j: (0, 0)),
            pl.BlockSpec((1, D_MODEL), lambda i, j: (0, 0)),
            pl.BlockSpec((D_MODEL, tn), lambda i, j: (0, j)),
            pl.BlockSpec((D_MODEL, LANES), lambda i, j: (0, 0)),
        ],
        out_specs=[
            pl.BlockSpec((tm, tn), lambda i, j: (i, j)),
            pl.BlockSpec((tm, LANES), lambda i, j: (i, 0)),
        ],
        out_shape=[
            jax.ShapeDtypeStruct((rows, H_COLS), BF16),
            jax.ShapeDtypeStruct((rows, LANES), F32),
        ],
        scratch_shapes=[pltpu.VMEM((tm, D_MODEL), BF16)],
        compiler_params=_cparams(("arbitrary", "arbitrary")),
        name="in_proj",
    )(x2d, ln_g, ln_b, w_main, w_ab)


def _tri_inverse(m, c):
    row = lax.broadcasted_iota(jnp.int32, (c, c), 0)
    col = lax.broadcasted_iota(jnp.int32, (c, c), 1)
    p = jnp.where(row == col, 1.0, 0.0).astype(F32) - m
    mk = m
    for _ in range(int(math.log2(c)) - 1):
        mk = _dot(mk, mk)
        p = p + _dot(p, mk)
    return p


def _gdn_kernel(qkv_ref, z_ref, ab_ref, cw_ref, alog_ref, dtb_ref, nw_ref, cinit_ref, s0_ref,
                o_ref, cstate_ref, s_ref, xp_ref, *, c):
    n = pl.program_id(1)
    pad = SUBLANES

    @pl.when(n == 0)
    def _():
        xp_ref[0:pad, :] = cinit_ref[0]
        s_ref[...] = s0_ref[...]

    xp_ref[pad:pad + c, :] = qkv_ref[...].astype(F32)

    row = lax.broadcasted_iota(jnp.int32, (c, c), 0)
    col = lax.broadcasted_iota(jnp.int32, (c, c), 1)
    causal = row >= col
    strict = row > col

    ab = ab_ref[...]
    sp_in = ab + dtb_ref[...]
    softplus = jnp.maximum(sp_in, 0.0) + jnp.log(1.0 + jnp.exp(-jnp.abs(sp_in)))
    g_all = -jnp.exp(alog_ref[...]) * softplus
    beta_all = _sigmoid(ab)
    g1, g2, g3 = _split3(g_all)
    tri = jnp.where(causal, 1.0, 0.0).astype(BF16)
    dec_all = (jnp.dot(tri, g1, preferred_element_type=F32)
               + (jnp.dot(tri, g2, preferred_element_type=F32)
                  + jnp.dot(tri, g3, preferred_element_type=F32)))
    dec_t = dec_all.T

    def conv_silu(c0):
        acc = xp_ref[pl.ds(pad, c), pl.ds(c0, GDN_DK)] * cw_ref[GDN_CONV - 1:GDN_CONV, pl.ds(c0, GDN_DK)]
        for i in range(1, GDN_CONV):
            acc = acc + (xp_ref[pl.ds(pad - i, c), pl.ds(c0, GDN_DK)]
                         * cw_ref[GDN_CONV - 1 - i:GDN_CONV - i, pl.ds(c0, GDN_DK)])
        return _silu(acc)

    for h in range(GDN_HEADS):
        xq = conv_silu(h * GDN_DK)
        xk = conv_silu(GDN_W + h * GDN_DK)
        xv = conv_silu(2 * GDN_W + h * GDN_DV)
        qn = xq * lax.rsqrt(jnp.sum(xq * xq, axis=-1, keepdims=True) + L2_EPS) * (GDN_DK ** -0.5)
        kn = xk * lax.rsqrt(jnp.sum(xk * xk, axis=-1, keepdims=True) + L2_EPS)
        dcol = dec_all[:, h:h + 1]
        drow = dec_t[h:h + 1, :]
        beta = beta_all[:, GDN_HEADS + h:GDN_HEADS + h + 1]
        lmask = jnp.exp(jnp.where(causal, dcol - drow, -jnp.inf))
        kb = kn * beta
        e_d = jnp.exp(dcol)
        gram = _dot_nt(jnp.concatenate([kb, qn], axis=0), kn)
        m = jnp.where(strict, gram[:c] * lmask, 0.0)
        qk = gram[c:] * lmask
        t_inv = _tri_inverse(m, c)
        sol = _dot(t_inv, jnp.concatenate([xv * beta, kb * e_d], axis=1))
        u = sol[:, :GDN_DV]
        w = sol[:, GDN_DV:]
        s = s_ref[0, h]
        ws_qs = _dot(jnp.concatenate([w, qn * e_d], axis=0), s)
        v_new = u - ws_qs[:c]
        o = ws_qs[c:] + _dot(qk, v_new)
        dl = dcol[c - 1:c, :]
        kd = kn * jnp.exp(dl - dcol)
        s_ref[0, h] = s * jnp.exp(dl) + _dot_tn(kd, v_new)
        zh = z_ref[:, pl.ds(h * GDN_DV, GDN_DV)].astype(F32)
        o = (o * lax.rsqrt(jnp.mean(o * o, axis=-1, keepdims=True) + RMS_EPS) * nw_ref[...]
             * _silu(zh))
        o_ref[:, pl.ds(h * GDN_DV, GDN_DV)] = o.astype(BF16)

    tail = xp_ref[c:c + pad, :]
    xp_ref[0:pad, :] = tail
    cstate_ref[0] = tail


def _gdn(h2d, ab, conv_w, a_log, dt_bias, norm_w, conv_init, s0, *, batch, t):
    c = min(CHUNK, t)
    nc = t // c
    rows = batch * t
    return pl.pallas_call(
        functools.partial(_gdn_kernel, c=c),
        grid=(batch, nc),
        in_specs=[
            pl.BlockSpec((c, GDN_CONV_CH), lambda b, n: (b * nc + n, COL_QKV // GDN_CONV_CH)),
            pl.BlockSpec((c, GDN_W), lambda b, n: (b * nc + n, COL_Z // GDN_W)),
            pl.BlockSpec((c, LANES), lambda b, n: (b * nc + n, 0)),
            pl.BlockSpec((GDN_CONV, GDN_CONV_CH), lambda b, n: (0, 0)),
            pl.BlockSpec((1, LANES), lambda b, n: (0, 0)),
            pl.BlockSpec((1, LANES), lambda b, n: (0, 0)),
            pl.BlockSpec((1, GDN_DV), lambda b, n: (0, 0)),
            pl.BlockSpec((1, SUBLANES, GDN_CONV_CH), lambda b, n: (b, 0, 0)),
            pl.BlockSpec((1, GDN_HEADS, GDN_DK, GDN_DV), lambda b, n: (b, 0, 0, 0)),
        ],
        out_specs=[
            pl.BlockSpec((c, GDN_W), lambda b, n: (b * nc + n, 0)),
            pl.BlockSpec((1, SUBLANES, GDN_CONV_CH), lambda b, n: (b, 0, 0)),
            pl.BlockSpec((1, GDN_HEADS, GDN_DK, GDN_DV), lambda b, n: (b, 0, 0, 0)),
        ],
        out_shape=[
            jax.ShapeDtypeStruct((rows, GDN_W), BF16),
            jax.ShapeDtypeStruct((batch, SUBLANES, GDN_CONV_CH), F32),
            jax.ShapeDtypeStruct((batch, GDN_HEADS, GDN_DK, GDN_DV), F32),
        ],
        scratch_shapes=[pltpu.VMEM((SUBLANES + c, GDN_CONV_CH), F32)],
        compiler_params=_cparams(("arbitrary", "arbitrary")),
        name="gdn",
    )(h2d, h2d, ab, conv_w, a_log, dt_bias, norm_w, conv_init, s0)


def _attn_kernel(q_ref, k_ref, v_ref, bias_ref, o_ref, *, tq, width, pad):
    n = pl.program_id(1)
    start = pl.multiple_of(n * tq, tq)
    kwin = k_ref[0, pl.ds(start, width), :]
    vwin = v_ref[0, pl.ds(start, width), :]
    kpos = start - pad + lax.broadcasted_iota(jnp.int32, (tq, width), 1)
    valid = kpos >= 0
    scale = ATT_DH ** -0.5
    outs = []
    for h in range(ATT_HEADS):
        lo = h * ATT_DH
        qh = q_ref[:, lo:lo + ATT_DH]
        s = _dot_nt(qh, kwin[:, lo:lo + ATT_DH]) * scale + bias_ref[h]
        s = jnp.where(valid, s, -jnp.inf)
        mx = jnp.max(s, axis=-1, keepdims=True)
        p = jnp.exp(s - mx)
        l = jnp.sum(p, axis=-1, keepdims=True)
        outs.append(_dot(p, vwin[:, lo:lo + ATT_DH]) / l)
    o_ref[...] = jnp.concatenate(outs, axis=1).astype(BF16)


def _attn(q2d, q_col_block, k3d, v3d, bias, *, batch, t, tq, width, pad):
    nq = t // tq
    tk = k3d.shape[1]
    return pl.pallas_call(
        functools.partial(_attn_kernel, tq=tq, width=width, pad=pad),
        grid=(batch, nq),
        in_specs=[
            pl.BlockSpec((tq, ATT_W), lambda b, n: (b * nq + n, q_col_block)),
            pl.BlockSpec((1, tk, ATT_W), lambda b, n: (b, 0, 0)),
            pl.BlockSpec((1, tk, ATT_W), lambda b, n: (b, 0, 0)),
            pl.BlockSpec((ATT_HEADS, tq, width), lambda b, n: (0, 0, 0)),
        ],
        out_specs=pl.BlockSpec((tq, ATT_W), lambda b, n: (b * nq + n, 0)),
        out_shape=jax.ShapeDtypeStruct((batch * t, ATT_W), BF16),
        compiler_params=_cparams(("arbitrary", "arbitrary")),
        name="attn",
    )(q2d, k3d, v3d, bias)


def _merge_kernel(x_ref, g_ref, b_ref, oa_ref, ob_ref, ga_ref, gb_ref, wa_ref, wb_ref, wo_ref,
                  g1_ref, b1_ref, x1_ref):
    xn = _layernorm_rows(x_ref[...], g_ref[...], b_ref[...])
    ya = jnp.dot(oa_ref[...], wa_ref[...], preferred_element_type=F32)
    yb = jnp.dot(ob_ref[...], wb_ref[...], preferred_element_type=F32)
    mixed = (_sigmoid(ga_ref[...].astype(F32)) * ya + _sigmoid(gb_ref[...].astype(F32)) * yb)
    y = DEEPNORM_ALPHA * xn + _dot(mixed, wo_ref[...])
    x1_ref[...] = _layernorm_rows(y, g1_ref[...], b1_ref[...])


def _merge(x2d, ln_g, ln_b, oa, ob, h2d, wa, wb, wo, ln1_g, ln1_b, *, tm):
    rows = x2d.shape[0]
    full = lambda i: (0, 0)
    return pl.pallas_call(
        _merge_kernel,
        grid=(rows // tm,),
        in_specs=[
            pl.BlockSpec((tm, D_MODEL), lambda i: (i, 0)),
            pl.BlockSpec((1, D_MODEL), full),
            pl.BlockSpec((1, D_MODEL), full),
            pl.BlockSpec((tm, GDN_W), lambda i: (i, 0)),
            pl.BlockSpec((tm, ATT_W), lambda i: (i, 0)),
            pl.BlockSpec((tm, D_MODEL), lambda i: (i, COL_GA // D_MODEL)),
            pl.BlockSpec((tm, D_MODEL), lambda i: (i, COL_GB // D_MODEL)),
            pl.BlockSpec((GDN_W, D_MODEL), full),
            pl.BlockSpec((ATT_W, D_MODEL), full),
            pl.BlockSpec((D_MODEL, D_MODEL), full),
            pl.BlockSpec((1, D_MODEL), full),
            pl.BlockSpec((1, D_MODEL), full),
        ],
        out_specs=pl.BlockSpec((tm, D_MODEL), lambda i: (i, 0)),
        out_shape=jax.ShapeDtypeStruct((rows, D_MODEL), F32),
        compiler_params=_cparams(("arbitrary",)),
        name="merge",
    )(x2d, ln_g, ln_b, oa, ob, h2d, h2d, wa, wb, wo, ln1_g, ln1_b)


def _gelu_tanh(x):
    return 0.5 * x * (1.0 + jnp.tanh(math.sqrt(2.0 / math.pi) * (x + 0.044715 * (x * x * x))))


def _ffn_kernel(x1_ref, pe_ref, wup_ref, cw_ref, wdn_ref, wple_ref, wpg_ref, g2_ref, b2_ref,
                finit_ref, y_ref, fstate_ref, gp_ref, acc_ref, *, tm, fc):
    n = pl.program_id(1)
    pad = SUBLANES

    @pl.when(n == 0)
    def _():
        gp_ref[0:pad, :] = finit_ref[0]

    x1 = x1_ref[...]
    x1b = x1.astype(BF16)
    acc_ref[...] = jnp.zeros_like(acc_ref)
    for k in range(D_FF // fc):
        c0 = k * fc
        gate = jnp.dot(x1b, wup_ref[:, c0:c0 + fc], preferred_element_type=F32)
        val = jnp.dot(x1b, wup_ref[:, D_FF + c0:D_FF + c0 + fc], preferred_element_type=F32)
        gp_ref[pad:pad + tm, c0:c0 + fc] = gate
        conv = gate * cw_ref[FFN_CONV - 1:FFN_CONV, c0:c0 + fc]
        for i in range(1, FFN_CONV):
            conv = conv + (gp_ref[pl.ds(pad - i, tm), c0:c0 + fc]
                           * cw_ref[FFN_CONV - 1 - i:FFN_CONV - i, c0:c0 + fc])
        act = (_gelu_tanh(conv) * val).astype(BF16)
        acc_ref[...] += jnp.dot(act, wdn_ref[c0:c0 + fc, :], preferred_element_type=F32)
    tail = gp_ref[tm:tm + pad, :]
    gp_ref[0:pad, :] = tail
    fstate_ref[0] = tail

    r = DEEPNORM_ALPHA * x1 + acc_ref[...]
    gate_p = _sigmoid(_dot(r, wpg_ref[...]))
    r = r + gate_p * jnp.dot(pe_ref[...], wple_ref[...], preferred_element_type=F32)
    y_ref[...] = _layernorm_rows(r, g2_ref[...], b2_ref[...])


def _ffn(x1, pe, wup, conv_w, wdn, wple, wpg, ln2_g, ln2_b, finit, *, batch, t, tm, fc):
    nt = t // tm
    rows = batch * t
    full = lambda b, n: (0, 0)
    return pl.pallas_call(
        functools.partial(_ffn_kernel, tm=tm, fc=fc),
        grid=(batch, nt),
        in_specs=[
            pl.BlockSpec((tm, D_MODEL), lambda b, n: (b * nt + n, 0)),
            pl.BlockSpec((tm, PLE_DIM), lambda b, n: (b * nt + n, 0)),
            pl.BlockSpec((D_MODEL, 2 * D_FF), full, pipeline_mode=pl.Buffered(1)),
            pl.BlockSpec((FFN_CONV, D_FF), full),
            pl.BlockSpec((D_FF, D_MODEL), full, pipeline_mode=pl.Buffered(1)),
            pl.BlockSpec((PLE_DIM, D_MODEL), full),
            pl.BlockSpec((D_MODEL, D_MODEL), full),
            pl.BlockSpec((1, D_MODEL), full),
            pl.BlockSpec((1, D_MODEL), full),
            pl.BlockSpec((1, SUBLANES, D_FF), lambda b, n: (b, 0, 0)),
        ],
        out_specs=[
            pl.BlockSpec((tm, D_MODEL), lambda b, n: (b * nt + n, 0)),
            pl.BlockSpec((1, SUBLANES, D_FF), lambda b, n: (b, 0, 0)),
        ],
        out_shape=[
            jax.ShapeDtypeStruct((rows, D_MODEL), F32),
            jax.ShapeDtypeStruct((batch, SUBLANES, D_FF), F32),
        ],
        scratch_shapes=[pltpu.VMEM((SUBLANES + tm, D_FF), F32), pltpu.VMEM((tm, D_MODEL), F32)],
        compiler_params=_cparams(("arbitrary", "arbitrary")),
        name="ffn",
    )(x1, pe, wup, conv_w, wdn, wple, wpg, ln2_g, ln2_b, finit)


def _rel_bias(table, n_q, n_past, n_k):
    rel = jnp.arange(n_q)[:, None] + n_past - jnp.arange(n_k)[None, :]
    idx = jnp.clip(rel, -REL_CLIP, REL_CLIP) + REL_CLIP
    return table[:, idx].astype(F32)


def _pad_state(state, rows):
    b, r, ch = state.shape
    return jnp.concatenate([jnp.zeros((b, rows - r, ch), F32), state.astype(F32)], axis=1)


def _layer(x, pe, gconv_prev, s_gdn, k_prev, v_prev, fconv_prev, wts, *, tm_proj, tm_merge, tm_ffn):
    (ln_in_g, ln_in_b, w_main, w_ab, gdn_conv_w, a_log, dt_bias, norm_w, rel_table, w_a, w_b, w_o,
     ln1_g, ln1_b, w_up, ffn_conv_w, w_dn, w_ple, w_pg, ln2_g, ln2_b) = wts
    b, t, _ = x.shape
    rows = b * t
    x2d = x.reshape(rows, D_MODEL)

    h2d, ab = _in_proj(x2d, ln_in_g, ln_in_b, w_main, w_ab, tm=min(tm_proj, rows), tn=1536)

    o_a, gconv_new, s_new = _gdn(h2d, ab, gdn_conv_w, a_log, dt_bias, norm_w,
                                 _pad_state(gconv_prev, SUBLANES), s_gdn.astype(F32), batch=b, t=t)

    k_b = h2d[:, COL_AK:COL_AK + ATT_W].reshape(b, t, ATT_W)
    v_b = h2d[:, COL_AV:COL_AV + ATT_W].reshape(b, t, ATT_W)
    if k_prev is None:
        kp = jnp.pad(k_b, ((0, 0), (BAND, 0), (0, 0)))
        vp = jnp.pad(v_b, ((0, 0), (BAND, 0), (0, 0)))
        bias = _rel_bias(rel_table, CHUNK, BAND, BAND + CHUNK)
        o_b = _attn(h2d, COL_AQ // ATT_W, kp, vp, bias, batch=b, t=t, tq=CHUNK,
                    width=BAND + CHUNK, pad=BAND)
        keep = min(BAND, t)
        k_new, v_new = k_b[:, t - keep:], v_b[:, t - keep:]
    else:
        n_past = k_prev.shape[1]
        kk = jnp.concatenate([k_prev.reshape(b, n_past, ATT_W).astype(BF16), k_b], axis=1)
        vv = jnp.concatenate([v_prev.reshape(b, n_past, ATT_W).astype(BF16), v_b], axis=1)
        bias = _rel_bias(rel_table, t, n_past, n_past + t)
        o_b = _attn(h2d, COL_AQ // ATT_W, kk, vv, bias, batch=b, t=t, tq=t, width=n_past + t, pad=0)
        k_new, v_new = k_b, v_b

    x1 = _merge(x2d, ln_in_g, ln_in_b, o_a, o_b, h2d, w_a, w_b, w_o, ln1_g, ln1_b,
                tm=min(tm_merge, rows))

    y, fconv_new = _ffn(x1, pe.reshape(rows, PLE_DIM).astype(BF16), w_up, ffn_conv_w, w_dn, w_ple, w_pg,
                        ln2_g, ln2_b, _pad_state(fconv_prev, SUBLANES), batch=b, t=t,
                        tm=min(tm_ffn, t), fc=256)

    keep_k = k_new.shape[1]
    return (y.reshape(b, t, D_MODEL),
            gconv_new[:, SUBLANES - (GDN_CONV - 1):],
            s_new,
            k_new.astype(F32).reshape(b, keep_k, ATT_HEADS, ATT_DH),
            v_new.astype(F32).reshape(b, keep_k, ATT_HEADS, ATT_DH),
            fconv_new[:, SUBLANES - (FFN_CONV - 1):])


def _pad_lanes(v, width=LANES):
    return jnp.pad(v.astype(F32), (0, width - v.shape[0])).reshape(1, width)


def kernel(x_prompt, x_sample, state_gdn_conv, state_gdn, cache_attn_k, cache_attn_v, state_ffn_conv,
           p_prompt, p_sample, ln_in_g, ln_in_b, w_in, gdn_conv_w, gdn_a_log, gdn_dt_bias, gdn_norm_w,
           att_rel_bias, w_branch_a, w_branch_b, w_out, ln1_g, ln1_b, w_ffn_up, ffn_conv_w, w_ffn_down,
           w_ple, w_ple_gate, ln2_g, ln2_b):
    depth = w_in.shape[0]
    assert depth == 1
    bp = x_prompt.shape[0]
    row = lambda v: v.reshape(1, -1).astype(F32)

    xp, xs = x_prompt, x_sample
    new_p, new_s = [], []
    for i in range(depth):
        wi = w_in[i]
        o_ab = 4 * GDN_W
        o_att = o_ab + 2 * GDN_HEADS
        o_gate = o_att + 3 * ATT_W
        w_main = jnp.concatenate([wi[:, :o_ab], wi[:, o_gate:], wi[:, o_att:o_gate]], axis=1).astype(BF16)
        w_ab = jnp.pad(wi[:, o_ab:o_att], ((0, 0), (0, LANES - 2 * GDN_HEADS))).astype(BF16)
        wts = (row(ln_in_g), row(ln_in_b), w_main, w_ab, gdn_conv_w[i].astype(F32),
               _pad_lanes(gdn_a_log[i]), _pad_lanes(gdn_dt_bias[i]), row(gdn_norm_w[i]), att_rel_bias[i],
               w_branch_a[i].astype(BF16), w_branch_b[i].astype(BF16), w_out[i].astype(BF16),
               row(ln1_g[i]), row(ln1_b[i]), w_ffn_up[i].astype(BF16), ffn_conv_w[i].astype(F32),
               w_ffn_down[i].astype(BF16), w_ple[i].astype(BF16), w_ple_gate[i].astype(BF16),
               row(ln2_g[i]), row(ln2_b[i]))
        xp, st_p = _unpack(_layer(xp, p_prompt[i],
                                  jnp.zeros((bp, GDN_CONV - 1, GDN_CONV_CH), F32),
                                  jnp.zeros((bp, GDN_HEADS, GDN_DK, GDN_DV), F32),
                                  None, None,
                                  jnp.zeros((bp, FFN_CONV - 1, D_FF), F32), wts,
                                  tm_proj=1024, tm_merge=512, tm_ffn=256))
        xs, st_s = _unpack(_layer(xs, p_sample[i], state_gdn_conv[i], state_gdn[i], cache_attn_k[i],
                                  cache_attn_v[i], state_ffn_conv[i], wts,
                                  tm_proj=256, tm_merge=256, tm_ffn=256))
        new_p.append(st_p)
        new_s.append(st_s)
    p_gconv, p_gdn, p_k, p_v, p_fconv = [jnp.stack(a) for a in zip(*new_p)]
    s_gconv, s_gdn, s_k, s_v, s_fconv = [jnp.stack(a) for a in zip(*new_s)]
    return (xp, xs, p_gconv, p_gdn, p_k, p_v, p_fconv, s_gconv, s_gdn, s_k, s_v, s_fconv)


def _unpack(res):
    return res[0], res[1:]
```

```python
import functools
import math

import jax
import jax.numpy as jnp
from jax import lax
from jax.experimental import pallas as pl
from jax.experimental.pallas import tpu as pltpu

F32 = jnp.float32
BF16 = jnp.bfloat16

D_MODEL = 1024
CHUNK = 64
BAND = 512
GDN_HEADS = 8
GDN_DK = 128
GDN_DV = 128
GDN_CONV = 4
ATT_HEADS = 8
ATT_DH = 64
REL_CLIP = 128
D_FF = 2816
FFN_CONV = 3
PLE_DIM = 256
DEEPNORM_ALPHA = 2.0 ** 0.25
LN_EPS = 1e-5
RMS_EPS = 1e-6
L2_EPS = 1e-6

GDN_W = GDN_HEADS * GDN_DK
GDN_CONV_CH = 3 * GDN_W
ATT_W = ATT_HEADS * ATT_DH
LANES = 128
SUBLANES = 8

COL_QKV = 0
COL_Z = 3 * GDN_W
COL_GA = 4 * GDN_W
COL_GB = 5 * GDN_W
COL_AQ = 6 * GDN_W
COL_AK = COL_AQ + ATT_W
COL_AV = COL_AK + ATT_W
H_COLS = COL_AV + ATT_W

VMEM_LIMIT = 56 * 1024 * 1024


def _cparams(sem):
    return pltpu.CompilerParams(dimension_semantics=sem, vmem_limit_bytes=VMEM_LIMIT)


def _sigmoid(x):
    return 1.0 / (1.0 + jnp.exp(-x))


def _silu(x):
    return x * _sigmoid(x)


def _layernorm_rows(x, g, b):
    mu = jnp.mean(x, axis=-1, keepdims=True)
    xc = x - mu
    var = jnp.mean(xc * xc, axis=-1, keepdims=True)
    return xc * lax.rsqrt(var + LN_EPS) * g + b


def _dot(a, b):
    return jnp.dot(a.astype(BF16), b.astype(BF16), preferred_element_type=F32)


def _dot_nt(a, b):
    return lax.dot_general(a.astype(BF16), b.astype(BF16), (((1,), (1,)), ((), ())),
                           preferred_element_type=F32)


def _dot_tn(a, b):
    return lax.dot_general(a.astype(BF16), b.astype(BF16), (((0,), (0,)), ((), ())),
                           preferred_element_type=F32)


def _split3(x):
    x1 = x.astype(BF16)
    r1 = x - x1.astype(F32)
    x2 = r1.astype(BF16)
    x3 = (r1 - x2.astype(F32)).astype(BF16)
    return x1, x2, x3


def _in_proj_kernel(x_ref, g_ref, b_ref, w_ref, wab_ref, h_ref, ab_ref, xn_ref, *, sub):
    j = pl.program_id(1)
    tm = x_ref.shape[0]

    @pl.when(j == 0)
    def _():
        def body(i, c):
            r = pl.multiple_of(i * sub, sub)
            xn = _layernorm_rows(x_ref[pl.ds(r, sub), :], g_ref[...], b_ref[...])
            xn_ref[pl.ds(r, sub), :] = xn.astype(BF16)
            return c
        lax.fori_loop(0, tm // sub, body, 0)
        ab_ref[...] = jnp.dot(xn_ref[...], wab_ref[...], preferred_element_type=F32)

    h_ref[...] = jnp.dot(xn_ref[...], w_ref[...], preferred_element_type=F32).astype(BF16)


def _in_proj(x2d, ln_g, ln_b, w_main, w_ab, *, tm, tn):
    rows = x2d.shape[0]
    assert rows % tm == 0 and H_COLS % tn == 0
    sub = min(tm, 128)
    return pl.pallas_call(
        functools.partial(_in_proj_kernel, sub=sub),
        grid=(rows // tm, H_COLS // tn),
        in_specs=[
            pl.BlockSpec((tm, D_MODEL), lambda i, j: (i, 0)),
            pl.BlockSpec((1, D_MODEL), lambda i, j: (0, 0)),
            pl.BlockSpec((1, D_MODEL), lambda i, j: (0, 0)),
            pl.BlockSpec((D_MODEL, tn), lambda i, j: (0, j)),
            pl.BlockSpec((D_MODEL, LANES), lambda i, j: (0, 0)),
        ],
        out_specs=[
            pl.BlockSpec((tm, tn), lambda i, j: (i, j)),
            pl.BlockSpec((tm, LANES), lambda i, j: (i, 0)),
        ],
        out_shape=[
            jax.ShapeDtypeStruct((rows, H_COLS), BF16),
            jax.ShapeDtypeStruct((rows, LANES), F32),
        ],
        scratch_shapes=[pltpu.VMEM((tm, D_MODEL), BF16)],
        compiler_params=_cparams(("arbitrary", "arbitrary")),
        name="in_proj",
    )(x2d, ln_g, ln_b, w_main, w_ab)


def _gdn_kernel(qkv_ref, z_ref, ab_ref, cw_ref, alog_ref, dtb_ref, nw_ref, cinit_ref, s0_ref,
                o_ref, cstate_ref, s_ref, xp_ref, *, c):
    n = pl.program_id(1)
    pad = SUBLANES

    @pl.when(n == 0)
    def _():
        xp_ref[0:pad, :] = cinit_ref[0]
        s_ref[...] = s0_ref[...]

    xp_ref[pad:pad + c, :] = qkv_ref[...].astype(F32)

    row = lax.broadcasted_iota(jnp.int32, (c, c), 0)
    col = lax.broadcasted_iota(jnp.int32, (c, c), 1)
    causal = row >= col
    strict = row > col

    ab = ab_ref[...]
    sp_in = ab + dtb_ref[...]
    softplus = jnp.maximum(sp_in, 0.0) + jnp.log(1.0 + jnp.exp(-jnp.abs(sp_in)))
    g_all = -jnp.exp(alog_ref[...]) * softplus
    beta_all = _sigmoid(ab)
    g1, g2, g3 = _split3(g_all)
    tri = jnp.where(causal, 1.0, 0.0).astype(BF16)
    dec_all = (jnp.dot(tri, g1, preferred_element_type=F32)
               + (jnp.dot(tri, g2, preferred_element_type=F32)
                  + jnp.dot(tri, g3, preferred_element_type=F32)))
    dec_t = dec_all.T

    def conv_silu(c0):
        acc = xp_ref[pl.ds(pad, c), pl.ds(c0, GDN_DK)] * cw_ref[GDN_CONV - 1:GDN_CONV, pl.ds(c0, GDN_DK)]
        for i in range(1, GDN_CONV):
            acc = acc + (xp_ref[pl.ds(pad - i, c), pl.ds(c0, GDN_DK)]
                         * cw_ref[GDN_CONV - 1 - i:GDN_CONV - i, pl.ds(c0, GDN_DK)])
        return _silu(acc)

    heads = range(GDN_HEADS)
    eye = jnp.where(row == col, 1.0, 0.0).astype(F32)
    xq = [conv_silu(h * GDN_DK) for h in heads]
    xk = [conv_silu(GDN_W + h * GDN_DK) for h in heads]
    xv = [conv_silu(2 * GDN_W + h * GDN_DV) for h in heads]
    qn = [x * lax.rsqrt(jnp.sum(x * x, axis=-1, keepdims=True) + L2_EPS) * (GDN_DK ** -0.5) for x in xq]
    kn = [x * lax.rsqrt(jnp.sum(x * x, axis=-1, keepdims=True) + L2_EPS) for x in xk]
    dcol = [dec_all[:, h:h + 1] for h in heads]
    beta = [beta_all[:, GDN_HEADS + h:GDN_HEADS + h + 1] for h in heads]
    lmask = [jnp.exp(jnp.where(causal, dcol[h] - dec_t[h:h + 1, :], -jnp.inf)) for h in heads]
    kb = [kn[h] * beta[h] for h in heads]
    e_d = [jnp.exp(d) for d in dcol]
    gram = [_dot_nt(jnp.concatenate([kb[h], qn[h]], axis=0), kn[h]) for h in heads]
    m = [jnp.where(strict, gram[h][:c] * lmask[h], 0.0) for h in heads]
    qk = [gram[h][c:] * lmask[h] for h in heads]
    p = [eye - m[h] for h in heads]
    mk = m
    for _ in range(int(math.log2(c)) - 1):
        mk = [_dot(x, x) for x in mk]
        p = [p[h] + _dot(p[h], mk[h]) for h in heads]
    sol = [_dot(p[h], jnp.concatenate([xv[h] * beta[h], kb[h] * e_d[h]], axis=1)) for h in heads]
    s_old = [s_ref[0, h] for h in heads]
    ws_qs = [_dot(jnp.concatenate([sol[h][:, GDN_DV:], qn[h] * e_d[h]], axis=0), s_old[h])
             for h in heads]
    v_new = [sol[h][:, :GDN_DV] - ws_qs[h][:c] for h in heads]
    dl = [d[c - 1:c, :] for d in dcol]
    kd = [kn[h] * jnp.exp(dl[h] - dcol[h]) for h in heads]
    s_upd = [_dot_tn(kd[h], v_new[h]) for h in heads]
    o_in = [_dot(qk[h], v_new[h]) for h in heads]
    for h in heads:
        s_ref[0, h] = s_old[h] * jnp.exp(dl[h]) + s_upd[h]
        o = ws_qs[h][c:] + o_in[h]
        zh = z_ref[:, pl.ds(h * GDN_DV, GDN_DV)].astype(F32)
        o = (o * lax.rsqrt(jnp.mean(o * o, axis=-1, keepdims=True) + RMS_EPS) * nw_ref[...]
             * _silu(zh))
        o_ref[:, pl.ds(h * GDN_DV, GDN_DV)] = o.astype(BF16)

    tail = xp_ref[c:c + pad, :]
    xp_ref[0:pad, :] = tail
    cstate_ref[0] = tail


def _gdn(h2d, ab, conv_w, a_log, dt_bias, norm_w, conv_init, s0, *, batch, t):
    c = min(CHUNK, t)
    nc = t // c
    rows = batch * t
    return pl.pallas_call(
        functools.partial(_gdn_kernel, c=c),
        grid=(batch, nc),
        in_specs=[
            pl.BlockSpec((c, GDN_CONV_CH), lambda b, n: (b * nc + n, COL_QKV // GDN_CONV_CH)),
            pl.BlockSpec((c, GDN_W), lambda b, n: (b * nc + n, COL_Z // GDN_W)),
            pl.BlockSpec((c, LANES), lambda b, n: (b * nc + n, 0)),
            pl.BlockSpec((GDN_CONV, GDN_CONV_CH), lambda b, n: (0, 0)),
            pl.BlockSpec((1, LANES), lambda b, n: (0, 0)),
            pl.BlockSpec((1, LANES), lambda b, n: (0, 0)),
            pl.BlockSpec((1, GDN_DV), lambda b, n: (0, 0)),
            pl.BlockSpec((1, SUBLANES, GDN_CONV_CH), lambda b, n: (b, 0, 0)),
            pl.BlockSpec((1, GDN_HEADS, GDN_DK, GDN_DV), lambda b, n: (b, 0, 0, 0)),
        ],
        out_specs=[
            pl.BlockSpec((c, GDN_W), lambda b, n: (b * nc + n, 0)),
            pl.BlockSpec((1, SUBLANES, GDN_CONV_CH), lambda b, n: (b, 0, 0)),
            pl.BlockSpec((1, GDN_HEADS, GDN_DK, GDN_DV), lambda b, n: (b, 0, 0, 0)),
        ],
        out_shape=[
            jax.ShapeDtypeStruct((rows, GDN_W), BF16),
            jax.ShapeDtypeStruct((batch, SUBLANES, GDN_CONV_CH), F32),
            jax.ShapeDtypeStruct((batch, GDN_HEADS, GDN_DK, GDN_DV), F32),
        ],
        scratch_shapes=[pltpu.VMEM((SUBLANES + c, GDN_CONV_CH), F32)],
        compiler_params=_cparams(("arbitrary", "arbitrary")),
        name="gdn",
    )(h2d, h2d, ab, conv_w, a_log, dt_bias, norm_w, conv_init, s0)


def _attn_kernel(q_ref, k_ref, v_ref, bias_ref, o_ref, *, tq, width, pad):
    n = pl.program_id(1)
    start = pl.multiple_of(n * tq, tq)
    kwin = k_ref[0, pl.ds(start, width), :]
    vwin = v_ref[0, pl.ds(start, width), :]
    kpos = start - pad + lax.broadcasted_iota(jnp.int32, (tq, width), 1)
    valid = kpos >= 0
    scale = ATT_DH ** -0.5
    heads = range(ATT_HEADS)
    cols = [slice(h * ATT_DH, (h + 1) * ATT_DH) for h in heads]
    s = [_dot_nt(q_ref[:, cols[h]], kwin[:, cols[h]]) for h in heads]
    s = [jnp.where(valid, s[h] * scale + bias_ref[h], -jnp.inf) for h in heads]
    p = [jnp.exp(x - jnp.max(x, axis=-1, keepdims=True)) for x in s]
    l = [jnp.sum(x, axis=-1, keepdims=True) for x in p]
    pv = [_dot(p[h], vwin[:, cols[h]]) for h in heads]
    o_ref[...] = jnp.concatenate([pv[h] / l[h] for h in heads], axis=1).astype(BF16)


def _attn(q2d, q_col_block, k3d, v3d, bias, *, batch, t, tq, width, pad):
    nq = t // tq
    tk = k3d.shape[1]
    return pl.pallas_call(
        functools.partial(_attn_kernel, tq=tq, width=width, pad=pad),
        grid=(batch, nq),
        in_specs=[
            pl.BlockSpec((tq, ATT_W), lambda b, n: (b * nq + n, q_col_block)),
            pl.BlockSpec((1, tk, ATT_W), lambda b, n: (b, 0, 0)),
            pl.BlockSpec((1, tk, ATT_W), lambda b, n: (b, 0, 0)),
            pl.BlockSpec((ATT_HEADS, tq, width), lambda b, n: (0, 0, 0)),
        ],
        out_specs=pl.BlockSpec((tq, ATT_W), lambda b, n: (b * nq + n, 0)),
        out_shape=jax.ShapeDtypeStruct((batch * t, ATT_W), BF16),
        compiler_params=_cparams(("arbitrary", "arbitrary")),
        name="attn",
    )(q2d, k3d, v3d, bias)


def _merge_kernel(x_ref, g_ref, b_ref, oa_ref, ob_ref, ga_ref, gb_ref, wa_ref, wb_ref, wo_ref,
                  g1_ref, b1_ref, x1_ref):
    xn = _layernorm_rows(x_ref[...], g_ref[...], b_ref[...])
    ya = jnp.dot(oa_ref[...], wa_ref[...], preferred_element_type=F32)
    yb = jnp.dot(ob_ref[...], wb_ref[...], preferred_element_type=F32)
    mixed = (_sigmoid(ga_ref[...].astype(F32)) * ya + _sigmoid(gb_ref[...].astype(F32)) * yb)
    y = DEEPNORM_ALPHA * xn + _dot(mixed, wo_ref[...])
    x1_ref[...] = _layernorm_rows(y, g1_ref[...], b1_ref[...])


def _merge(x2d, ln_g, ln_b, oa, ob, h2d, wa, wb, wo, ln1_g, ln1_b, *, tm):
    rows = x2d.shape[0]
    full = lambda i: (0, 0)
    return pl.pallas_call(
        _merge_kernel,
        grid=(rows // tm,),
        in_specs=[
            pl.BlockSpec((tm, D_MODEL), lambda i: (i, 0)),
            pl.BlockSpec((1, D_MODEL), full),
            pl.BlockSpec((1, D_MODEL), full),
            pl.BlockSpec((tm, GDN_W), lambda i: (i, 0)),
            pl.BlockSpec((tm, ATT_W), lambda i: (i, 0)),
            pl.BlockSpec((tm, D_MODEL), lambda i: (i, COL_GA // D_MODEL)),
            pl.BlockSpec((tm, D_MODEL), lambda i: (i, COL_GB // D_MODEL)),
            pl.BlockSpec((GDN_W, D_MODEL), full),
            pl.BlockSpec((ATT_W, D_MODEL), full),
            pl.BlockSpec((D_MODEL, D_MODEL), full),
            pl.BlockSpec((1, D_MODEL), full),
            pl.BlockSpec((1, D_MODEL), full),
        ],
        out_specs=pl.BlockSpec((tm, D_MODEL), lambda i: (i, 0)),
        out_shape=jax.ShapeDtypeStruct((rows, D_MODEL), F32),
        compiler_params=_cparams(("arbitrary",)),
        name="merge",
    )(x2d, ln_g, ln_b, oa, ob, h2d, h2d, wa, wb, wo, ln1_g, ln1_b)


def _gelu_tanh(x):
    return 0.5 * x * (1.0 + jnp.tanh(math.sqrt(2.0 / math.pi) * (x + 0.044715 * (x * x * x))))


def _ffn_kernel(x1_ref, pe_ref, wup_ref, cw_ref, wdn_ref, wple_ref, wpg_ref, g2_ref, b2_ref,
                finit_ref, y_ref, fstate_ref, gp_ref, acc_ref, *, tm, fc):
    n = pl.program_id(1)
    pad = SUBLANES

    @pl.when(n == 0)
    def _():
        gp_ref[0:pad, :] = finit_ref[0]

    x1 = x1_ref[...]
    x1b = x1.astype(BF16)
    acc_ref[...] = jnp.zeros_like(acc_ref)
    for k in range(D_FF // fc):
        c0 = k * fc
        gate = jnp.dot(x1b, wup_ref[:, c0:c0 + fc], preferred_element_type=F32)
        val = jnp.dot(x1b, wup_ref[:, D_FF + c0:D_FF + c0 + fc], preferred_element_type=F32)
        gp_ref[pad:pad + tm, c0:c0 + fc] = gate
        conv = gate * cw_ref[FFN_CONV - 1:FFN_CONV, c0:c0 + fc]
        for i in range(1, FFN_CONV):
            conv = conv + (gp_ref[pl.ds(pad - i, tm), c0:c0 + fc]
                           * cw_ref[FFN_CONV - 1 - i:FFN_CONV - i, c0:c0 + fc])
        act = (_gelu_tanh(conv) * val).astype(BF16)
        acc_ref[...] += jnp.dot(act, wdn_ref[c0:c0 + fc, :], preferred_element_type=F32)
    tail = gp_ref[tm:tm + pad, :]
    gp_ref[0:pad, :] = tail
    fstate_ref[0] = tail

    r = DEEPNORM_ALPHA * x1 + acc_ref[...]
    gate_p = _sigmoid(_dot(r, wpg_ref[...]))
    r = r + gate_p * jnp.dot(pe_ref[...], wple_ref[...], preferred_element_type=F32)
    y_ref[...] = _layernorm_rows(r, g2_ref[...], b2_ref[...])


def _ffn(x1, pe, wup, conv_w, wdn, wple, wpg, ln2_g, ln2_b, finit, *, batch, t, tm, fc):
    nt = t // tm
    rows = batch * t
    full = lambda b, n: (0, 0)
    return pl.pallas_call(
        functools.partial(_ffn_kernel, tm=tm, fc=fc),
        grid=(batch, nt),
        in_specs=[
            pl.BlockSpec((tm, D_MODEL), lambda b, n: (b * nt + n, 0)),
            pl.BlockSpec((tm, PLE_DIM), lambda b, n: (b * nt + n, 0)),
            pl.BlockSpec((D_MODEL, 2 * D_FF), full, pipeline_mode=pl.Buffered(1)),
            pl.BlockSpec((FFN_CONV, D_FF), full),
            pl.BlockSpec((D_FF, D_MODEL), full, pipeline_mode=pl.Buffered(1)),
            pl.BlockSpec((PLE_DIM, D_MODEL), full),
            pl.BlockSpec((D_MODEL, D_MODEL), full),
            pl.BlockSpec((1, D_MODEL), full),
            pl.BlockSpec((1, D_MODEL), full),
            pl.BlockSpec((1, SUBLANES, D_FF), lambda b, n: (b, 0, 0)),
        ],
        out_specs=[
            pl.BlockSpec((tm, D_MODEL), lambda b, n: (b * nt + n, 0)),
            pl.BlockSpec((1, SUBLANES, D_FF), lambda b, n: (b, 0, 0)),
        ],
        out_shape=[
            jax.ShapeDtypeStruct((rows, D_MODEL), F32),
            jax.ShapeDtypeStruct((batch, SUBLANES, D_FF), F32),
        ],
        scratch_shapes=[pltpu.VMEM((SUBLANES + tm, D_FF), F32), pltpu.VMEM((tm, D_MODEL), F32)],
        compiler_params=_cparams(("arbitrary", "arbitrary")),
        name="ffn",
    )(x1, pe, wup, conv_w, wdn, wple, wpg, ln2_g, ln2_b, finit)


def _rel_bias(table, n_q, n_past, n_k):
    span = n_q + n_k - 1
    rel = n_q - 1 + n_past - jnp.arange(span)
    rev = table[:, jnp.clip(rel, -REL_CLIP, REL_CLIP) + REL_CLIP].astype(F32)
    return jnp.stack([rev[:, n_q - 1 - q:n_q - 1 - q + n_k] for q in range(n_q)], axis=1)


def _pad_state(state, rows):
    b, r, ch = state.shape
    return jnp.concatenate([jnp.zeros((b, rows - r, ch), F32), state.astype(F32)], axis=1)


def _layer(x, pe, gconv_prev, s_gdn, k_prev, v_prev, fconv_prev, wts, *, tm_proj, tm_merge, tm_ffn):
    (ln_in_g, ln_in_b, w_main, w_ab, gdn_conv_w, a_log, dt_bias, norm_w, rel_table, w_a, w_b, w_o,
     ln1_g, ln1_b, w_up, ffn_conv_w, w_dn, w_ple, w_pg, ln2_g, ln2_b) = wts
    b, t, _ = x.shape
    rows = b * t
    x2d = x.reshape(rows, D_MODEL)

    h2d, ab = _in_proj(x2d, ln_in_g, ln_in_b, w_main, w_ab, tm=min(tm_proj, rows), tn=1536)

    o_a, gconv_new, s_new = _gdn(h2d, ab, gdn_conv_w, a_log, dt_bias, norm_w,
                                 _pad_state(gconv_prev, SUBLANES), s_gdn.astype(F32), batch=b, t=t)

    k_b = h2d[:, COL_AK:COL_AK + ATT_W].reshape(b, t, ATT_W)
    v_b = h2d[:, COL_AV:COL_AV + ATT_W].reshape(b, t, ATT_W)
    if k_prev is None:
        kp = jnp.pad(k_b, ((0, 0), (BAND, 0), (0, 0)))
        vp = jnp.pad(v_b, ((0, 0), (BAND, 0), (0, 0)))
        bias = _rel_bias(rel_table, CHUNK, BAND, BAND + CHUNK)
        o_b = _attn(h2d, COL_AQ // ATT_W, kp, vp, bias, batch=b, t=t, tq=CHUNK,
                    width=BAND + CHUNK, pad=BAND)
        keep = min(BAND, t)
        k_new, v_new = k_b[:, t - keep:], v_b[:, t - keep:]
    else:
        n_past = k_prev.shape[1]
        kk = jnp.concatenate([k_prev.reshape(b, n_past, ATT_W).astype(BF16), k_b], axis=1)
        vv = jnp.concatenate([v_prev.reshape(b, n_past, ATT_W).astype(BF16), v_b], axis=1)
        bias = _rel_bias(rel_table, t, n_past, n_past + t)
        o_b = _attn(h2d, COL_AQ // ATT_W, kk, vv, bias, batch=b, t=t, tq=t, width=n_past + t, pad=0)
        k_new, v_new = k_b, v_b

    x1 = _merge(x2d, ln_in_g, ln_in_b, o_a, o_b, h2d, w_a, w_b, w_o, ln1_g, ln1_b,
                tm=min(tm_merge, rows))

    y, fconv_new = _ffn(x1, pe.reshape(rows, PLE_DIM).astype(BF16), w_up, ffn_conv_w, w_dn, w_ple, w_pg,
                        ln2_g, ln2_b, _pad_state(fconv_prev, SUBLANES), batch=b, t=t,
                        tm=min(tm_ffn, t), fc=256)

    keep_k = k_new.shape[1]
    return (y.reshape(b, t, D_MODEL),
            gconv_new[:, SUBLANES - (GDN_CONV - 1):],
            s_new,
            k_new.astype(F32).reshape(b, keep_k, ATT_HEADS, ATT_DH),
            v_new.astype(F32).reshape(b, keep_k, ATT_HEADS, ATT_DH),
            fconv_new[:, SUBLANES - (FFN_CONV - 1):])


def _pad_lanes(v, width=LANES):
    return jnp.pad(v.astype(F32), (0, width - v.shape[0])).reshape(1, width)


def kernel(x_prompt, x_sample, state_gdn_conv, state_gdn, cache_attn_k, cache_attn_v, state_ffn_conv,
           p_prompt, p_sample, ln_in_g, ln_in_b, w_in, gdn_conv_w, gdn_a_log, gdn_dt_bias, gdn_norm_w,
           att_rel_bias, w_branch_a, w_branch_b, w_out, ln1_g, ln1_b, w_ffn_up, ffn_conv_w, w_ffn_down,
           w_ple, w_ple_gate, ln2_g, ln2_b):
    depth = w_in.shape[0]
    assert depth == 1
    bp = x_prompt.shape[0]
    row = lambda v: v.reshape(1, -1).astype(F32)

    xp, xs = x_prompt, x_sample
    new_p, new_s = [], []
    for i in range(depth):
        wi = w_in[i]
        o_ab = 4 * GDN_W
        o_att = o_ab + 2 * GDN_HEADS
        o_gate = o_att + 3 * ATT_W
        w_main = jnp.concatenate([wi[:, :o_ab], wi[:, o_gate:], wi[:, o_att:o_gate]], axis=1).astype(BF16)
        w_ab = jnp.pad(wi[:, o_ab:o_att], ((0, 0), (0, LANES - 2 * GDN_HEADS))).astype(BF16)
        wts = (row(ln_in_g), row(ln_in_b), w_main, w_ab, gdn_conv_w[i].astype(F32),
               _pad_lanes(gdn_a_log[i]), _pad_lanes(gdn_dt_bias[i]), row(gdn_norm_w[i]), att_rel_bias[i],
               w_branch_a[i].astype(BF16), w_branch_b[i].astype(BF16), w_out[i].astype(BF16),
               row(ln1_g[i]), row(ln1_b[i]), w_ffn_up[i].astype(BF16), ffn_conv_w[i].astype(F32),
               w_ffn_down[i].astype(BF16), w_ple[i].astype(BF16), w_ple_gate[i].astype(BF16),
               row(ln2_g[i]), row(ln2_b[i]))
        xp, st_p = _unpack(_layer(xp, p_prompt[i],
                                  jnp.zeros((bp, GDN_CONV - 1, GDN_CONV_CH), F32),
                                  jnp.zeros((bp, GDN_HEADS, GDN_DK, GDN_DV), F32),
                                  None, None,
                                  jnp.zeros((bp, FFN_CONV - 1, D_FF), F32), wts,
                                  tm_proj=1024, tm_merge=512, tm_ffn=256))
        xs, st_s = _unpack(_layer(xs, p_sample[i], state_gdn_conv[i], state_gdn[i], cache_attn_k[i],
                                  cache_attn_v[i], state_ffn_conv[i], wts,
                                  tm_proj=256, tm_merge=256, tm_ffn=256))
        new_p.append(st_p)
        new_s.append(st_s)
    p_gconv, p_gdn, p_k, p_v, p_fconv = [jnp.stack(a) for a in zip(*new_p)]
    s_gconv, s_gdn, s_k, s_v, s_fconv = [jnp.stack(a) for a in zip(*new_s)]
    return (xp, xs, p_gconv, p_gdn, p_k, p_v, p_fconv, s_gconv, s_gdn, s_k, s_v, s_fconv)


def _unpack(res):
    return res[0], res[1:]
```

```python
import functools
import math

import jax
import jax.numpy as jnp
from jax import lax
from jax.experimental import pallas as pl
from jax.experimental.pallas import tpu as pltpu

F32 = jnp.float32
BF16 = jnp.bfloat16

D_MODEL = 1024
CHUNK = 64
BAND = 512
GDN_HEADS = 8
GDN_DK = 128
GDN_DV = 128
GDN_CONV = 4
ATT_HEADS = 8
ATT_DH = 64
REL_CLIP = 128
D_FF = 2816
FFN_CONV = 3
PLE_DIM = 256
DEEPNORM_ALPHA = 2.0 ** 0.25
LN_EPS = 1e-5
RMS_EPS = 1e-6
L2_EPS = 1e-6

GDN_W = GDN_HEADS * GDN_DK
GDN_CONV_CH = 3 * GDN_W
ATT_W = ATT_HEADS * ATT_DH
LANES = 128
SUBLANES = 8

COL_QKV = 0
COL_Z = 3 * GDN_W
COL_GA = 4 * GDN_W
COL_GB = 5 * GDN_W
H_MAIN = 6 * GDN_W

VMEM_LIMIT = 56 * 1024 * 1024


def _cparams(sem):
    return pltpu.CompilerParams(dimension_semantics=sem, vmem_limit_bytes=VMEM_LIMIT)


def _sigmoid(x):
    return 1.0 / (1.0 + jnp.exp(-x))


def _silu(x):
    return x * _sigmoid(x)


def _layernorm_rows(x, g, b):
    mu = jnp.mean(x, axis=-1, keepdims=True)
    xc = x - mu
    var = jnp.mean(xc * xc, axis=-1, keepdims=True)
    return xc * lax.rsqrt(var + LN_EPS) * g + b


def _dot(a, b):
    return jnp.dot(a.astype(BF16), b.astype(BF16), preferred_element_type=F32)


def _dot_nt(a, b):
    return lax.dot_general(a.astype(BF16), b.astype(BF16), (((1,), (1,)), ((), ())),
                           preferred_element_type=F32)


def _dot_tn(a, b):
    return lax.dot_general(a.astype(BF16), b.astype(BF16), (((0,), (0,)), ((), ())),
                           preferred_element_type=F32)


def _split3(x):
    x1 = x.astype(BF16)
    r1 = x - x1.astype(F32)
    x2 = r1.astype(BF16)
    x3 = (r1 - x2.astype(F32)).astype(BF16)
    return x1, x2, x3


def _in_proj_kernel(x_ref, g_ref, b_ref, w_ref, wab_ref, kz_ref, vz_ref,
                    h_ref, ab_ref, q_ref, k_ref, v_ref, xn_ref, *, sub, n_main):
    del kz_ref, vz_ref
    j = pl.program_id(1)
    tm = x_ref.shape[0]

    @pl.when(j == 0)
    def _():
        def body(i, c):
            r = pl.multiple_of(i * sub, sub)
            xn = _layernorm_rows(x_ref[pl.ds(r, sub), :], g_ref[...], b_ref[...])
            xn_ref[pl.ds(r, sub), :] = xn.astype(BF16)
            return c
        lax.fori_loop(0, tm // sub, body, 0)
        ab_ref[...] = jnp.dot(xn_ref[...], wab_ref[...], preferred_element_type=F32)

    @pl.when(j < n_main)
    def _():
        h_ref[...] = jnp.dot(xn_ref[...], w_ref[...], preferred_element_type=F32).astype(BF16)

    @pl.when(j == n_main)
    def _():
        for o_ref, c0 in ((q_ref, 0), (k_ref, ATT_W), (v_ref, 2 * ATT_W)):
            o_ref[...] = jnp.dot(xn_ref[...], w_ref[:, c0:c0 + ATT_W],
                                 preferred_element_type=F32).astype(BF16)


def _in_proj(x2d, ln_g, ln_b, w_main, w_ab, *, tm, t, pad_rows):
    rows = x2d.shape[0]
    tn = 3 * ATT_W
    n_main = H_MAIN // tn
    assert rows % tm == 0 and t % tm == 0 and pad_rows % tm == 0 and H_MAIN % tn == 0
    sub = min(tm, 128)
    per_stream = t // tm
    stride = per_stream + pad_rows // tm
    kv_rows = (rows // t) * (t + pad_rows)
    kv_block = lambda i, j: ((i // per_stream) * stride + pad_rows // tm + i % per_stream, 0)
    zeros = jnp.zeros((kv_rows, ATT_W), BF16)
    return pl.pallas_call(
        functools.partial(_in_proj_kernel, sub=sub, n_main=n_main),
        grid=(rows // tm, n_main + 1),
        in_specs=[
            pl.BlockSpec((tm, D_MODEL), lambda i, j: (i, 0)),
            pl.BlockSpec((1, D_MODEL), lambda i, j: (0, 0)),
            pl.BlockSpec((1, D_MODEL), lambda i, j: (0, 0)),
            pl.BlockSpec((D_MODEL, tn), lambda i, j: (0, j)),
            pl.BlockSpec((D_MODEL, LANES), lambda i, j: (0, 0)),
            pl.BlockSpec(memory_space=pl.ANY),
            pl.BlockSpec(memory_space=pl.ANY),
        ],
        out_specs=[
            pl.BlockSpec((tm, tn), lambda i, j: (i, jnp.minimum(j, n_main - 1))),
            pl.BlockSpec((tm, LANES), lambda i, j: (i, 0)),
            pl.BlockSpec((tm, ATT_W), lambda i, j: (i, 0)),
            pl.BlockSpec((tm, ATT_W), kv_block),
            pl.BlockSpec((tm, ATT_W), kv_block),
        ],
        out_shape=[
            jax.ShapeDtypeStruct((rows, H_MAIN), BF16),
            jax.ShapeDtypeStruct((rows, LANES), F32),
            jax.ShapeDtypeStruct((rows, ATT_W), BF16),
            jax.ShapeDtypeStruct((kv_rows, ATT_W), BF16),
            jax.ShapeDtypeStruct((kv_rows, ATT_W), BF16),
        ],
        input_output_aliases={5: 3, 6: 4},
        scratch_shapes=[pltpu.VMEM((tm, D_MODEL), BF16)],
        compiler_params=_cparams(("arbitrary", "arbitrary")),
        name="in_proj",
    )(x2d, ln_g, ln_b, w_main, w_ab, zeros, zeros)


def _gdn_kernel(qkv_ref, z_ref, ab_ref, cw_ref, alog_ref, dtb_ref, nw_ref, cinit_ref, s0_ref,
                o_ref, cstate_ref, s_ref, xp_ref, *, c):
    n = pl.program_id(1)
    pad = SUBLANES

    @pl.when(n == 0)
    def _():
        xp_ref[0:pad, :] = cinit_ref[0]
        s_ref[...] = s0_ref[...]

    xp_ref[pad:pad + c, :] = qkv_ref[...].astype(F32)

    row = lax.broadcasted_iota(jnp.int32, (c, c), 0)
    col = lax.broadcasted_iota(jnp.int32, (c, c), 1)
    causal = row >= col
    strict = row > col

    ab = ab_ref[...]
    sp_in = ab + dtb_ref[...]
    softplus = jnp.maximum(sp_in, 0.0) + jnp.log(1.0 + jnp.exp(-jnp.abs(sp_in)))
    g_all = -jnp.exp(alog_ref[...]) * softplus
    beta_all = _sigmoid(ab)
    g1, g2, g3 = _split3(g_all)
    tri = jnp.where(causal, 1.0, 0.0).astype(BF16)
    dec_all = (jnp.dot(tri, g1, preferred_element_type=F32)
               + (jnp.dot(tri, g2, preferred_element_type=F32)
                  + jnp.dot(tri, g3, preferred_element_type=F32)))
    dec_t = dec_all.T

    def conv_silu(c0):
        acc = xp_ref[pl.ds(pad, c), pl.ds(c0, GDN_DK)] * cw_ref[GDN_CONV - 1:GDN_CONV, pl.ds(c0, GDN_DK)]
        for i in range(1, GDN_CONV):
            acc = acc + (xp_ref[pl.ds(pad - i, c), pl.ds(c0, GDN_DK)]
                         * cw_ref[GDN_CONV - 1 - i:GDN_CONV - i, pl.ds(c0, GDN_DK)])
        return _silu(acc)

    heads = range(GDN_HEADS)
    eye = jnp.where(row == col, 1.0, 0.0).astype(F32)
    xq = [conv_silu(h * GDN_DK) for h in heads]
    xk = [conv_silu(GDN_W + h * GDN_DK) for h in heads]
    xv = [conv_silu(2 * GDN_W + h * GDN_DV) for h in heads]
    qn = [x * lax.rsqrt(jnp.sum(x * x, axis=-1, keepdims=True) + L2_EPS) * (GDN_DK ** -0.5) for x in xq]
    kn = [x * lax.rsqrt(jnp.sum(x * x, axis=-1, keepdims=True) + L2_EPS) for x in xk]
    dcol = [dec_all[:, h:h + 1] for h in heads]
    beta = [beta_all[:, GDN_HEADS + h:GDN_HEADS + h + 1] for h in heads]
    lmask = [jnp.exp(jnp.where(causal, dcol[h] - dec_t[h:h + 1, :], -jnp.inf)) for h in heads]
    kb = [kn[h] * beta[h] for h in heads]
    e_d = [jnp.exp(d) for d in dcol]
    gram = [_dot_nt(jnp.concatenate([kb[h], qn[h]], axis=0), kn[h]) for h in heads]
    m = [jnp.where(strict, gram[h][:c] * lmask[h], 0.0) for h in heads]
    qk = [gram[h][c:] * lmask[h] for h in heads]
    p = [eye - m[h] for h in heads]
    mk = m
    for _ in range(int(math.log2(c)) - 1):
        mk = [_dot(x, x) for x in mk]
        p = [p[h] + _dot(p[h], mk[h]) for h in heads]
    sol = [_dot(p[h], jnp.concatenate([xv[h] * beta[h], kb[h] * e_d[h]], axis=1)) for h in heads]
    s_old = [s_ref[0, h] for h in heads]
    ws_qs = [_dot(jnp.concatenate([sol[h][:, GDN_DV:], qn[h] * e_d[h]], axis=0), s_old[h])
             for h in heads]
    v_new = [sol[h][:, :GDN_DV] - ws_qs[h][:c] for h in heads]
    dl = [d[c - 1:c, :] for d in dcol]
    kd = [kn[h] * jnp.exp(dl[h] - dcol[h]) for h in heads]
    s_upd = [_dot_tn(kd[h], v_new[h]) for h in heads]
    o_in = [_dot(qk[h], v_new[h]) for h in heads]
    for h in heads:
        s_ref[0, h] = s_old[h] * jnp.exp(dl[h]) + s_upd[h]
        o = ws_qs[h][c:] + o_in[h]
        zh = z_ref[:, pl.ds(h * GDN_DV, GDN_DV)].astype(F32)
        o = (o * lax.rsqrt(jnp.mean(o * o, axis=-1, keepdims=True) + RMS_EPS) * nw_ref[...]
             * _silu(zh))
        o_ref[:, pl.ds(h * GDN_DV, GDN_DV)] = o.astype(BF16)

    tail = xp_ref[c:c + pad, :]
    xp_ref[0:pad, :] = tail
    cstate_ref[0] = tail


def _gdn(h2d, ab, conv_w, a_log, dt_bias, norm_w, conv_init, s0, *, batch, t):
    c = min(CHUNK, t)
    nc = t // c
    rows = batch * t
    return pl.pallas_call(
        functools.partial(_gdn_kernel, c=c),
        grid=(batch, nc),
        in_specs=[
            pl.BlockSpec((c, GDN_CONV_CH), lambda b, n: (b * nc + n, COL_QKV // GDN_CONV_CH)),
            pl.BlockSpec((c, GDN_W), lambda b, n: (b * nc + n, COL_Z // GDN_W)),
            pl.BlockSpec((c, LANES), lambda b, n: (b * nc + n, 0)),
            pl.BlockSpec((GDN_CONV, GDN_CONV_CH), lambda b, n: (0, 0)),
            pl.BlockSpec((1, LANES), lambda b, n: (0, 0)),
            pl.BlockSpec((1, LANES), lambda b, n: (0, 0)),
            pl.BlockSpec((1, GDN_DV), lambda b, n: (0, 0)),
            pl.BlockSpec((1, SUBLANES, GDN_CONV_CH), lambda b, n: (b, 0, 0)),
            pl.BlockSpec((1, GDN_HEADS, GDN_DK, GDN_DV), lambda b, n: (b, 0, 0, 0)),
        ],
        out_specs=[
            pl.BlockSpec((c, GDN_W), lambda b, n: (b * nc + n, 0)),
            pl.BlockSpec((1, SUBLANES, GDN_CONV_CH), lambda b, n: (b, 0, 0)),
            pl.BlockSpec((1, GDN_HEADS, GDN_DK, GDN_DV), lambda b, n: (b, 0, 0, 0)),
        ],
        out_shape=[
            jax.ShapeDtypeStruct((rows, GDN_W), BF16),
            jax.ShapeDtypeStruct((batch, SUBLANES, GDN_CONV_CH), F32),
            jax.ShapeDtypeStruct((batch, GDN_HEADS, GDN_DK, GDN_DV), F32),
        ],
        scratch_shapes=[pltpu.VMEM((SUBLANES + c, GDN_CONV_CH), F32)],
        compiler_params=_cparams(("arbitrary", "arbitrary")),
        name="gdn",
    )(h2d, h2d, ab, conv_w, a_log, dt_bias, norm_w, conv_init, s0)


def _attn_kernel(q_ref, k_ref, v_ref, bias_ref, o_ref, *, tq, width, pad, koff):
    n = pl.program_id(1)
    start = pl.multiple_of(koff + n * tq, tq)
    kwin = k_ref[0, pl.ds(start, width), :]
    vwin = v_ref[0, pl.ds(start, width), :]
    kpos = n * tq - pad + lax.broadcasted_iota(jnp.int32, (tq, width), 1)
    valid = kpos >= 0
    scale = ATT_DH ** -0.5
    heads = range(ATT_HEADS)
    cols = [slice(h * ATT_DH, (h + 1) * ATT_DH) for h in heads]
    s = [_dot_nt(q_ref[:, cols[h]], kwin[:, cols[h]]) for h in heads]
    s = [jnp.where(valid, s[h] * scale + bias_ref[h], -jnp.inf) for h in heads]
    p = [jnp.exp(x - jnp.max(x, axis=-1, keepdims=True)) for x in s]
    l = [jnp.sum(x, axis=-1, keepdims=True) for x in p]
    pv = [_dot(p[h], vwin[:, cols[h]]) for h in heads]
    o_ref[...] = jnp.concatenate([pv[h] / l[h] for h in heads], axis=1).astype(BF16)


def _attn(q2d, k3d, v3d, bias, *, batch, t, tq, width, pad, koff):
    nq = t // tq
    tk = k3d.shape[1]
    assert koff % tq == 0 and koff + (nq - 1) * tq + width <= tk
    return pl.pallas_call(
        functools.partial(_attn_kernel, tq=tq, width=width, pad=pad, koff=koff),
        grid=(batch, nq),
        in_specs=[
            pl.BlockSpec((tq, ATT_W), lambda b, n: (b * nq + n, 0)),
            pl.BlockSpec((1, tk, ATT_W), lambda b, n: (b, 0, 0)),
            pl.BlockSpec((1, tk, ATT_W), lambda b, n: (b, 0, 0)),
            pl.BlockSpec((ATT_HEADS, tq, width), lambda b, n: (0, 0, 0)),
        ],
        out_specs=pl.BlockSpec((tq, ATT_W), lambda b, n: (b * nq + n, 0)),
        out_shape=jax.ShapeDtypeStruct((batch * t, ATT_W), BF16),
        compiler_params=_cparams(("arbitrary", "arbitrary")),
        name="attn",
    )(q2d, k3d, v3d, bias)


def _merge_kernel(x_ref, g_ref, b_ref, oa_ref, ob_ref, ga_ref, gb_ref, wa_ref, wb_ref, wo_ref,
                  g1_ref, b1_ref, x1_ref):
    xn = _layernorm_rows(x_ref[...], g_ref[...], b_ref[...])
    ya = jnp.dot(oa_ref[...], wa_ref[...], preferred_element_type=F32)
    yb = jnp.dot(ob_ref[...], wb_ref[...], preferred_element_type=F32)
    mixed = (_sigmoid(ga_ref[...].astype(F32)) * ya + _sigmoid(gb_ref[...].astype(F32)) * yb)
    y = DEEPNORM_ALPHA * xn + _dot(mixed, wo_ref[...])
    x1_ref[...] = _layernorm_rows(y, g1_ref[...], b1_ref[...])


def _merge(x2d, ln_g, ln_b, oa, ob, h2d, wa, wb, wo, ln1_g, ln1_b, *, tm):
    rows = x2d.shape[0]
    full = lambda i: (0, 0)
    return pl.pallas_call(
        _merge_kernel,
        grid=(rows // tm,),
        in_specs=[
            pl.BlockSpec((tm, D_MODEL), lambda i: (i, 0)),
            pl.BlockSpec((1, D_MODEL), full),
            pl.BlockSpec((1, D_MODEL), full),
            pl.BlockSpec((tm, GDN_W), lambda i: (i, 0)),
            pl.BlockSpec((tm, ATT_W), lambda i: (i, 0)),
            pl.BlockSpec((tm, D_MODEL), lambda i: (i, COL_GA // D_MODEL)),
            pl.BlockSpec((tm, D_MODEL), lambda i: (i, COL_GB // D_MODEL)),
            pl.BlockSpec((GDN_W, D_MODEL), full),
            pl.BlockSpec((ATT_W, D_MODEL), full),
            pl.BlockSpec((D_MODEL, D_MODEL), full),
            pl.BlockSpec((1, D_MODEL), full),
            pl.BlockSpec((1, D_MODEL), full),
        ],
        out_specs=pl.BlockSpec((tm, D_MODEL), lambda i: (i, 0)),
        out_shape=jax.ShapeDtypeStruct((rows, D_MODEL), F32),
        compiler_params=_cparams(("arbitrary",)),
        name="merge",
    )(x2d, ln_g, ln_b, oa, ob, h2d, h2d, wa, wb, wo, ln1_g, ln1_b)


def _gelu_tanh(x):
    return 0.5 * x * (1.0 + jnp.tanh(math.sqrt(2.0 / math.pi) * (x + 0.044715 * (x * x * x))))


def _ffn_kernel(x1_ref, pe_ref, wup_ref, cw_ref, wdn_ref, wple_ref, wpg_ref, g2_ref, b2_ref,
                finit_ref, y_ref, fstate_ref, gp_ref, acc_ref, *, tm, fc):
    n = pl.program_id(1)
    pad = SUBLANES

    @pl.when(n == 0)
    def _():
        gp_ref[0:pad, :] = finit_ref[0]

    x1 = x1_ref[...]
    x1b = x1.astype(BF16)

    def up(k):
        c0 = k * fc
        gate = jnp.dot(x1b, wup_ref[:, c0:c0 + fc], preferred_element_type=F32)
        val = jnp.dot(x1b, wup_ref[:, D_FF + c0:D_FF + c0 + fc], preferred_element_type=F32)
        gp_ref[pad:pad + tm, c0:c0 + fc] = gate
        return gate, val

    nk = D_FF // fc
    nxt = up(0)
    ple = jnp.dot(pe_ref[...].astype(BF16), wple_ref[...], preferred_element_type=F32)
    for k in range(nk):
        c0 = k * fc
        gate, val = nxt
        if k + 1 < nk:
            nxt = up(k + 1)
        conv = gate * cw_ref[FFN_CONV - 1:FFN_CONV, c0:c0 + fc]
        for i in range(1, FFN_CONV):
            conv = conv + (gp_ref[pl.ds(pad - i, tm), c0:c0 + fc]
                           * cw_ref[FFN_CONV - 1 - i:FFN_CONV - i, c0:c0 + fc])
        act = (_gelu_tanh(conv) * val).astype(BF16)
        down = jnp.dot(act, wdn_ref[c0:c0 + fc, :], preferred_element_type=F32)
        if k == 0:
            acc_ref[...] = down
        else:
            acc_ref[...] += down
    tail = gp_ref[tm:tm + pad, :]
    gp_ref[0:pad, :] = tail
    fstate_ref[0] = tail

    r = DEEPNORM_ALPHA * x1 + acc_ref[...]
    gate_p = _sigmoid(_dot(r, wpg_ref[...]))
    r = r + gate_p * ple
    y_ref[...] = _layernorm_rows(r, g2_ref[...], b2_ref[...])


def _ffn(x1, pe, wup, conv_w, wdn, wple, wpg, ln2_g, ln2_b, finit, *, batch, t, tm, fc):
    nt = t // tm
    rows = batch * t
    full = lambda b, n: (0, 0)
    return pl.pallas_call(
        functools.partial(_ffn_kernel, tm=tm, fc=fc),
        grid=(batch, nt),
        in_specs=[
            pl.BlockSpec((tm, D_MODEL), lambda b, n: (b * nt + n, 0)),
            pl.BlockSpec((tm, PLE_DIM), lambda b, n: (b * nt + n, 0)),
            pl.BlockSpec((D_MODEL, 2 * D_FF), full, pipeline_mode=pl.Buffered(1)),
            pl.BlockSpec((FFN_CONV, D_FF), full),
            pl.BlockSpec((D_FF, D_MODEL), full, pipeline_mode=pl.Buffered(1)),
            pl.BlockSpec((PLE_DIM, D_MODEL), full),
            pl.BlockSpec((D_MODEL, D_MODEL), full),
            pl.BlockSpec((1, D_MODEL), full),
            pl.BlockSpec((1, D_MODEL), full),
            pl.BlockSpec((1, SUBLANES, D_FF), lambda b, n: (b, 0, 0)),
        ],
        out_specs=[
            pl.BlockSpec((tm, D_MODEL), lambda b, n: (b * nt + n, 0)),
            pl.BlockSpec((1, SUBLANES, D_FF), lambda b, n: (b, 0, 0)),
        ],
        out_shape=[
            jax.ShapeDtypeStruct((rows, D_MODEL), F32),
            jax.ShapeDtypeStruct((batch, SUBLANES, D_FF), F32),
        ],
        scratch_shapes=[pltpu.VMEM((SUBLANES + tm, D_FF), F32), pltpu.VMEM((tm, D_MODEL), F32)],
        compiler_params=_cparams(("arbitrary", "arbitrary")),
        name="ffn",
    )(x1, pe, wup, conv_w, wdn, wple, wpg, ln2_g, ln2_b, finit)


def _rel_bias(table, n_q, n_past, n_k):
    span = n_q + n_k - 1
    rel = n_q - 1 + n_past - jnp.arange(span)
    rev = table[:, jnp.clip(rel, -REL_CLIP, REL_CLIP) + REL_CLIP].astype(F32)
    heads = table.shape[0]
    period = jnp.pad(rev, ((0, 0), (0, 1)))
    skew = jnp.tile(period, (1, n_q))[:, :n_q * span].reshape(heads, n_q, span)
    return skew[:, :, n_q - 1:n_q - 1 + n_k]


def _pad_state(state, rows):
    b, r, ch = state.shape
    return jnp.concatenate([jnp.zeros((b, rows - r, ch), F32), state.astype(F32)], axis=1)


def _layer(x, pe, gconv_prev, s_gdn, k_prev, v_prev, fconv_prev, wts, *, tm_proj, tm_merge, tm_ffn):
    (ln_in_g, ln_in_b, w_main, w_ab, gdn_conv_w, a_log, dt_bias, norm_w, rel_table, w_a, w_b, w_o,
     ln1_g, ln1_b, w_up, ffn_conv_w, w_dn, w_ple, w_pg, ln2_g, ln2_b) = wts
    b, t, _ = x.shape
    rows = b * t
    x2d = x.reshape(rows, D_MODEL)

    prompt = k_prev is None
    tm_proj = min(tm_proj, t if prompt else rows)
    pad_rows = tm_proj if prompt else 0
    assert pad_rows >= BAND or not prompt
    h2d, ab, q_b, k_b, v_b = _in_proj(x2d, ln_in_g, ln_in_b, w_main, w_ab, tm=tm_proj,
                                      t=t if prompt else rows, pad_rows=pad_rows)

    o_a, gconv_new, s_new = _gdn(h2d, ab, gdn_conv_w, a_log, dt_bias, norm_w,
                                 _pad_state(gconv_prev, SUBLANES), s_gdn.astype(F32), batch=b, t=t)

    k_b = k_b.reshape(b, -1, ATT_W)
    v_b = v_b.reshape(b, -1, ATT_W)
    if prompt:
        bias = _rel_bias(rel_table, CHUNK, BAND, BAND + CHUNK)
        o_b = _attn(q_b, k_b, v_b, bias, batch=b, t=t, tq=CHUNK, width=BAND + CHUNK, pad=BAND,
                    koff=pad_rows - BAND)
        keep = min(BAND, t)
        k_new, v_new = k_b[:, pad_rows + t - keep:], v_b[:, pad_rows + t - keep:]
    else:
        n_past = k_prev.shape[1]
        kk = jnp.concatenate([k_prev.reshape(b, n_past, ATT_W).astype(BF16), k_b], axis=1)
        vv = jnp.concatenate([v_prev.reshape(b, n_past, ATT_W).astype(BF16), v_b], axis=1)
        bias = _rel_bias(rel_table, t, n_past, n_past + t)
        o_b = _attn(q_b, kk, vv, bias, batch=b, t=t, tq=t, width=n_past + t, pad=0, koff=0)
        k_new, v_new = k_b, v_b

    x1 = _merge(x2d, ln_in_g, ln_in_b, o_a, o_b, h2d, w_a, w_b, w_o, ln1_g, ln1_b,
                tm=min(tm_merge, rows))

    y, fconv_new = _ffn(x1, pe.reshape(rows, PLE_DIM), w_up, ffn_conv_w, w_dn, w_ple, w_pg,
                        ln2_g, ln2_b, _pad_state(fconv_prev, SUBLANES), batch=b, t=t,
                        tm=min(tm_ffn, t), fc=256)

    keep_k = k_new.shape[1]
    return (y.reshape(b, t, D_MODEL),
            gconv_new[:, SUBLANES - (GDN_CONV - 1):],
            s_new,
            k_new.astype(F32).reshape(b, keep_k, ATT_HEADS, ATT_DH),
            v_new.astype(F32).reshape(b, keep_k, ATT_HEADS, ATT_DH),
            fconv_new[:, SUBLANES - (FFN_CONV - 1):])


def _pad_lanes(v, width=LANES):
    return jnp.pad(v.astype(F32), (0, width - v.shape[0])).reshape(1, width)


def kernel(x_prompt, x_sample, state_gdn_conv, state_gdn, cache_attn_k, cache_attn_v, state_ffn_conv,
           p_prompt, p_sample, ln_in_g, ln_in_b, w_in, gdn_conv_w, gdn_a_log, gdn_dt_bias, gdn_norm_w,
           att_rel_bias, w_branch_a, w_branch_b, w_out, ln1_g, ln1_b, w_ffn_up, ffn_conv_w, w_ffn_down,
           w_ple, w_ple_gate, ln2_g, ln2_b):
    depth = w_in.shape[0]
    assert depth == 1
    bp = x_prompt.shape[0]
    row = lambda v: v.reshape(1, -1).astype(F32)

    xp, xs = x_prompt, x_sample
    new_p, new_s = [], []
    for i in range(depth):
        wi = w_in[i]
        o_ab = 4 * GDN_W
        o_att = o_ab + 2 * GDN_HEADS
        o_gate = o_att + 3 * ATT_W
        w_main = jnp.concatenate([wi[:, :o_ab], wi[:, o_gate:], wi[:, o_att:o_gate]], axis=1).astype(BF16)
        w_ab = jnp.pad(wi[:, o_ab:o_att], ((0, 0), (0, LANES - 2 * GDN_HEADS))).astype(BF16)
        wts = (row(ln_in_g), row(ln_in_b), w_main, w_ab, gdn_conv_w[i].astype(F32),
               _pad_lanes(gdn_a_log[i]), _pad_lanes(gdn_dt_bias[i]), row(gdn_norm_w[i]), att_rel_bias[i],
               w_branch_a[i].astype(BF16), w_branch_b[i].astype(BF16), w_out[i].astype(BF16),
               row(ln1_g[i]), row(ln1_b[i]), w_ffn_up[i].astype(BF16), ffn_conv_w[i].astype(F32),
               w_ffn_down[i].astype(BF16), w_ple[i].astype(BF16), w_ple_gate[i].astype(BF16),
               row(ln2_g[i]), row(ln2_b[i]))
        xp, st_p = _unpack(_layer(xp, p_prompt[i],
                                  jnp.zeros((bp, GDN_CONV - 1, GDN_CONV_CH), F32),
                                  jnp.zeros((bp, GDN_HEADS, GDN_DK, GDN_DV), F32),
                                  None, None,
                                  jnp.zeros((bp, FFN_CONV - 1, D_FF), F32), wts,
                                  tm_proj=1024, tm_merge=512, tm_ffn=512))
        xs, st_s = _unpack(_layer(xs, p_sample[i], state_gdn_conv[i], state_gdn[i], cache_attn_k[i],
                                  cache_attn_v[i], state_ffn_conv[i], wts,
                                  tm_proj=256, tm_merge=256, tm_ffn=256))
        new_p.append(st_p)
        new_s.append(st_s)
    p_gconv, p_gdn, p_k, p_v, p_fconv = [jnp.stack(a) for a in zip(*new_p)]
    s_gconv, s_gdn, s_k, s_v, s_fconv = [jnp.stack(a) for a in zip(*new_s)]
    return (xp, xs, p_gconv, p_gdn, p_k, p_v, p_fconv, s_gconv, s_gdn, s_k, s_v, s_fconv)


def _unpack(res):
    return res[0], res[1:]
```

```python
import functools
import math

import jax
import jax.numpy as jnp
from jax import lax
from jax.experimental import pallas as pl
from jax.experimental.pallas import tpu as pltpu

F32 = jnp.float32
BF16 = jnp.bfloat16

D_MODEL = 1024
CHUNK = 64
BAND = 512
GDN_HEADS = 8
GDN_DK = 128
GDN_DV = 128
GDN_CONV = 4
ATT_HEADS = 8
ATT_DH = 64
REL_CLIP = 128
D_FF = 2816
FFN_CONV = 3
PLE_DIM = 256
DEEPNORM_ALPHA = 2.0 ** 0.25
LN_EPS = 1e-5
RMS_EPS = 1e-6
L2_EPS = 1e-6

GDN_W = GDN_HEADS * GDN_DK
GDN_CONV_CH = 3 * GDN_W
ATT_W = ATT_HEADS * ATT_DH
LANES = 128
SUBLANES = 8

COL_QKV = 0
COL_Z = 3 * GDN_W
COL_GA = 4 * GDN_W
COL_GB = 5 * GDN_W
H_MAIN = 6 * GDN_W

VMEM_LIMIT = 56 * 1024 * 1024


def _cparams(sem):
    return pltpu.CompilerParams(dimension_semantics=sem, vmem_limit_bytes=VMEM_LIMIT)


def _sigmoid(x):
    return 1.0 / (1.0 + jnp.exp(-x))


def _silu(x):
    return x * _sigmoid(x)


def _layernorm_rows(x, g, b):
    mu = jnp.mean(x, axis=-1, keepdims=True)
    xc = x - mu
    var = jnp.mean(xc * xc, axis=-1, keepdims=True)
    return xc * lax.rsqrt(var + LN_EPS) * g + b


def _dot(a, b):
    return jnp.dot(a.astype(BF16), b.astype(BF16), preferred_element_type=F32)


def _dot_nt(a, b):
    return lax.dot_general(a.astype(BF16), b.astype(BF16), (((1,), (1,)), ((), ())),
                           preferred_element_type=F32)


def _dot_tn(a, b):
    return lax.dot_general(a.astype(BF16), b.astype(BF16), (((0,), (0,)), ((), ())),
                           preferred_element_type=F32)


def _split3(x):
    x1 = x.astype(BF16)
    r1 = x - x1.astype(F32)
    x2 = r1.astype(BF16)
    x3 = (r1 - x2.astype(F32)).astype(BF16)
    return x1, x2, x3


def _in_proj_kernel(x_ref, g_ref, b_ref, w_ref, wab_ref, kz_ref, vz_ref,
                    h_ref, ab_ref, q_ref, k_ref, v_ref, xn_ref, *, sub, n_main):
    del kz_ref, vz_ref
    j = pl.program_id(1)
    tm = x_ref.shape[0]

    @pl.when(j == 0)
    def _():
        def body(i, c):
            r = pl.multiple_of(i * sub, sub)
            xn = _layernorm_rows(x_ref[pl.ds(r, sub), :], g_ref[...], b_ref[...])
            xn_ref[pl.ds(r, sub), :] = xn.astype(BF16)
            return c
        lax.fori_loop(0, tm // sub, body, 0)
        ab_ref[...] = jnp.dot(xn_ref[...], wab_ref[...], preferred_element_type=F32)

    @pl.when(j < n_main)
    def _():
        h_ref[...] = jnp.dot(xn_ref[...], w_ref[...], preferred_element_type=F32).astype(BF16)

    @pl.when(j == n_main)
    def _():
        for o_ref, c0 in ((q_ref, 0), (k_ref, ATT_W), (v_ref, 2 * ATT_W)):
            o_ref[...] = jnp.dot(xn_ref[...], w_ref[:, c0:c0 + ATT_W],
                                 preferred_element_type=F32).astype(BF16)


def _in_proj(x2d, ln_g, ln_b, w_main, w_ab, *, tm, t, pad_rows):
    rows = x2d.shape[0]
    tn = 3 * ATT_W
    n_main = H_MAIN // tn
    assert rows % tm == 0 and t % tm == 0 and pad_rows % tm == 0 and H_MAIN % tn == 0
    sub = min(tm, 128)
    per_stream = t // tm
    stride = per_stream + pad_rows // tm
    kv_rows = (rows // t) * (t + pad_rows)
    kv_block = lambda i, j: ((i // per_stream) * stride + pad_rows // tm + i % per_stream, 0)
    zeros = jnp.zeros((kv_rows, ATT_W), BF16)
    return pl.pallas_call(
        functools.partial(_in_proj_kernel, sub=sub, n_main=n_main),
        grid=(rows // tm, n_main + 1),
        in_specs=[
            pl.BlockSpec((tm, D_MODEL), lambda i, j: (i, 0)),
            pl.BlockSpec((1, D_MODEL), lambda i, j: (0, 0)),
            pl.BlockSpec((1, D_MODEL), lambda i, j: (0, 0)),
            pl.BlockSpec((D_MODEL, tn), lambda i, j: (0, j)),
            pl.BlockSpec((D_MODEL, LANES), lambda i, j: (0, 0)),
            pl.BlockSpec(memory_space=pl.ANY),
            pl.BlockSpec(memory_space=pl.ANY),
        ],
        out_specs=[
            pl.BlockSpec((tm, tn), lambda i, j: (i, jnp.minimum(j, n_main - 1))),
            pl.BlockSpec((tm, LANES), lambda i, j: (i, 0)),
            pl.BlockSpec((tm, ATT_W), lambda i, j: (i, 0)),
            pl.BlockSpec((tm, ATT_W), kv_block),
            pl.BlockSpec((tm, ATT_W), kv_block),
        ],
        out_shape=[
            jax.ShapeDtypeStruct((rows, H_MAIN), BF16),
            jax.ShapeDtypeStruct((rows, LANES), F32),
            jax.ShapeDtypeStruct((rows, ATT_W), BF16),
            jax.ShapeDtypeStruct((kv_rows, ATT_W), BF16),
            jax.ShapeDtypeStruct((kv_rows, ATT_W), BF16),
        ],
        input_output_aliases={5: 3, 6: 4},
        scratch_shapes=[pltpu.VMEM((tm, D_MODEL), BF16)],
        compiler_params=_cparams(("arbitrary", "arbitrary")),
        name="in_proj",
    )(x2d, ln_g, ln_b, w_main, w_ab, zeros, zeros)


def _gdn_kernel(qkv_ref, z_ref, ab_ref, cw_ref, alog_ref, dtb_ref, nw_ref, cinit_ref, s0_ref,
                o_ref, cstate_ref, s_ref,
                xp_ref, q_st, k_st, v_st, dec_st, beta_st, dect_st, lm_st, *, c, nb):
    n = pl.program_id(1)
    pad = SUBLANES
    cur = lax.rem(n, 2)
    prv = 1 - cur
    heads = range(GDN_HEADS)
    pairs = [(bi, h) for bi in range(nb) for h in heads]

    @pl.when(n == 0)
    def _():
        xp_ref[:, 0:pad, :] = cinit_ref[...]
        s_ref[...] = s0_ref[...]
        for st in (q_st, k_st, v_st, dec_st, beta_st, dect_st, lm_st):
            st[1] = jnp.zeros(st.shape[1:], F32)

    row = lax.broadcasted_iota(jnp.int32, (c, c), 0)
    col = lax.broadcasted_iota(jnp.int32, (c, c), 1)
    causal = row >= col
    strict = row > col
    eye = jnp.where(row == col, 1.0, 0.0).astype(F32)
    tri = jnp.where(causal, 1.0, 0.0).astype(BF16)

    for bi in range(nb):
        ab = ab_ref[bi]
        sp_in = ab + dtb_ref[...]
        softplus = jnp.maximum(sp_in, 0.0) + jnp.log(1.0 + jnp.exp(-jnp.abs(sp_in)))
        g_all = -jnp.exp(alog_ref[...]) * softplus
        g1, g2, g3 = _split3(g_all)
        dec_all = (jnp.dot(tri, g1, preferred_element_type=F32)
                   + (jnp.dot(tri, g2, preferred_element_type=F32)
                      + jnp.dot(tri, g3, preferred_element_type=F32)))
        dec_t = dec_all.T
        dec_st[cur, bi] = dec_all
        dect_st[cur, bi] = dec_t
        beta_st[cur, bi] = _sigmoid(ab)
        for h in heads:
            lm_st[cur, bi * GDN_HEADS + h] = jnp.exp(
                jnp.where(causal, dec_all[:, h:h + 1] - dec_t[h:h + 1, :], -jnp.inf))
        xp_ref[bi, pad:pad + c, :] = qkv_ref[bi].astype(F32)

    def conv_silu(bi, c0, width):
        acc = xp_ref[bi, pl.ds(pad, c), pl.ds(c0, width)] * cw_ref[GDN_CONV - 1:GDN_CONV, pl.ds(c0, width)]
        for i in range(1, GDN_CONV):
            acc = acc + (xp_ref[bi, pl.ds(pad - i, c), pl.ds(c0, width)]
                         * cw_ref[GDN_CONV - 1 - i:GDN_CONV - i, pl.ds(c0, width)])
        return _silu(acc)

    def q_job(bi, h):
        lo = h * GDN_DK
        xq = conv_silu(bi, lo, GDN_DK)
        q_st[cur, bi, :, lo:lo + GDN_DK] = (
            xq * lax.rsqrt(jnp.sum(xq * xq, axis=-1, keepdims=True) + L2_EPS) * (GDN_DK ** -0.5))

    def k_job(bi, h):
        lo = h * GDN_DK
        xk = conv_silu(bi, GDN_W + lo, GDN_DK)
        k_st[cur, bi, :, lo:lo + GDN_DK] = (
            xk * lax.rsqrt(jnp.sum(xk * xk, axis=-1, keepdims=True) + L2_EPS))

    def v_job(bi, h):
        lo = h * GDN_DV
        v_st[cur, bi, :, lo:lo + GDN_DV] = conv_silu(bi, 2 * GDN_W + lo, GDN_DV)

    jobs = [functools.partial(job, bi, h) for bi, h in pairs for job in (q_job, k_job, v_job)]
    n_stage = 2 * (int(math.log2(c)) - 1) + 4
    per_stage = -(-len(jobs) // n_stage)

    def front_jobs():
        for job in jobs[:per_stage]:
            job()
        del jobs[:per_stage]

    def cols(ref, bi, h, width=GDN_DK):
        return ref[prv, bi, :, h * width:(h + 1) * width]

    qn = [cols(q_st, bi, h) for bi, h in pairs]
    kn = [cols(k_st, bi, h) for bi, h in pairs]
    xv = [cols(v_st, bi, h, GDN_DV) for bi, h in pairs]
    dcol = [dec_st[prv, bi, :, h:h + 1] for bi, h in pairs]
    beta = [beta_st[prv, bi, :, GDN_HEADS + h:GDN_HEADS + h + 1] for bi, h in pairs]
    lmask = [lm_st[prv, bi * GDN_HEADS + h] for bi, h in pairs]
    idx = range(len(pairs))
    kb = [kn[i] * beta[i] for i in idx]
    e_d = [jnp.exp(d) for d in dcol]
    gram = [_dot_nt(jnp.concatenate([kb[i], qn[i]], axis=0), kn[i]) for i in idx]
    front_jobs()
    m =[jnp.where(strict, gram[i][:c] * lmask[i], 0.0) for i in idx]
    qk = [gram[i][c:] * lmask[i] for i in idx]
    p = [eye - m[i] for i in idx]
    mk = m
    for _ in range(int(math.log2(c)) - 1):
        mk = [_dot(x, x) for x in mk]
        front_jobs()
        p = [p[i] + _dot(p[i], mk[i]) for i in idx]
        front_jobs()
    sol = [_dot(p[i], jnp.concatenate([xv[i] * beta[i], kb[i] * e_d[i]], axis=1)) for i in idx]
    front_jobs()
    s_old = [s_ref[bi, h] for bi, h in pairs]
    ws_qs = [_dot(jnp.concatenate([sol[i][:, GDN_DV:], qn[i] * e_d[i]], axis=0), s_old[i])
             for i in idx]
    front_jobs()
    v_new = [sol[i][:, :GDN_DV] - ws_qs[i][:c] for i in idx]
    dl = [d[c - 1:c, :] for d in dcol]
    kd = [kn[i] * jnp.exp(dl[i] - dcol[i]) for i in idx]
    s_upd = [_dot_tn(kd[i], v_new[i]) for i in idx]
    o_in = [_dot(qk[i], v_new[i]) for i in idx]
    front_jobs()
    assert not jobs
    for i, (bi, h) in enumerate(pairs):
        s_ref[bi, h] = s_old[i] * jnp.exp(dl[i]) + s_upd[i]
        o = ws_qs[i][c:] + o_in[i]
        zh = z_ref[bi, :, pl.ds(h * GDN_DV, GDN_DV)].astype(F32)
        o = (o * lax.rsqrt(jnp.mean(o * o, axis=-1, keepdims=True) + RMS_EPS) * nw_ref[...]
             * _silu(zh))
        o_ref[bi, :, pl.ds(h * GDN_DV, GDN_DV)] = o.astype(BF16)
    for bi in range(nb):
        tail = xp_ref[bi, c:c + pad, :]
        xp_ref[bi, 0:pad, :] = tail
        cstate_ref[bi] = tail


def _gdn(h2d, ab, conv_w, a_log, dt_bias, norm_w, conv_init, s0, *, batch, t, nb):
    c = min(CHUNK, t)
    nc = t // c
    assert batch % nb == 0
    h3d = h2d.reshape(batch, t, H_MAIN)
    ab3d = ab.reshape(batch, t, LANES)
    front = lambda b, n: (b, jnp.minimum(n, nc - 1), 0)
    back = lambda b, n: (b, jnp.maximum(n - 1, 0), 0)
    stash = lambda *shape: pltpu.VMEM((2, *shape), F32)
    o_a, cstate, s_new = pl.pallas_call(
        functools.partial(_gdn_kernel, c=c, nb=nb),
        grid=(batch // nb, nc + 1),
        in_specs=[
            pl.BlockSpec((nb, c, GDN_CONV_CH), front),
            pl.BlockSpec((nb, c, GDN_W), lambda b, n: (b, jnp.maximum(n - 1, 0), COL_Z // GDN_W)),
            pl.BlockSpec((nb, c, LANES), front),
            pl.BlockSpec((GDN_CONV, GDN_CONV_CH), lambda b, n: (0, 0)),
            pl.BlockSpec((1, LANES), lambda b, n: (0, 0)),
            pl.BlockSpec((1, LANES), lambda b, n: (0, 0)),
            pl.BlockSpec((1, GDN_DV), lambda b, n: (0, 0)),
            pl.BlockSpec((nb, SUBLANES, GDN_CONV_CH), lambda b, n: (b, 0, 0)),
            pl.BlockSpec((nb, GDN_HEADS, GDN_DK, GDN_DV), lambda b, n: (b, 0, 0, 0)),
        ],
        out_specs=[
            pl.BlockSpec((nb, c, GDN_W), back),
            pl.BlockSpec((nb, SUBLANES, GDN_CONV_CH), lambda b, n: (b, 0, 0)),
            pl.BlockSpec((nb, GDN_HEADS, GDN_DK, GDN_DV), lambda b, n: (b, 0, 0, 0)),
        ],
        out_shape=[
            jax.ShapeDtypeStruct((batch, t, GDN_W), BF16),
            jax.ShapeDtypeStruct((batch, SUBLANES, GDN_CONV_CH), F32),
            jax.ShapeDtypeStruct((batch, GDN_HEADS, GDN_DK, GDN_DV), F32),
        ],
        scratch_shapes=[
            pltpu.VMEM((nb, SUBLANES + c, GDN_CONV_CH), F32),
            stash(nb, c, GDN_W), stash(nb, c, GDN_W), stash(nb, c, GDN_W),
            stash(nb, c, LANES), stash(nb, c, LANES), stash(nb, LANES, c),
            stash(nb * GDN_HEADS, c, c),
        ],
        compiler_params=_cparams(("arbitrary", "arbitrary")),
        name="gdn",
    )(h3d, h3d, ab3d, conv_w, a_log, dt_bias, norm_w, conv_init, s0)
    return o_a.reshape(batch * t, GDN_W), cstate, s_new


def _attn_kernel(q_ref, k_ref, v_ref, bias_ref, o_ref, *, tq, width, pad, koff):
    n = pl.program_id(1)
    start = pl.multiple_of(koff + n * tq, tq)
    kwin = k_ref[0, pl.ds(start, width), :]
    vwin = v_ref[0, pl.ds(start, width), :]
    kpos = n * tq - pad + lax.broadcasted_iota(jnp.int32, (tq, width), 1)
    valid = kpos >= 0
    scale = ATT_DH ** -0.5
    heads = range(ATT_HEADS)
    cols = [slice(h * ATT_DH, (h + 1) * ATT_DH) for h in heads]
    s = [_dot_nt(q_ref[:, cols[h]], kwin[:, cols[h]]) for h in heads]
    s = [jnp.where(valid, s[h] * scale + bias_ref[h], -jnp.inf) for h in heads]
    p = [jnp.exp(x - jnp.max(x, axis=-1, keepdims=True)) for x in s]
    l = [jnp.sum(x, axis=-1, keepdims=True) for x in p]
    pv = [_dot(p[h], vwin[:, cols[h]]) for h in heads]
    o_ref[...] = jnp.concatenate([pv[h] / l[h] for h in heads], axis=1).astype(BF16)


def _attn(q2d, k3d, v3d, bias, *, batch, t, tq, width, pad, koff):
    nq = t // tq
    tk = k3d.shape[1]
    assert koff % tq == 0 and koff + (nq - 1) * tq + width <= tk
    return pl.pallas_call(
        functools.partial(_attn_kernel, tq=tq, width=width, pad=pad, koff=koff),
        grid=(batch, nq),
        in_specs=[
            pl.BlockSpec((tq, ATT_W), lambda b, n: (b * nq + n, 0)),
            pl.BlockSpec((1, tk, ATT_W), lambda b, n: (b, 0, 0)),
            pl.BlockSpec((1, tk, ATT_W), lambda b, n: (b, 0, 0)),
            pl.BlockSpec((ATT_HEADS, tq, width), lambda b, n: (0, 0, 0)),
        ],
        out_specs=pl.BlockSpec((tq, ATT_W), lambda b, n: (b * nq + n, 0)),
        out_shape=jax.ShapeDtypeStruct((batch * t, ATT_W), BF16),
        compiler_params=_cparams(("arbitrary", "arbitrary")),
        name="attn",
    )(q2d, k3d, v3d, bias)


def _merge_kernel(x_ref, g_ref, b_ref, oa_ref, ob_ref, ga_ref, gb_ref, wa_ref, wb_ref, wo_ref,
                  g1_ref, b1_ref, x1_ref):
    xn = _layernorm_rows(x_ref[...], g_ref[...], b_ref[...])
    ya = jnp.dot(oa_ref[...], wa_ref[...], preferred_element_type=F32)
    yb = jnp.dot(ob_ref[...], wb_ref[...], preferred_element_type=F32)
    mixed = (_sigmoid(ga_ref[...].astype(F32)) * ya + _sigmoid(gb_ref[...].astype(F32)) * yb)
    y = DEEPNORM_ALPHA * xn + _dot(mixed, wo_ref[...])
    x1_ref[...] = _layernorm_rows(y, g1_ref[...], b1_ref[...])


def _merge(x2d, ln_g, ln_b, oa, ob, h2d, wa, wb, wo, ln1_g, ln1_b, *, tm):
    rows = x2d.shape[0]
    full = lambda i: (0, 0)
    return pl.pallas_call(
        _merge_kernel,
        grid=(rows // tm,),
        in_specs=[
            pl.BlockSpec((tm, D_MODEL), lambda i: (i, 0)),
            pl.BlockSpec((1, D_MODEL), full),
            pl.BlockSpec((1, D_MODEL), full),
            pl.BlockSpec((tm, GDN_W), lambda i: (i, 0)),
            pl.BlockSpec((tm, ATT_W), lambda i: (i, 0)),
            pl.BlockSpec((tm, D_MODEL), lambda i: (i, COL_GA // D_MODEL)),
            pl.BlockSpec((tm, D_MODEL), lambda i: (i, COL_GB // D_MODEL)),
            pl.BlockSpec((GDN_W, D_MODEL), full),
            pl.BlockSpec((ATT_W, D_MODEL), full),
            pl.BlockSpec((D_MODEL, D_MODEL), full),
            pl.BlockSpec((1, D_MODEL), full),
            pl.BlockSpec((1, D_MODEL), full),
        ],
        out_specs=pl.BlockSpec((tm, D_MODEL), lambda i: (i, 0)),
        out_shape=jax.ShapeDtypeStruct((rows, D_MODEL), F32),
        compiler_params=_cparams(("arbitrary",)),
        name="merge",
    )(x2d, ln_g, ln_b, oa, ob, h2d, h2d, wa, wb, wo, ln1_g, ln1_b)


def _gelu_tanh(x):
    return 0.5 * x * (1.0 + jnp.tanh(math.sqrt(2.0 / math.pi) * (x + 0.044715 * (x * x * x))))


def _ffn_kernel(x1_ref, pe_ref, wup_ref, cw_ref, wdn_ref, wple_ref, wpg_ref, g2_ref, b2_ref,
                finit_ref, y_ref, fstate_ref, gp_ref, acc_ref, *, tm, fc):
    n = pl.program_id(1)
    pad = SUBLANES

    @pl.when(n == 0)
    def _():
        gp_ref[0:pad, :] = finit_ref[0]

    x1 = x1_ref[...]
    x1b = x1.astype(BF16)

    def up(k):
        c0 = k * fc
        gate = jnp.dot(x1b, wup_ref[:, c0:c0 + fc], preferred_element_type=F32)
        val = jnp.dot(x1b, wup_ref[:, D_FF + c0:D_FF + c0 + fc], preferred_element_type=F32)
        gp_ref[pad:pad + tm, c0:c0 + fc] = gate
        return gate, val

    nk = D_FF // fc
    nxt = up(0)
    ple = jnp.dot(pe_ref[...].astype(BF16), wple_ref[...], preferred_element_type=F32)
    for k in range(nk):
        c0 = k * fc
        gate, val = nxt
        if k + 1 < nk:
            nxt = up(k + 1)
        conv = gate * cw_ref[FFN_CONV - 1:FFN_CONV, c0:c0 + fc]
        for i in range(1, FFN_CONV):
            conv = conv + (gp_ref[pl.ds(pad - i, tm), c0:c0 + fc]
                           * cw_ref[FFN_CONV - 1 - i:FFN_CONV - i, c0:c0 + fc])
        act = (_gelu_tanh(conv) * val).astype(BF16)
        down = jnp.dot(act, wdn_ref[c0:c0 + fc, :], preferred_element_type=F32)
        if k == 0:
            acc_ref[...] = down
        else:
            acc_ref[...] += down
    tail = gp_ref[tm:tm + pad, :]
    gp_ref[0:pad, :] = tail
    fstate_ref[0] = tail

    r = DEEPNORM_ALPHA * x1 + acc_ref[...]
    gate_p = _sigmoid(_dot(r, wpg_ref[...]))
    r = r + gate_p * ple
    y_ref[...] = _layernorm_rows(r, g2_ref[...], b2_ref[...])


def _ffn(x1, pe, wup, conv_w, wdn, wple, wpg, ln2_g, ln2_b, finit, *, batch, t, tm, fc):
    nt = t // tm
    rows = batch * t
    full = lambda b, n: (0, 0)
    return pl.pallas_call(
        functools.partial(_ffn_kernel, tm=tm, fc=fc),
        grid=(batch, nt),
        in_specs=[
            pl.BlockSpec((tm, D_MODEL), lambda b, n: (b * nt + n, 0)),
            pl.BlockSpec((tm, PLE_DIM), lambda b, n: (b * nt + n, 0)),
            pl.BlockSpec((D_MODEL, 2 * D_FF), full, pipeline_mode=pl.Buffered(1)),
            pl.BlockSpec((FFN_CONV, D_FF), full),
            pl.BlockSpec((D_FF, D_MODEL), full, pipeline_mode=pl.Buffered(1)),
            pl.BlockSpec((PLE_DIM, D_MODEL), full),
            pl.BlockSpec((D_MODEL, D_MODEL), full),
            pl.BlockSpec((1, D_MODEL), full),
            pl.BlockSpec((1, D_MODEL), full),
            pl.BlockSpec((1, SUBLANES, D_FF), lambda b, n: (b, 0, 0)),
        ],
        out_specs=[
            pl.BlockSpec((tm, D_MODEL), lambda b, n: (b * nt + n, 0)),
            pl.BlockSpec((1, SUBLANES, D_FF), lambda b, n: (b, 0, 0)),
        ],
        out_shape=[
            jax.ShapeDtypeStruct((rows, D_MODEL), F32),
            jax.ShapeDtypeStruct((batch, SUBLANES, D_FF), F32),
        ],
        scratch_shapes=[pltpu.VMEM((SUBLANES + tm, D_FF), F32), pltpu.VMEM((tm, D_MODEL), F32)],
        compiler_params=_cparams(("arbitrary", "arbitrary")),
        name="ffn",
    )(x1, pe, wup, conv_w, wdn, wple, wpg, ln2_g, ln2_b, finit)


def _rel_bias(table, n_q, n_past, n_k):
    span = n_q + n_k - 1
    rel = n_q - 1 + n_past - jnp.arange(span)
    rev = table[:, jnp.clip(rel, -REL_CLIP, REL_CLIP) + REL_CLIP].astype(F32)
    heads = table.shape[0]
    period = jnp.pad(rev, ((0, 0), (0, 1)))
    skew = jnp.tile(period, (1, n_q))[:, :n_q * span].reshape(heads, n_q, span)
    return skew[:, :, n_q - 1:n_q - 1 + n_k]


def _pad_state(state, rows):
    b, r, ch = state.shape
    return jnp.concatenate([jnp.zeros((b, rows - r, ch), F32), state.astype(F32)], axis=1)


def _layer(x, pe, gconv_prev, s_gdn, k_prev, v_prev, fconv_prev, wts, *, tm_proj, tm_merge, tm_ffn, gdn_nb):
    (ln_in_g, ln_in_b, w_main, w_ab, gdn_conv_w, a_log, dt_bias, norm_w, rel_table, w_a, w_b, w_o,
     ln1_g, ln1_b, w_up, ffn_conv_w, w_dn, w_ple, w_pg, ln2_g, ln2_b) = wts
    b, t, _ = x.shape
    rows = b * t
    x2d = x.reshape(rows, D_MODEL)

    prompt = k_prev is None
    tm_proj = min(tm_proj, t if prompt else rows)
    pad_rows = tm_proj if prompt else 0
    assert pad_rows >= BAND or not prompt
    h2d, ab, q_b, k_b, v_b = _in_proj(x2d, ln_in_g, ln_in_b, w_main, w_ab, tm=tm_proj,
                                      t=t if prompt else rows, pad_rows=pad_rows)

    o_a, gconv_new, s_new = _gdn(h2d, ab, gdn_conv_w, a_log, dt_bias, norm_w,
                                 _pad_state(gconv_prev, SUBLANES), s_gdn.astype(F32), batch=b, t=t,
                                 nb=gdn_nb)

    k_b = k_b.reshape(b, -1, ATT_W)
    v_b = v_b.reshape(b, -1, ATT_W)
    if prompt:
        bias = _rel_bias(rel_table, CHUNK, BAND, BAND + CHUNK)
        o_b = _attn(q_b, k_b, v_b, bias, batch=b, t=t, tq=CHUNK, width=BAND + CHUNK, pad=BAND,
                    koff=pad_rows - BAND)
        keep = min(BAND, t)
        k_new, v_new = k_b[:, pad_rows + t - keep:], v_b[:, pad_rows + t - keep:]
    else:
        n_past = k_prev.shape[1]
        kk = jnp.concatenate([k_prev.reshape(b, n_past, ATT_W).astype(BF16), k_b], axis=1)
        vv = jnp.concatenate([v_prev.reshape(b, n_past, ATT_W).astype(BF16), v_b], axis=1)
        bias = _rel_bias(rel_table, t, n_past, n_past + t)
        o_b = _attn(q_b, kk, vv, bias, batch=b, t=t, tq=t, width=n_past + t, pad=0, koff=0)
        k_new, v_new = k_b, v_b

    x1 = _merge(x2d, ln_in_g, ln_in_b, o_a, o_b, h2d, w_a, w_b, w_o, ln1_g, ln1_b,
                tm=min(tm_merge, rows))

    y, fconv_new = _ffn(x1, pe.reshape(rows, PLE_DIM), w_up, ffn_conv_w, w_dn, w_ple, w_pg,
                        ln2_g, ln2_b, _pad_state(fconv_prev, SUBLANES), batch=b, t=t,
                        tm=min(tm_ffn, t), fc=256)

    keep_k = k_new.shape[1]
    return (y.reshape(b, t, D_MODEL),
            gconv_new[:, SUBLANES - (GDN_CONV - 1):],
            s_new,
            k_new.astype(F32).reshape(b, keep_k, ATT_HEADS, ATT_DH),
            v_new.astype(F32).reshape(b, keep_k, ATT_HEADS, ATT_DH),
            fconv_new[:, SUBLANES - (FFN_CONV - 1):])


def _pad_lanes(v, width=LANES):
    return jnp.pad(v.astype(F32), (0, width - v.shape[0])).reshape(1, width)


def kernel(x_prompt, x_sample, state_gdn_conv, state_gdn, cache_attn_k, cache_attn_v, state_ffn_conv,
           p_prompt, p_sample, ln_in_g, ln_in_b, w_in, gdn_conv_w, gdn_a_log, gdn_dt_bias, gdn_norm_w,
           att_rel_bias, w_branch_a, w_branch_b, w_out, ln1_g, ln1_b, w_ffn_up, ffn_conv_w, w_ffn_down,
           w_ple, w_ple_gate, ln2_g, ln2_b):
    depth = w_in.shape[0]
    assert depth == 1
    bp = x_prompt.shape[0]
    row = lambda v: v.reshape(1, -1).astype(F32)

    xp, xs = x_prompt, x_sample
    new_p, new_s = [], []
    for i in range(depth):
        wi = w_in[i]
        o_ab = 4 * GDN_W
        o_att = o_ab + 2 * GDN_HEADS
        o_gate = o_att + 3 * ATT_W
        w_main = jnp.concatenate([wi[:, :o_ab], wi[:, o_gate:], wi[:, o_att:o_gate]], axis=1).astype(BF16)
        w_ab = jnp.pad(wi[:, o_ab:o_att], ((0, 0), (0, LANES - 2 * GDN_HEADS))).astype(BF16)
        wts = (row(ln_in_g), row(ln_in_b), w_main, w_ab, gdn_conv_w[i].astype(F32),
               _pad_lanes(gdn_a_log[i]), _pad_lanes(gdn_dt_bias[i]), row(gdn_norm_w[i]), att_rel_bias[i],
               w_branch_a[i].astype(BF16), w_branch_b[i].astype(BF16), w_out[i].astype(BF16),
               row(ln1_g[i]), row(ln1_b[i]), w_ffn_up[i].astype(BF16), ffn_conv_w[i].astype(F32),
               w_ffn_down[i].astype(BF16), w_ple[i].astype(BF16), w_ple_gate[i].astype(BF16),
               row(ln2_g[i]), row(ln2_b[i]))
        xp, st_p = _unpack(_layer(xp, p_prompt[i],
                                  jnp.zeros((bp, GDN_CONV - 1, GDN_CONV_CH), F32),
                                  jnp.zeros((bp, GDN_HEADS, GDN_DK, GDN_DV), F32),
                                  None, None,
                                  jnp.zeros((bp, FFN_CONV - 1, D_FF), F32), wts,
                                  tm_proj=1024, tm_merge=512, tm_ffn=512, gdn_nb=2))
        xs, st_s = _unpack(_layer(xs, p_sample[i], state_gdn_conv[i], state_gdn[i], cache_attn_k[i],
                                  cache_attn_v[i], state_ffn_conv[i], wts,
                                  tm_proj=256, tm_merge=256, tm_ffn=256, gdn_nb=4))
        new_p.append(st_p)
        new_s.append(st_s)
    p_gconv, p_gdn, p_k, p_v, p_fconv = [jnp.stack(a) for a in zip(*new_p)]
    s_gconv, s_gdn, s_k, s_v, s_fconv = [jnp.stack(a) for a in zip(*new_s)]
    return (xp, xs, p_gconv, p_gdn, p_k, p_v, p_fconv, s_gconv, s_gdn, s_k, s_v, s_fconv)


def _unpack(res):
    return res[0], res[1:]
```

```python
import functools
import math

import jax
import jax.numpy as jnp
from jax import lax
from jax.experimental import pallas as pl
from jax.experimental.pallas import tpu as pltpu

F32 = jnp.float32
BF16 = jnp.bfloat16

D_MODEL = 1024
CHUNK = 64
BAND = 512
GDN_HEADS = 8
GDN_DK = 128
GDN_DV = 128
GDN_CONV = 4
ATT_HEADS = 8
ATT_DH = 64
REL_CLIP = 128
D_FF = 2816
FFN_CONV = 3
PLE_DIM = 256
DEEPNORM_ALPHA = 2.0 ** 0.25
LN_EPS = 1e-5
RMS_EPS = 1e-6
L2_EPS = 1e-6

GDN_W = GDN_HEADS * GDN_DK
GDN_CONV_CH = 3 * GDN_W
ATT_W = ATT_HEADS * ATT_DH
LANES = 128
SUBLANES = 8

COL_QKV = 0
COL_Z = 3 * GDN_W
COL_GA = 4 * GDN_W
COL_GB = 5 * GDN_W
H_MAIN = 6 * GDN_W

VMEM_LIMIT = 56 * 1024 * 1024


def _cparams(sem):
    return pltpu.CompilerParams(dimension_semantics=sem, vmem_limit_bytes=VMEM_LIMIT)


def _sigmoid(x):
    return 1.0 / (1.0 + jnp.exp(-x))


def _silu(x):
    return x * _sigmoid(x)


def _layernorm_rows(x, g, b):
    mu = jnp.mean(x, axis=-1, keepdims=True)
    xc = x - mu
    var = jnp.mean(xc * xc, axis=-1, keepdims=True)
    return xc * lax.rsqrt(var + LN_EPS) * g + b


def _dot(a, b):
    return jnp.dot(a.astype(BF16), b.astype(BF16), preferred_element_type=F32)


def _dot_nt(a, b):
    return lax.dot_general(a.astype(BF16), b.astype(BF16), (((1,), (1,)), ((), ())),
                           preferred_element_type=F32)


def _dot_tn(a, b):
    return lax.dot_general(a.astype(BF16), b.astype(BF16), (((0,), (0,)), ((), ())),
                           preferred_element_type=F32)


def _split3(x):
    x1 = x.astype(BF16)
    r1 = x - x1.astype(F32)
    x2 = r1.astype(BF16)
    x3 = (r1 - x2.astype(F32)).astype(BF16)
    return x1, x2, x3


def _in_proj_kernel(x_ref, g_ref, b_ref, wa_ref, watt_ref, wg_ref, wab_ref, kz_ref, vz_ref,
                    h_ref, ab_ref, q_ref, k_ref, v_ref, xn_ref, *, sub, n_a):
    del kz_ref, vz_ref
    j = pl.program_id(1)
    tm = x_ref.shape[0]

    @pl.when(j == 0)
    def _():
        def body(i, c):
            r = pl.multiple_of(i * sub, sub)
            xn = _layernorm_rows(x_ref[pl.ds(r, sub), :], g_ref[...], b_ref[...])
            xn_ref[pl.ds(r, sub), :] = xn.astype(BF16)
            return c
        lax.fori_loop(0, tm // sub, body, 0)
        ab_ref[...] = jnp.dot(xn_ref[...], wab_ref[...], preferred_element_type=F32)

    @pl.when(j < n_a)
    def _():
        h_ref[...] = jnp.dot(xn_ref[...], wa_ref[...], preferred_element_type=F32).astype(BF16)

    @pl.when(j == n_a)
    def _():
        for o_ref, c0 in ((q_ref, 0), (k_ref, ATT_W), (v_ref, 2 * ATT_W)):
            o_ref[...] = jnp.dot(xn_ref[...], watt_ref[:, c0:c0 + ATT_W],
                                 preferred_element_type=F32).astype(BF16)

    @pl.when(j == n_a + 1)
    def _():
        h_ref[...] = jnp.dot(xn_ref[...], wg_ref[...], preferred_element_type=F32).astype(BF16)


def _in_proj(x2d, ln_g, ln_b, w_a, w_att, w_g, w_ab, *, tm, t, pad_rows):
    rows = x2d.shape[0]
    tn = w_g.shape[1]
    n_a = w_a.shape[1] // tn
    assert rows % tm == 0 and t % tm == 0 and pad_rows % tm == 0
    assert w_a.shape[1] == n_a * tn and w_a.shape[1] + tn == H_MAIN and w_att.shape[1] == 3 * ATT_W
    sub = min(tm, 128)
    per_stream = t // tm
    stride = per_stream + pad_rows // tm
    kv_rows = (rows // t) * (t + pad_rows)
    kv_block = lambda i, j: ((i // per_stream) * stride + pad_rows // tm + i % per_stream, 0)
    zeros = jnp.zeros((kv_rows, ATT_W), BF16)
    const = lambda i, j: (0, 0)
    return pl.pallas_call(
        functools.partial(_in_proj_kernel, sub=sub, n_a=n_a),
        grid=(rows // tm, n_a + 2),
        in_specs=[
            pl.BlockSpec((tm, D_MODEL), lambda i, j: (i, 0)),
            pl.BlockSpec((1, D_MODEL), const),
            pl.BlockSpec((1, D_MODEL), const),
            pl.BlockSpec((D_MODEL, tn), lambda i, j: (0, jnp.minimum(j, n_a - 1))),
            pl.BlockSpec((D_MODEL, 3 * ATT_W), const, pipeline_mode=pl.Buffered(1)),
            pl.BlockSpec((D_MODEL, tn), const, pipeline_mode=pl.Buffered(1)),
            pl.BlockSpec((D_MODEL, LANES), const),
            pl.BlockSpec(memory_space=pl.ANY),
            pl.BlockSpec(memory_space=pl.ANY),
        ],
        out_specs=[
            pl.BlockSpec((tm, tn), lambda i, j: (i, jnp.where(j < n_a, j, jnp.maximum(j - 1, n_a - 1)))),
            pl.BlockSpec((tm, LANES), lambda i, j: (i, 0)),
            pl.BlockSpec((tm, ATT_W), lambda i, j: (i, 0)),
            pl.BlockSpec((tm, ATT_W), kv_block),
            pl.BlockSpec((tm, ATT_W), kv_block),
        ],
        out_shape=[
            jax.ShapeDtypeStruct((rows, H_MAIN), BF16),
            jax.ShapeDtypeStruct((rows, LANES), F32),
            jax.ShapeDtypeStruct((rows, ATT_W), BF16),
            jax.ShapeDtypeStruct((kv_rows, ATT_W), BF16),
            jax.ShapeDtypeStruct((kv_rows, ATT_W), BF16),
        ],
        input_output_aliases={7: 3, 8: 4},
        scratch_shapes=[pltpu.VMEM((tm, D_MODEL), BF16)],
        compiler_params=_cparams(("arbitrary", "arbitrary")),
        name="in_proj",
    )(x2d, ln_g, ln_b, w_a, w_att, w_g, w_ab, zeros, zeros)


def _gdn_kernel(qkv_ref, z_ref, ab_ref, cw_ref, alog_ref, dtb_ref, nw_ref, cinit_ref, s0_ref,
                o_ref, cstate_ref, s_ref,
                xp_ref, q_st, k_st, v_st, dec_st, beta_st, dect_st, lm_st, *, c, nb):
    n = pl.program_id(1)
    pad = SUBLANES
    cur = lax.rem(n, 2)
    prv = 1 - cur
    heads = range(GDN_HEADS)
    pairs = [(bi, h) for bi in range(nb) for h in heads]

    @pl.when(n == 0)
    def _():
        xp_ref[:, 0:pad, :] = cinit_ref[...]
        s_ref[...] = s0_ref[...]
        for st in (q_st, k_st, v_st, dec_st, beta_st, dect_st, lm_st):
            st[1] = jnp.zeros(st.shape[1:], F32)

    row = lax.broadcasted_iota(jnp.int32, (c, c), 0)
    col = lax.broadcasted_iota(jnp.int32, (c, c), 1)
    causal = row >= col
    strict = row > col
    eye = jnp.where(row == col, 1.0, 0.0).astype(F32)
    tri = jnp.where(causal, 1.0, 0.0).astype(BF16)

    for bi in range(nb):
        ab = ab_ref[bi]
        sp_in = ab + dtb_ref[...]
        softplus = jnp.maximum(sp_in, 0.0) + jnp.log(1.0 + jnp.exp(-jnp.abs(sp_in)))
        g_all = -jnp.exp(alog_ref[...]) * softplus
        g1, g2, g3 = _split3(g_all)
        dec_all = (jnp.dot(tri, g1, preferred_element_type=F32)
                   + (jnp.dot(tri, g2, preferred_element_type=F32)
                      + jnp.dot(tri, g3, preferred_element_type=F32)))
        dec_t = dec_all.T
        dec_st[cur, bi] = dec_all
        dect_st[cur, bi] = dec_t
        beta_st[cur, bi] = _sigmoid(ab)
        for h in heads:
            lm_st[cur, bi * GDN_HEADS + h] = jnp.exp(
                jnp.where(causal, dec_all[:, h:h + 1] - dec_t[h:h + 1, :], -jnp.inf))

        xp_ref[bi, pad:pad + c, :] = qkv_ref[bi].astype(F32)

    def conv_silu(bi, c0, width):
        acc = xp_ref[bi, pl.ds(pad, c), pl.ds(c0, width)] * cw_ref[GDN_CONV - 1:GDN_CONV, pl.ds(c0, width)]
        for i in range(1, GDN_CONV):
            acc = acc + (xp_ref[bi, pl.ds(pad - i, c), pl.ds(c0, width)]
                         * cw_ref[GDN_CONV - 1 - i:GDN_CONV - i, pl.ds(c0, width)])
        return _silu(acc)

    def q_job(bi, h):
        lo = h * GDN_DK
        xq = conv_silu(bi, lo, GDN_DK)
        q_st[cur, bi, :, lo:lo + GDN_DK] = (
            xq * lax.rsqrt(jnp.sum(xq * xq, axis=-1, keepdims=True) + L2_EPS) * (GDN_DK ** -0.5))

    def k_job(bi, h):
        lo = h * GDN_DK
        xk = conv_silu(bi, GDN_W + lo, GDN_DK)
        k_st[cur, bi, :, lo:lo + GDN_DK] = (
            xk * lax.rsqrt(jnp.sum(xk * xk, axis=-1, keepdims=True) + L2_EPS))

    def v_job(bi, h):
        lo = h * GDN_DV
        v_st[cur, bi, :, lo:lo + GDN_DV] = conv_silu(bi, 2 * GDN_W + lo, GDN_DV)

    jobs = [functools.partial(job, bi, h) for bi, h in pairs for job in (q_job, k_job, v_job)]
    n_stage = 2 * (int(math.log2(c)) - 1) + 4
    per_stage = -(-len(jobs) // n_stage)

    def front_jobs():
        for job in jobs[:per_stage]:
            job()
        del jobs[:per_stage]

    def cols(ref, bi, h, width=GDN_DK):
        return ref[prv, bi, :, h * width:(h + 1) * width]

    qn = [cols(q_st, bi, h) for bi, h in pairs]
    kn = [cols(k_st, bi, h) for bi, h in pairs]
    xv = [cols(v_st, bi, h, GDN_DV) for bi, h in pairs]
    dcol = [dec_st[prv, bi, :, h:h + 1] for bi, h in pairs]
    beta = [beta_st[prv, bi, :, GDN_HEADS + h:GDN_HEADS + h + 1] for bi, h in pairs]
    lmask = [lm_st[prv, bi * GDN_HEADS + h] for bi, h in pairs]
    idx = range(len(pairs))
    kb = [kn[i] * beta[i] for i in idx]
    e_d = [jnp.exp(d) for d in dcol]
    gram = [_dot_nt(jnp.concatenate([kb[i], qn[i]], axis=0), kn[i]) for i in idx]
    front_jobs()
    m = [jnp.where(strict, gram[i][:c] * lmask[i], 0.0) for i in idx]
    qk = [gram[i][c:] * lmask[i] for i in idx]
    p = [eye - m[i] for i in idx]
    mk = m
    for _ in range(int(math.log2(c)) - 1):
        mk = [_dot(x, x) for x in mk]
        front_jobs()
        p = [p[i] + _dot(p[i], mk[i]) for i in idx]
        front_jobs()
    sol = [_dot(p[i], jnp.concatenate([xv[i] * beta[i], kb[i] * e_d[i]], axis=1)) for i in idx]
    front_jobs()
    s_old = [s_ref[bi, h] for bi, h in pairs]
    ws_qs = [_dot(jnp.concatenate([sol[i][:, GDN_DV:], qn[i] * e_d[i]], axis=0), s_old[i])
             for i in idx]
    front_jobs()
    v_new = [sol[i][:, :GDN_DV] - ws_qs[i][:c] for i in idx]
    dl = [d[c - 1:c, :] for d in dcol]
    kd = [kn[i] * jnp.exp(dl[i] - dcol[i]) for i in idx]
    s_upd = [_dot_tn(kd[i], v_new[i]) for i in idx]
    o_in = [_dot(qk[i], v_new[i]) for i in idx]
    front_jobs()
    assert not jobs
    for i, (bi, h) in enumerate(pairs):
        s_ref[bi, h] = s_old[i] * jnp.exp(dl[i]) + s_upd[i]
        o = ws_qs[i][c:] + o_in[i]
        zh = z_ref[bi, :, pl.ds(h * GDN_DV, GDN_DV)].astype(F32)
        o = (o * lax.rsqrt(jnp.mean(o * o, axis=-1, keepdims=True) + RMS_EPS) * nw_ref[...]
             * _silu(zh))
        o_ref[bi, :, pl.ds(h * GDN_DV, GDN_DV)] = o.astype(BF16)
    for bi in range(nb):
        tail = xp_ref[bi, c:c + pad, :]
        xp_ref[bi, 0:pad, :] = tail
        cstate_ref[bi] = tail


def _gdn(h2d, ab, conv_w, a_log, dt_bias, norm_w, conv_init, s0, *, batch, t, nb):
    c = min(CHUNK, t)
    nc = t // c
    assert batch % nb == 0
    h3d = h2d.reshape(batch, t, H_MAIN)
    ab3d = ab.reshape(batch, t, LANES)
    front = lambda b, n: (b, jnp.minimum(n, nc - 1), 0)
    back = lambda b, n: (b, jnp.maximum(n - 1, 0), 0)
    stash = lambda *shape: pltpu.VMEM((2, *shape), F32)
    o_a, cstate, s_new = pl.pallas_call(
        functools.partial(_gdn_kernel, c=c, nb=nb),
        grid=(batch // nb, nc + 1),
        in_specs=[
            pl.BlockSpec((nb, c, GDN_CONV_CH), front),
            pl.BlockSpec((nb, c, GDN_W), lambda b, n: (b, jnp.maximum(n - 1, 0), COL_Z // GDN_W)),
            pl.BlockSpec((nb, c, LANES), front),
            pl.BlockSpec((GDN_CONV, GDN_CONV_CH), lambda b, n: (0, 0)),
            pl.BlockSpec((1, LANES), lambda b, n: (0, 0)),
            pl.BlockSpec((1, LANES), lambda b, n: (0, 0)),
            pl.BlockSpec((1, GDN_DV), lambda b, n: (0, 0)),
            pl.BlockSpec((nb, SUBLANES, GDN_CONV_CH), lambda b, n: (b, 0, 0)),
            pl.BlockSpec((nb, GDN_HEADS, GDN_DK, GDN_DV), lambda b, n: (b, 0, 0, 0)),
        ],
        out_specs=[
            pl.BlockSpec((nb, c, GDN_W), back),
            pl.BlockSpec((nb, SUBLANES, GDN_CONV_CH), lambda b, n: (b, 0, 0)),
            pl.BlockSpec((nb, GDN_HEADS, GDN_DK, GDN_DV), lambda b, n: (b, 0, 0, 0)),
        ],
        out_shape=[
            jax.ShapeDtypeStruct((batch, t, GDN_W), BF16),
            jax.ShapeDtypeStruct((batch, SUBLANES, GDN_CONV_CH), F32),
            jax.ShapeDtypeStruct((batch, GDN_HEADS, GDN_DK, GDN_DV), F32),
        ],
        scratch_shapes=[
            pltpu.VMEM((nb, SUBLANES + c, GDN_CONV_CH), F32),
            stash(nb, c, GDN_W), stash(nb, c, GDN_W), stash(nb, c, GDN_W),
            stash(nb, c, LANES), stash(nb, c, LANES), stash(nb, LANES, c),
            stash(nb * GDN_HEADS, c, c),
        ],
        compiler_params=_cparams(("arbitrary", "arbitrary")),
        name="gdn",
    )(h3d, h3d, ab3d, conv_w, a_log, dt_bias, norm_w, conv_init, s0)
    return o_a.reshape(batch * t, GDN_W), cstate, s_new


def _attn_kernel(q_ref, k_ref, v_ref, bias_ref, o_ref, *, tq, width, pad, koff):
    n = pl.program_id(1)
    start = pl.multiple_of(koff + n * tq, tq)
    kwin = k_ref[0, pl.ds(start, width), :]
    vwin = v_ref[0, pl.ds(start, width), :]
    kpos = n * tq - pad + lax.broadcasted_iota(jnp.int32, (tq, width), 1)
    valid = kpos >= 0
    scale = ATT_DH ** -0.5
    heads = range(ATT_HEADS)
    cols = [slice(h * ATT_DH, (h + 1) * ATT_DH) for h in heads]
    s = [_dot_nt(q_ref[:, cols[h]], kwin[:, cols[h]]) for h in heads]
    s = [jnp.where(valid, s[h] * scale + bias_ref[h], -jnp.inf) for h in heads]
    p = [jnp.exp(x - jnp.max(x, axis=-1, keepdims=True)) for x in s]
    l = [jnp.sum(x, axis=-1, keepdims=True) for x in p]
    pv = [_dot(p[h], vwin[:, cols[h]]) for h in heads]
    o_ref[...] = jnp.concatenate([pv[h] / l[h] for h in heads], axis=1).astype(BF16)


def _attn(q2d, k3d, v3d, bias, *, batch, t, tq, width, pad, koff):
    nq = t // tq
    tk = k3d.shape[1]
    assert koff % tq == 0 and koff + (nq - 1) * tq + width <= tk
    return pl.pallas_call(
        functools.partial(_attn_kernel, tq=tq, width=width, pad=pad, koff=koff),
        grid=(batch, nq),
        in_specs=[
            pl.BlockSpec((tq, ATT_W), lambda b, n: (b * nq + n, 0)),
            pl.BlockSpec((1, tk, ATT_W), lambda b, n: (b, 0, 0)),
            pl.BlockSpec((1, tk, ATT_W), lambda b, n: (b, 0, 0)),
            pl.BlockSpec((ATT_HEADS, tq, width), lambda b, n: (0, 0, 0)),
        ],
        out_specs=pl.BlockSpec((tq, ATT_W), lambda b, n: (b * nq + n, 0)),
        out_shape=jax.ShapeDtypeStruct((batch * t, ATT_W), BF16),
        compiler_params=_cparams(("arbitrary", "arbitrary")),
        name="attn",
    )(q2d, k3d, v3d, bias)


def _merge_kernel(x_ref, g_ref, b_ref, oa_ref, ob_ref, ga_ref, gb_ref, wa_ref, wb_ref, wo_ref,
                  g1_ref, b1_ref, x1_ref, *, parts):
    sub = x_ref.shape[0] // parts
    for s in range(parts):
        r = pl.ds(s * sub, sub)
        ya = jnp.dot(oa_ref[r, :], wa_ref[...], preferred_element_type=F32)
        yb = jnp.dot(ob_ref[r, :], wb_ref[...], preferred_element_type=F32)
        mixed = (_sigmoid(ga_ref[r, :].astype(F32)) * ya + _sigmoid(gb_ref[r, :].astype(F32)) * yb)
        xn = _layernorm_rows(x_ref[r, :], g_ref[...], b_ref[...])
        y = DEEPNORM_ALPHA * xn + _dot(mixed, wo_ref[...])
        x1_ref[r, :] = _layernorm_rows(y, g1_ref[...], b1_ref[...])


def _merge(x2d, ln_g, ln_b, oa, ob, h2d, wa, wb, wo, ln1_g, ln1_b, *, tm):
    rows = x2d.shape[0]
    full = lambda i: (0, 0)
    return pl.pallas_call(
        functools.partial(_merge_kernel, parts=2 if tm >= 512 else 1),
        grid=(rows // tm,),
        in_specs=[
            pl.BlockSpec((tm, D_MODEL), lambda i: (i, 0)),
            pl.BlockSpec((1, D_MODEL), full),
            pl.BlockSpec((1, D_MODEL), full),
            pl.BlockSpec((tm, GDN_W), lambda i: (i, 0)),
            pl.BlockSpec((tm, ATT_W), lambda i: (i, 0)),
            pl.BlockSpec((tm, D_MODEL), lambda i: (i, COL_GA // D_MODEL)),
            pl.BlockSpec((tm, D_MODEL), lambda i: (i, COL_GB // D_MODEL)),
            pl.BlockSpec((GDN_W, D_MODEL), full),
            pl.BlockSpec((ATT_W, D_MODEL), full),
            pl.BlockSpec((D_MODEL, D_MODEL), full),
            pl.BlockSpec((1, D_MODEL), full),
            pl.BlockSpec((1, D_MODEL), full),
        ],
        out_specs=pl.BlockSpec((tm, D_MODEL), lambda i: (i, 0)),
        out_shape=jax.ShapeDtypeStruct((rows, D_MODEL), F32),
        compiler_params=_cparams(("arbitrary",)),
        name="merge",
    )(x2d, ln_g, ln_b, oa, ob, h2d, h2d, wa, wb, wo, ln1_g, ln1_b)


def _gelu_tanh(x):
    return 0.5 * x * (1.0 + jnp.tanh(math.sqrt(2.0 / math.pi) * (x + 0.044715 * (x * x * x))))


def _ffn_kernel(x1_ref, pe_ref, wup_ref, cw_ref, wdn_ref, wple_ref, wpg_ref, g2_ref, b2_ref,
                finit_ref, y_ref, fstate_ref, gp_ref, acc_ref, *, tm, fc):
    n = pl.program_id(1)
    pad = SUBLANES

    @pl.when(n == 0)
    def _():
        gp_ref[0:pad, :] = finit_ref[0]

    x1 = x1_ref[...]
    x1b = x1.astype(BF16)

    def up(k):
        c0 = k * fc
        gate = jnp.dot(x1b, wup_ref[:, c0:c0 + fc], preferred_element_type=F32)
        val = jnp.dot(x1b, wup_ref[:, D_FF + c0:D_FF + c0 + fc], preferred_element_type=F32)
        gp_ref[pad:pad + tm, c0:c0 + fc] = gate
        return gate, val

    nk = D_FF // fc
    nxt = up(0)
    ple = jnp.dot(pe_ref[...].astype(BF16), wple_ref[...], preferred_element_type=F32)
    for k in range(nk):
        c0 = k * fc
        gate, val = nxt
        if k + 1 < nk:
            nxt = up(k + 1)
        conv = gate * cw_ref[FFN_CONV - 1:FFN_CONV, c0:c0 + fc]
        for i in range(1, FFN_CONV):
            conv = conv + (gp_ref[pl.ds(pad - i, tm), c0:c0 + fc]
                           * cw_ref[FFN_CONV - 1 - i:FFN_CONV - i, c0:c0 + fc])
        act = (_gelu_tanh(conv) * val).astype(BF16)
        down = jnp.dot(act, wdn_ref[c0:c0 + fc, :], preferred_element_type=F32)
        if k == 0:
            acc_ref[...] = down
        else:
            acc_ref[...] += down
    tail = gp_ref[tm:tm + pad, :]
    gp_ref[0:pad, :] = tail
    fstate_ref[0] = tail

    r = DEEPNORM_ALPHA * x1 + acc_ref[...]
    gate_p = _sigmoid(_dot(r, wpg_ref[...]))
    r = r + gate_p * ple
    y_ref[...] = _layernorm_rows(r, g2_ref[...], b2_ref[...])


def _ffn(x1, pe, wup, conv_w, wdn, wple, wpg, ln2_g, ln2_b, finit, *, batch, t, tm, fc):
    nt = t // tm
    rows = batch * t
    full = lambda b, n: (0, 0)
    return pl.pallas_call(
        functools.partial(_ffn_kernel, tm=tm, fc=fc),
        grid=(batch, nt),
        in_specs=[
            pl.BlockSpec((tm, D_MODEL), lambda b, n: (b * nt + n, 0)),
            pl.BlockSpec((tm, PLE_DIM), lambda b, n: (b * nt + n, 0)),
            pl.BlockSpec((D_MODEL, 2 * D_FF), full, pipeline_mode=pl.Buffered(1)),
            pl.BlockSpec((FFN_CONV, D_FF), full),
            pl.BlockSpec((D_FF, D_MODEL), full, pipeline_mode=pl.Buffered(1)),
            pl.BlockSpec((PLE_DIM, D_MODEL), full),
            pl.BlockSpec((D_MODEL, D_MODEL), full),
            pl.BlockSpec((1, D_MODEL), full),
            pl.BlockSpec((1, D_MODEL), full),
            pl.BlockSpec((1, SUBLANES, D_FF), lambda b, n: (b, 0, 0)),
        ],
        out_specs=[
            pl.BlockSpec((tm, D_MODEL), lambda b, n: (b * nt + n, 0)),
            pl.BlockSpec((1, SUBLANES, D_FF), lambda b, n: (b, 0, 0)),
        ],
        out_shape=[
            jax.ShapeDtypeStruct((rows, D_MODEL), F32),
            jax.ShapeDtypeStruct((batch, SUBLANES, D_FF), F32),
        ],
        scratch_shapes=[pltpu.VMEM((SUBLANES + tm, D_FF), F32), pltpu.VMEM((tm, D_MODEL), F32)],
        compiler_params=_cparams(("arbitrary", "arbitrary")),
        name="ffn",
    )(x1, pe, wup, conv_w, wdn, wple, wpg, ln2_g, ln2_b, finit)


def _rel_bias(table, n_q, n_past, n_k):
    span = n_q + n_k - 1
    rel = n_q - 1 + n_past - jnp.arange(span)
    rev = table[:, jnp.clip(rel, -REL_CLIP, REL_CLIP) + REL_CLIP].astype(F32)
    heads = table.shape[0]
    period = jnp.pad(rev, ((0, 0), (0, 1)))
    skew = jnp.tile(period, (1, n_q))[:, :n_q * span].reshape(heads, n_q, span)
    return skew[:, :, n_q - 1:n_q - 1 + n_k]


def _pad_state(state, rows):
    b, r, ch = state.shape
    return jnp.concatenate([jnp.zeros((b, rows - r, ch), F32), state.astype(F32)], axis=1)


def _layer(x, pe, gconv_prev, s_gdn, k_prev, v_prev, fconv_prev, wts, *, tm_proj, tm_merge, tm_ffn, gdn_nb,
           att_chunks=1):
    (ln_in_g, ln_in_b, w_proj, gdn_conv_w, a_log, dt_bias, norm_w, rel_table, w_a, w_b, w_o,
     ln1_g, ln1_b, w_up, ffn_conv_w, w_dn, w_ple, w_pg, ln2_g, ln2_b) = wts
    b, t, _ = x.shape
    rows = b * t
    x2d = x.reshape(rows, D_MODEL)

    prompt = k_prev is None
    tm_proj = min(tm_proj, t if prompt else rows)
    pad_rows = tm_proj if prompt else 0
    assert pad_rows >= BAND or not prompt
    h2d, ab, q_b, k_b, v_b = _in_proj(x2d, ln_in_g, ln_in_b, *w_proj, tm=tm_proj,
                                      t=t if prompt else rows, pad_rows=pad_rows)

    o_a, gconv_new, s_new = _gdn(h2d, ab, gdn_conv_w, a_log, dt_bias, norm_w,
                                 _pad_state(gconv_prev, SUBLANES), s_gdn.astype(F32), batch=b, t=t,
                                 nb=gdn_nb)

    k_b = k_b.reshape(b, -1, ATT_W)
    v_b = v_b.reshape(b, -1, ATT_W)
    if prompt:
        tq = att_chunks * CHUNK
        width = BAND + tq
        own = (jnp.arange(tq) // CHUNK)[:, None] * CHUNK
        key = jnp.arange(width)[None, :]
        in_band = (key >= own) & (key < own + BAND + CHUNK)
        bias = jnp.where(in_band, _rel_bias(rel_table, tq, BAND, width), -jnp.inf)
        o_b = _attn(q_b, k_b, v_b, bias, batch=b, t=t, tq=tq, width=width, pad=BAND,
                    koff=pad_rows - BAND)
        keep = min(BAND, t)
        k_new, v_new = k_b[:, pad_rows + t - keep:], v_b[:, pad_rows + t - keep:]
    else:
        n_past = k_prev.shape[1]
        kk = jnp.concatenate([k_prev.reshape(b, n_past, ATT_W).astype(BF16), k_b], axis=1)
        vv = jnp.concatenate([v_prev.reshape(b, n_past, ATT_W).astype(BF16), v_b], axis=1)
        bias = _rel_bias(rel_table, t, n_past, n_past + t)
        o_b = _attn(q_b, kk, vv, bias, batch=b, t=t, tq=t, width=n_past + t, pad=0, koff=0)
        k_new, v_new = k_b, v_b

    x1 = _merge(x2d, ln_in_g, ln_in_b, o_a, o_b, h2d, w_a, w_b, w_o, ln1_g, ln1_b,
                tm=min(tm_merge, rows))

    y, fconv_new = _ffn(x1, pe.reshape(rows, PLE_DIM), w_up, ffn_conv_w, w_dn, w_ple, w_pg,
                        ln2_g, ln2_b, _pad_state(fconv_prev, SUBLANES), batch=b, t=t,
                        tm=min(tm_ffn, t), fc=256)

    keep_k = k_new.shape[1]
    return (y.reshape(b, t, D_MODEL),
            gconv_new[:, SUBLANES - (GDN_CONV - 1):],
            s_new,
            k_new.astype(F32).reshape(b, keep_k, ATT_HEADS, ATT_DH),
            v_new.astype(F32).reshape(b, keep_k, ATT_HEADS, ATT_DH),
            fconv_new[:, SUBLANES - (FFN_CONV - 1):])


def _pad_lanes(v, width=LANES):
    return jnp.pad(v.astype(F32), (0, width - v.shape[0])).reshape(1, width)


def kernel(x_prompt, x_sample, state_gdn_conv, state_gdn, cache_attn_k, cache_attn_v, state_ffn_conv,
           p_prompt, p_sample, ln_in_g, ln_in_b, w_in, gdn_conv_w, gdn_a_log, gdn_dt_bias, gdn_norm_w,
           att_rel_bias, w_branch_a, w_branch_b, w_out, ln1_g, ln1_b, w_ffn_up, ffn_conv_w, w_ffn_down,
           w_ple, w_ple_gate, ln2_g, ln2_b):
    depth = w_in.shape[0]
    assert depth == 1
    bp = x_prompt.shape[0]
    row = lambda v: v.reshape(1, -1).astype(F32)

    xp, xs = x_prompt, x_sample
    new_p, new_s = [], []
    for i in range(depth):
        wi = w_in[i]
        o_ab = 4 * GDN_W
        o_att = o_ab + 2 * GDN_HEADS
        o_gate = o_att + 3 * ATT_W
        wi = wi.astype(BF16)
        w_proj = (wi[:, :o_ab], wi[:, o_att:o_gate], wi[:, o_gate:],
                  jnp.pad(wi[:, o_ab:o_att], ((0, 0), (0, LANES - 2 * GDN_HEADS))))
        wts = (row(ln_in_g), row(ln_in_b), w_proj, gdn_conv_w[i].astype(F32),
               _pad_lanes(gdn_a_log[i]), _pad_lanes(gdn_dt_bias[i]), row(gdn_norm_w[i]), att_rel_bias[i],
               w_branch_a[i].astype(BF16), w_branch_b[i].astype(BF16), w_out[i].astype(BF16),
               row(ln1_g[i]), row(ln1_b[i]), w_ffn_up[i].astype(BF16), ffn_conv_w[i].astype(F32),
               w_ffn_down[i].astype(BF16), w_ple[i].astype(BF16), w_ple_gate[i].astype(BF16),
               row(ln2_g[i]), row(ln2_b[i]))
        xp, st_p = _unpack(_layer(xp, p_prompt[i],
                                  jnp.zeros((bp, GDN_CONV - 1, GDN_CONV_CH), F32),
                                  jnp.zeros((bp, GDN_HEADS, GDN_DK, GDN_DV), F32),
                                  None, None,
                                  jnp.zeros((bp, FFN_CONV - 1, D_FF), F32), wts,
                                  tm_proj=1024, tm_merge=1024, tm_ffn=512, gdn_nb=2, att_chunks=4))
        xs, st_s = _unpack(_layer(xs, p_sample[i], state_gdn_conv[i], state_gdn[i], cache_attn_k[i],
                                  cache_attn_v[i], state_ffn_conv[i], wts,
                                  tm_proj=256, tm_merge=256, tm_ffn=256, gdn_nb=4))
        new_p.append(st_p)
        new_s.append(st_s)
    p_gconv, p_gdn, p_k, p_v, p_fconv = [jnp.stack(a) for a in zip(*new_p)]
    s_gconv, s_gdn, s_k, s_v, s_fconv = [jnp.stack(a) for a in zip(*new_s)]
    return (xp, xs, p_gconv, p_gdn, p_k, p_v, p_fconv, s_gconv, s_gdn, s_k, s_v, s_fconv)


def _unpack(res):
    return res[0], res[1:]
```

```python
import functools
import math

import jax
import jax.numpy as jnp
from jax import lax
from jax.experimental import pallas as pl
from jax.experimental.pallas import tpu as pltpu

F32 = jnp.float32
BF16 = jnp.bfloat16

D_MODEL = 1024
CHUNK = 64
BAND = 512
GDN_HEADS = 8
GDN_DK = 128
GDN_DV = 128
GDN_CONV = 4
ATT_HEADS = 8
ATT_DH = 64
REL_CLIP = 128
D_FF = 2816
FFN_CONV = 3
PLE_DIM = 256
DEEPNORM_ALPHA = 2.0 ** 0.25
LN_EPS = 1e-5
RMS_EPS = 1e-6
L2_EPS = 1e-6

GDN_W = GDN_HEADS * GDN_DK
GDN_CONV_CH = 3 * GDN_W
ATT_W = ATT_HEADS * ATT_DH
LANES = 128
SUBLANES = 8

COL_QKV = 0
COL_Z = 3 * GDN_W
COL_GA = 4 * GDN_W
COL_GB = 5 * GDN_W
H_MAIN = 6 * GDN_W

VMEM_LIMIT = 56 * 1024 * 1024


def _cparams(sem):
    return pltpu.CompilerParams(dimension_semantics=sem, vmem_limit_bytes=VMEM_LIMIT)


def _sigmoid(x):
    return 1.0 / (1.0 + jnp.exp(-x))


def _silu(x):
    return x * _sigmoid(x)


def _layernorm_rows(x, g, b):
    mu = jnp.mean(x, axis=-1, keepdims=True)
    xc = x - mu
    var = jnp.mean(xc * xc, axis=-1, keepdims=True)
    return xc * lax.rsqrt(var + LN_EPS) * g + b


def _dot(a, b):
    return jnp.dot(a.astype(BF16), b.astype(BF16), preferred_element_type=F32)


def _dot_nt(a, b):
    return lax.dot_general(a.astype(BF16), b.astype(BF16), (((1,), (1,)), ((), ())),
                           preferred_element_type=F32)


def _dot_tn(a, b):
    return lax.dot_general(a.astype(BF16), b.astype(BF16), (((0,), (0,)), ((), ())),
                           preferred_element_type=F32)


def _split3(x):
    x1 = x.astype(BF16)
    r1 = x - x1.astype(F32)
    x2 = r1.astype(BF16)
    x3 = (r1 - x2.astype(F32)).astype(BF16)
    return x1, x2, x3


def _in_proj_kernel(x_ref, g_ref, b_ref, wa_ref, watt_ref, wg_ref, wab_ref, kz_ref, vz_ref,
                    h_ref, ab_ref, q_ref, k_ref, v_ref, xn_ref, *, sub, n_a):
    del kz_ref, vz_ref
    j = pl.program_id(1)
    tm = x_ref.shape[0]

    @pl.when(j == 0)
    def _():
        def body(i, c):
            r = pl.multiple_of(i * sub, sub)
            xn = _layernorm_rows(x_ref[pl.ds(r, sub), :], g_ref[...], b_ref[...])
            xn_ref[pl.ds(r, sub), :] = xn.astype(BF16)
            return c
        lax.fori_loop(0, tm // sub, body, 0)
        ab_ref[...] = _dot_nt(xn_ref[...], wab_ref[...])

    @pl.when(j < n_a)
    def _():
        h_ref[...] = _dot_nt(xn_ref[...], wa_ref[...]).astype(BF16)

    @pl.when(j == n_a)
    def _():
        for o_ref, c0 in ((q_ref, 0), (k_ref, ATT_W), (v_ref, 2 * ATT_W)):
            o_ref[...] = _dot_nt(xn_ref[...], watt_ref[c0:c0 + ATT_W, :]).astype(BF16)

    @pl.when(j == n_a + 1)
    def _():
        h_ref[...] = _dot_nt(xn_ref[...], wg_ref[...]).astype(BF16)


def _in_proj(x2d, ln_g, ln_b, w_a, w_att, w_g, w_ab, *, tm, t, pad_rows):
    rows = x2d.shape[0]
    tn = w_g.shape[0]
    n_a = w_a.shape[0] // tn
    assert rows % tm == 0 and t % tm == 0 and pad_rows % tm == 0
    assert w_a.shape[0] == n_a * tn and w_a.shape[0] + tn == H_MAIN and w_att.shape[0] == 3 * ATT_W
    sub = min(tm, 128)
    per_stream = t // tm
    stride = per_stream + pad_rows // tm
    kv_rows = (rows // t) * (t + pad_rows)
    kv_block = lambda i, j: ((i // per_stream) * stride + pad_rows // tm + i % per_stream, 0)
    zeros = jnp.zeros((kv_rows, ATT_W), BF16)
    const = lambda i, j: (0, 0)
    return pl.pallas_call(
        functools.partial(_in_proj_kernel, sub=sub, n_a=n_a),
        grid=(rows // tm, n_a + 2),
        in_specs=[
            pl.BlockSpec((tm, D_MODEL), lambda i, j: (i, 0)),
            pl.BlockSpec((1, D_MODEL), const),
            pl.BlockSpec((1, D_MODEL), const),
            pl.BlockSpec((tn, D_MODEL), lambda i, j: (jnp.minimum(j, n_a - 1), 0)),
            pl.BlockSpec((3 * ATT_W, D_MODEL), const, pipeline_mode=pl.Buffered(1)),
            pl.BlockSpec((tn, D_MODEL), const, pipeline_mode=pl.Buffered(1)),
            pl.BlockSpec((LANES, D_MODEL), const),
            pl.BlockSpec(memory_space=pl.ANY),
            pl.BlockSpec(memory_space=pl.ANY),
        ],
        out_specs=[
            pl.BlockSpec((tm, tn), lambda i, j: (i, jnp.where(j < n_a, j, jnp.maximum(j - 1, n_a - 1)))),
            pl.BlockSpec((tm, LANES), lambda i, j: (i, 0)),
            pl.BlockSpec((tm, ATT_W), lambda i, j: (i, 0)),
            pl.BlockSpec((tm, ATT_W), kv_block),
            pl.BlockSpec((tm, ATT_W), kv_block),
        ],
        out_shape=[
            jax.ShapeDtypeStruct((rows, H_MAIN), BF16),
            jax.ShapeDtypeStruct((rows, LANES), F32),
            jax.ShapeDtypeStruct((rows, ATT_W), BF16),
            jax.ShapeDtypeStruct((kv_rows, ATT_W), BF16),
            jax.ShapeDtypeStruct((kv_rows, ATT_W), BF16),
        ],
        input_output_aliases={7: 3, 8: 4},
        scratch_shapes=[pltpu.VMEM((tm, D_MODEL), BF16)],
        compiler_params=_cparams(("arbitrary", "arbitrary")),
        name="in_proj",
    )(x2d, ln_g, ln_b, w_a, w_att, w_g, w_ab, zeros, zeros)


def _gdn_kernel(qkv_ref, z_ref, ab_ref, cw_ref, alog_ref, dtb_ref, nw_ref, cinit_ref, s0_ref,
                o_ref, cstate_ref, s_ref,
                xp_ref, q_st, k_st, v_st, dec_st, beta_st, dect_st, lm_st, *, c, nb):
    n = pl.program_id(1)
    pad = SUBLANES
    cur = lax.rem(n, 2)
    prv = 1 - cur
    heads = range(GDN_HEADS)
    pairs = [(bi, h) for bi in range(nb) for h in heads]

    @pl.when(n == 0)
    def _():
        xp_ref[:, 0:pad, :] = cinit_ref[...]
        s_ref[...] = s0_ref[...]
        for st in (q_st, k_st, v_st, dec_st, beta_st, dect_st, lm_st):
            st[1] = jnp.zeros(st.shape[1:], F32)

    row = lax.broadcasted_iota(jnp.int32, (c, c), 0)
    col = lax.broadcasted_iota(jnp.int32, (c, c), 1)
    causal = row >= col
    strict = row > col
    eye = jnp.where(row == col, 1.0, 0.0).astype(F32)
    tri = jnp.where(causal, 1.0, 0.0).astype(BF16)

    for bi in range(nb):
        ab = ab_ref[bi]
        sp_in = ab + dtb_ref[...]
        softplus = jnp.maximum(sp_in, 0.0) + jnp.log(1.0 + jnp.exp(-jnp.abs(sp_in)))
        g_all = -jnp.exp(alog_ref[...]) * softplus
        g1, g2, g3 = _split3(g_all)
        dec_all = (jnp.dot(tri, g1, preferred_element_type=F32)
                   + (jnp.dot(tri, g2, preferred_element_type=F32)
                      + jnp.dot(tri, g3, preferred_element_type=F32)))
        dec_t = dec_all.T
        dec_st[cur, bi] = dec_all
        dect_st[cur, bi] = dec_t
        beta_st[cur, bi] = _sigmoid(ab)
        for h in heads:
            lm_st[cur, bi * GDN_HEADS + h] = jnp.exp(
                jnp.where(causal, dec_all[:, h:h + 1] - dec_t[h:h + 1, :], -jnp.inf))

        xp_ref[bi, pad:pad + c, :] = qkv_ref[bi].astype(F32)

    def conv_silu(bi, c0, width):
        acc = xp_ref[bi, pl.ds(pad, c), pl.ds(c0, width)] * cw_ref[GDN_CONV - 1:GDN_CONV, pl.ds(c0, width)]
        for i in range(1, GDN_CONV):
            acc = acc + (xp_ref[bi, pl.ds(pad - i, c), pl.ds(c0, width)]
                         * cw_ref[GDN_CONV - 1 - i:GDN_CONV - i, pl.ds(c0, width)])
        return _silu(acc)

    def q_job(bi, h):
        lo = h * GDN_DK
        xq = conv_silu(bi, lo, GDN_DK)
        q_st[cur, bi, :, lo:lo + GDN_DK] = (
            xq * lax.rsqrt(jnp.sum(xq * xq, axis=-1, keepdims=True) + L2_EPS) * (GDN_DK ** -0.5))

    def k_job(bi, h):
        lo = h * GDN_DK
        xk = conv_silu(bi, GDN_W + lo, GDN_DK)
        k_st[cur, bi, :, lo:lo + GDN_DK] = (
            xk * lax.rsqrt(jnp.sum(xk * xk, axis=-1, keepdims=True) + L2_EPS))

    def v_job(bi, h):
        lo = h * GDN_DV
        v_st[cur, bi, :, lo:lo + GDN_DV] = conv_silu(bi, 2 * GDN_W + lo, GDN_DV)

    jobs = [functools.partial(job, bi, h) for bi, h in pairs for job in (q_job, k_job, v_job)]
    n_stage = 2 * (int(math.log2(c)) - 1) + 4
    per_stage = -(-len(jobs) // n_stage)

    def front_jobs():
        for job in jobs[:per_stage]:
            job()
        del jobs[:per_stage]

    def cols(ref, bi, h, width=GDN_DK):
        return ref[prv, bi, :, h * width:(h + 1) * width]

    qn = [cols(q_st, bi, h) for bi, h in pairs]
    kn = [cols(k_st, bi, h) for bi, h in pairs]
    xv = [cols(v_st, bi, h, GDN_DV) for bi, h in pairs]
    dcol = [dec_st[prv, bi, :, h:h + 1] for bi, h in pairs]
    beta = [beta_st[prv, bi, :, GDN_HEADS + h:GDN_HEADS + h + 1] for bi, h in pairs]
    lmask = [lm_st[prv, bi * GDN_HEADS + h] for bi, h in pairs]
    idx = range(len(pairs))
    kb = [kn[i] * beta[i] for i in idx]
    e_d = [jnp.exp(d) for d in dcol]
    gram = [_dot_nt(jnp.concatenate([kb[i], qn[i]], axis=0), kn[i]) for i in idx]
    front_jobs()
    m = [jnp.where(strict, gram[i][:c] * lmask[i], 0.0) for i in idx]
    qk = [gram[i][c:] * lmask[i] for i in idx]
    p = [eye - m[i] for i in idx]
    mk = m
    for _ in range(int(math.log2(c)) - 1):
        mk = [_dot(x, x) for x in mk]
        front_jobs()
        p = [p[i] + _dot(p[i], mk[i]) for i in idx]
        front_jobs()
    sol = [_dot(p[i], jnp.concatenate([xv[i] * beta[i], kb[i] * e_d[i]], axis=1)) for i in idx]
    front_jobs()
    s_old = [s_ref[bi, h] for bi, h in pairs]
    ws_qs = [_dot(jnp.concatenate([sol[i][:, GDN_DV:], qn[i] * e_d[i]], axis=0), s_old[i])
             for i in idx]
    front_jobs()
    v_new = [sol[i][:, :GDN_DV] - ws_qs[i][:c] for i in idx]
    dl = [d[c - 1:c, :] for d in dcol]
    kd = [kn[i] * jnp.exp(dl[i] - dcol[i]) for i in idx]
    s_upd = [_dot_tn(kd[i], v_new[i]) for i in idx]
    o_in = [_dot(qk[i], v_new[i]) for i in idx]
    front_jobs()
    assert not jobs
    for i, (bi, h) in enumerate(pairs):
        s_ref[bi, h] = s_old[i] * jnp.exp(dl[i]) + s_upd[i]
        o = ws_qs[i][c:] + o_in[i]
        zh = z_ref[bi, :, pl.ds(h * GDN_DV, GDN_DV)].astype(F32)
        o = (o * lax.rsqrt(jnp.mean(o * o, axis=-1, keepdims=True) + RMS_EPS) * nw_ref[...]
             * _silu(zh))
        o_ref[bi, :, pl.ds(h * GDN_DV, GDN_DV)] = o.astype(BF16)
    for bi in range(nb):
        tail = xp_ref[bi, c:c + pad, :]
        xp_ref[bi, 0:pad, :] = tail
        cstate_ref[bi] = tail


def _gdn(h2d, ab, conv_w, a_log, dt_bias, norm_w, conv_init, s0, *, batch, t, nb):
    c = min(CHUNK, t)
    nc = t // c
    assert batch % nb == 0
    h3d = h2d.reshape(batch, t, H_MAIN)
    ab3d = ab.reshape(batch, t, LANES)
    front = lambda b, n: (b, jnp.minimum(n, nc - 1), 0)
    back = lambda b, n: (b, jnp.maximum(n - 1, 0), 0)
    stash = lambda *shape: pltpu.VMEM((2, *shape), F32)
    o_a, cstate, s_new = pl.pallas_call(
        functools.partial(_gdn_kernel, c=c, nb=nb),
        grid=(batch // nb, nc + 1),
        in_specs=[
            pl.BlockSpec((nb, c, GDN_CONV_CH), front),
            pl.BlockSpec((nb, c, GDN_W), lambda b, n: (b, jnp.maximum(n - 1, 0), COL_Z // GDN_W)),
            pl.BlockSpec((nb, c, LANES), front),
            pl.BlockSpec((GDN_CONV, GDN_CONV_CH), lambda b, n: (0, 0)),
            pl.BlockSpec((1, LANES), lambda b, n: (0, 0)),
            pl.BlockSpec((1, LANES), lambda b, n: (0, 0)),
            pl.BlockSpec((1, GDN_DV), lambda b, n: (0, 0)),
            pl.BlockSpec((nb, SUBLANES, GDN_CONV_CH), lambda b, n: (b, 0, 0)),
            pl.BlockSpec((nb, GDN_HEADS, GDN_DK, GDN_DV), lambda b, n: (b, 0, 0, 0)),
        ],
        out_specs=[
            pl.BlockSpec((nb, c, GDN_W), back),
            pl.BlockSpec((nb, SUBLANES, GDN_CONV_CH), lambda b, n: (b, 0, 0)),
            pl.BlockSpec((nb, GDN_HEADS, GDN_DK, GDN_DV), lambda b, n: (b, 0, 0, 0)),
        ],
        out_shape=[
            jax.ShapeDtypeStruct((batch, t, GDN_W), BF16),
            jax.ShapeDtypeStruct((batch, SUBLANES, GDN_CONV_CH), F32),
            jax.ShapeDtypeStruct((batch, GDN_HEADS, GDN_DK, GDN_DV), F32),
        ],
        scratch_shapes=[
            pltpu.VMEM((nb, SUBLANES + c, GDN_CONV_CH), F32),
            stash(nb, c, GDN_W), stash(nb, c, GDN_W), stash(nb, c, GDN_W),
            stash(nb, c, LANES), stash(nb, c, LANES), stash(nb, LANES, c),
            stash(nb * GDN_HEADS, c, c),
        ],
        compiler_params=_cparams(("arbitrary", "arbitrary")),
        name="gdn",
    )(h3d, h3d, ab3d, conv_w, a_log, dt_bias, norm_w, conv_init, s0)
    return o_a.reshape(batch * t, GDN_W), cstate, s_new


def _attn_kernel(q_ref, k_ref, v_ref, bias_ref, o_ref, *, tq, width, pad, koff):
    n = pl.program_id(1)
    start = pl.multiple_of(koff + n * tq, tq)
    kwin = k_ref[0, pl.ds(start, width), :]
    vwin = v_ref[0, pl.ds(start, width), :]
    kpos = n * tq - pad + lax.broadcasted_iota(jnp.int32, (tq, width), 1)
    valid = kpos >= 0
    scale = ATT_DH ** -0.5
    heads = range(ATT_HEADS)
    cols = [slice(h * ATT_DH, (h + 1) * ATT_DH) for h in heads]
    s = [_dot_nt(q_ref[:, cols[h]], kwin[:, cols[h]]) for h in heads]
    s = [jnp.where(valid, s[h] * scale + bias_ref[h], -jnp.inf) for h in heads]
    p = [jnp.exp(x - jnp.max(x, axis=-1, keepdims=True)) for x in s]
    l = [jnp.sum(x, axis=-1, keepdims=True) for x in p]
    pv = [_dot(p[h], vwin[:, cols[h]]) for h in heads]
    o_ref[...] = jnp.concatenate([pv[h] / l[h] for h in heads], axis=1).astype(BF16)


def _attn(q2d, k3d, v3d, bias, *, batch, t, tq, width, pad, koff):
    nq = t // tq
    tk = k3d.shape[1]
    assert koff % tq == 0 and koff + (nq - 1) * tq + width <= tk
    return pl.pallas_call(
        functools.partial(_attn_kernel, tq=tq, width=width, pad=pad, koff=koff),
        grid=(batch, nq),
        in_specs=[
            pl.BlockSpec((tq, ATT_W), lambda b, n: (b * nq + n, 0)),
            pl.BlockSpec((1, tk, ATT_W), lambda b, n: (b, 0, 0)),
            pl.BlockSpec((1, tk, ATT_W), lambda b, n: (b, 0, 0)),
            pl.BlockSpec((ATT_HEADS, tq, width), lambda b, n: (0, 0, 0)),
        ],
        out_specs=pl.BlockSpec((tq, ATT_W), lambda b, n: (b * nq + n, 0)),
        out_shape=jax.ShapeDtypeStruct((batch * t, ATT_W), BF16),
        compiler_params=_cparams(("arbitrary", "arbitrary")),
        name="attn",
    )(q2d, k3d, v3d, bias)


def _attn_step_kernel(q_ref, kn_ref, vn_ref, kt_ref, vt_ref, bc_ref, bn_ref, o_ref):
    scale = ATT_DH ** -0.5
    heads = range(ATT_HEADS)
    cols = [slice(h * ATT_DH, (h + 1) * ATT_DH) for h in heads]
    q = [q_ref[:, cols[h]] for h in heads]
    s_c = [_dot(q[h], kt_ref[0, h]) * scale + bc_ref[h] for h in heads]
    s_n = [_dot_nt(q[h], kn_ref[:, cols[h]]) * scale + bn_ref[h] for h in heads]
    mx = [jnp.maximum(jnp.max(s_c[h], axis=-1, keepdims=True), jnp.max(s_n[h], axis=-1, keepdims=True))
          for h in heads]
    p_c = [jnp.exp(s_c[h] - mx[h]) for h in heads]
    p_n = [jnp.exp(s_n[h] - mx[h]) for h in heads]
    l = [jnp.sum(p_c[h], axis=-1, keepdims=True) + jnp.sum(p_n[h], axis=-1, keepdims=True) for h in heads]
    pv = [_dot_nt(p_c[h], vt_ref[0, h]) + _dot(p_n[h], vn_ref[:, cols[h]]) for h in heads]
    o_ref[...] = jnp.concatenate([pv[h] / l[h] for h in heads], axis=1).astype(BF16)


def _attn_step(q2d, kn2d, vn2d, kt, vt, bias, *, batch, t):
    n_past = kt.shape[-1]
    rows = lambda b: (b, 0)
    cache = lambda b: (b, 0, 0, 0)
    const = lambda b: (0, 0, 0)
    return pl.pallas_call(
        _attn_step_kernel,
        grid=(batch,),
        in_specs=[
            pl.BlockSpec((t, ATT_W), rows),
            pl.BlockSpec((t, ATT_W), rows),
            pl.BlockSpec((t, ATT_W), rows),
            pl.BlockSpec((1, ATT_HEADS, ATT_DH, n_past), cache),
            pl.BlockSpec((1, ATT_HEADS, ATT_DH, n_past), cache),
            pl.BlockSpec((ATT_HEADS, t, n_past), const),
            pl.BlockSpec((ATT_HEADS, t, t), const),
        ],
        out_specs=pl.BlockSpec((t, ATT_W), rows),
        out_shape=jax.ShapeDtypeStruct((batch * t, ATT_W), BF16),
        compiler_params=_cparams(("arbitrary",)),
        name="attn_step",
    )(q2d, kn2d, vn2d, kt, vt, bias[:, :, :n_past], bias[:, :, n_past:])


def _merge_kernel(x_ref, g_ref, b_ref, oa_ref, ob_ref, ga_ref, gb_ref, wa_ref, wb_ref, wo_ref,
                  g1_ref, b1_ref, x1_ref, *, parts):
    sub = x_ref.shape[0] // parts
    for s in range(parts):
        r = pl.ds(s * sub, sub)
        ya = jnp.dot(oa_ref[r, :], wa_ref[...], preferred_element_type=F32)
        yb = jnp.dot(ob_ref[r, :], wb_ref[...], preferred_element_type=F32)
        mixed = (_sigmoid(ga_ref[r, :].astype(F32)) * ya + _sigmoid(gb_ref[r, :].astype(F32)) * yb)
        xn = _layernorm_rows(x_ref[r, :], g_ref[...], b_ref[...])
        y = DEEPNORM_ALPHA * xn + _dot(mixed, wo_ref[...])
        x1_ref[r, :] = _layernorm_rows(y, g1_ref[...], b1_ref[...])


def _merge(x2d, ln_g, ln_b, oa, ob, h2d, wa, wb, wo, ln1_g, ln1_b, *, tm):
    rows = x2d.shape[0]
    full = lambda i: (0, 0)
    return pl.pallas_call(
        functools.partial(_merge_kernel, parts=2 if tm >= 512 else 1),
        grid=(rows // tm,),
        in_specs=[
            pl.BlockSpec((tm, D_MODEL), lambda i: (i, 0)),
            pl.BlockSpec((1, D_MODEL), full),
            pl.BlockSpec((1, D_MODEL), full),
            pl.BlockSpec((tm, GDN_W), lambda i: (i, 0)),
            pl.BlockSpec((tm, ATT_W), lambda i: (i, 0)),
            pl.BlockSpec((tm, D_MODEL), lambda i: (i, COL_GA // D_MODEL)),
            pl.BlockSpec((tm, D_MODEL), lambda i: (i, COL_GB // D_MODEL)),
            pl.BlockSpec((GDN_W, D_MODEL), full),
            pl.BlockSpec((ATT_W, D_MODEL), full),
            pl.BlockSpec((D_MODEL, D_MODEL), full),
            pl.BlockSpec((1, D_MODEL), full),
            pl.BlockSpec((1, D_MODEL), full),
        ],
        out_specs=pl.BlockSpec((tm, D_MODEL), lambda i: (i, 0)),
        out_shape=jax.ShapeDtypeStruct((rows, D_MODEL), F32),
        compiler_params=_cparams(("arbitrary",)),
        name="merge",
    )(x2d, ln_g, ln_b, oa, ob, h2d, h2d, wa, wb, wo, ln1_g, ln1_b)


def _gelu_tanh(x):
    return 0.5 * x * (1.0 + jnp.tanh(math.sqrt(2.0 / math.pi) * (x + 0.044715 * (x * x * x))))


def _ffn_kernel(x1_ref, pe_ref, wup_ref, cw_ref, wdn_ref, wple_ref, wpg_ref, g2_ref, b2_ref,
                finit_ref, y_ref, fstate_ref, gp_ref, acc_ref, *, tm, fc, ns):
    n = pl.program_id(1)
    pad = SUBLANES
    streams = range(ns)

    @pl.when(n == 0)
    def _():
        gp_ref[:, 0:pad, :] = finit_ref[...]

    x1 = x1_ref[...]
    x1b = x1.astype(BF16)

    def up(k):
        c0 = k * fc
        gate = jnp.dot(x1b, wup_ref[:, c0:c0 + fc], preferred_element_type=F32)
        val = jnp.dot(x1b, wup_ref[:, D_FF + c0:D_FF + c0 + fc], preferred_element_type=F32)
        for s in streams:
            gp_ref[s, pad:pad + tm, c0:c0 + fc] = gate[s * tm:(s + 1) * tm]
        return gate, val

    nk = D_FF // fc
    nxt = up(0)
    ple = jnp.dot(pe_ref[...].astype(BF16), wple_ref[...], preferred_element_type=F32)
    for k in range(nk):
        c0 = k * fc
        gate, val = nxt
        if k + 1 < nk:
            nxt = up(k + 1)
        conv = gate * cw_ref[FFN_CONV - 1:FFN_CONV, c0:c0 + fc]
        for i in range(1, FFN_CONV):
            prev = [gp_ref[s, pl.ds(pad - i, tm), c0:c0 + fc] for s in streams]
            conv = conv + (prev[0] if ns == 1 else jnp.concatenate(prev, axis=0)) * cw_ref[
                FFN_CONV - 1 - i:FFN_CONV - i, c0:c0 + fc]
        act = (_gelu_tanh(conv) * val).astype(BF16)
        down = jnp.dot(act, wdn_ref[c0:c0 + fc, :], preferred_element_type=F32)
        if k == 0:
            acc_ref[...] = down
        else:
            acc_ref[...] += down
    tail = gp_ref[:, tm:tm + pad, :]
    gp_ref[:, 0:pad, :] = tail
    fstate_ref[...] = tail

    r = DEEPNORM_ALPHA * x1 + acc_ref[...]
    gate_p = _sigmoid(_dot(r, wpg_ref[...]))
    r = r + gate_p * ple
    y_ref[...] = _layernorm_rows(r, g2_ref[...], b2_ref[...])


def _ffn(x1, pe, wup, conv_w, wdn, wple, wpg, ln2_g, ln2_b, finit, *, batch, t, tm, fc, ns):
    nt = t // tm
    assert batch % ns == 0 and (ns == 1 or nt == 1)
    rows = batch * t
    tile = ns * tm
    full = lambda b, n: (0, 0)
    return pl.pallas_call(
        functools.partial(_ffn_kernel, tm=tm, fc=fc, ns=ns),
        grid=(batch // ns, nt),
        in_specs=[
            pl.BlockSpec((tile, D_MODEL), lambda b, n: (b * nt + n, 0)),
            pl.BlockSpec((tile, PLE_DIM), lambda b, n: (b * nt + n, 0)),
            pl.BlockSpec((D_MODEL, 2 * D_FF), full, pipeline_mode=pl.Buffered(1)),
            pl.BlockSpec((FFN_CONV, D_FF), full),
            pl.BlockSpec((D_FF, D_MODEL), full, pipeline_mode=pl.Buffered(1)),
            pl.BlockSpec((PLE_DIM, D_MODEL), full),
            pl.BlockSpec((D_MODEL, D_MODEL), full),
            pl.BlockSpec((1, D_MODEL), full),
            pl.BlockSpec((1, D_MODEL), full),
            pl.BlockSpec((ns, SUBLANES, D_FF), lambda b, n: (b, 0, 0)),
        ],
        out_specs=[
            pl.BlockSpec((tile, D_MODEL), lambda b, n: (b * nt + n, 0)),
            pl.BlockSpec((ns, SUBLANES, D_FF), lambda b, n: (b, 0, 0)),
        ],
        out_shape=[
            jax.ShapeDtypeStruct((rows, D_MODEL), F32),
            jax.ShapeDtypeStruct((batch, SUBLANES, D_FF), F32),
        ],
        scratch_shapes=[pltpu.VMEM((ns, SUBLANES + tm, D_FF), F32), pltpu.VMEM((tile, D_MODEL), F32)],
        compiler_params=_cparams(("arbitrary", "arbitrary")),
        name="ffn",
    )(x1, pe, wup, conv_w, wdn, wple, wpg, ln2_g, ln2_b, finit)


def _rel_bias(table, n_q, n_past, n_k):
    span = n_q + n_k - 1
    rel = n_q - 1 + n_past - jnp.arange(span)
    rev = table[:, jnp.clip(rel, -REL_CLIP, REL_CLIP) + REL_CLIP].astype(F32)
    heads = table.shape[0]
    period = jnp.pad(rev, ((0, 0), (0, 1)))
    skew = jnp.tile(period, (1, n_q))[:, :n_q * span].reshape(heads, n_q, span)
    return skew[:, :, n_q - 1:n_q - 1 + n_k]


def _pad_state(state, rows):
    b, r, ch = state.shape
    return jnp.concatenate([jnp.zeros((b, rows - r, ch), F32), state.astype(F32)], axis=1)


def _layer(x, pe, gconv_prev, s_gdn, k_prev, v_prev, fconv_prev, wts, *, tm_proj, tm_merge, tm_ffn, gdn_nb,
           att_chunks=1, ffn_ns=1):
    (ln_in_g, ln_in_b, w_proj, gdn_conv_w, a_log, dt_bias, norm_w, rel_table, w_a, w_b, w_o,
     ln1_g, ln1_b, w_up, ffn_conv_w, w_dn, w_ple, w_pg, ln2_g, ln2_b) = wts
    b, t, _ = x.shape
    rows = b * t
    x2d = x.reshape(rows, D_MODEL)

    prompt = k_prev is None
    tm_proj = min(tm_proj, t if prompt else rows)
    pad_rows = tm_proj if prompt else 0
    assert pad_rows >= BAND or not prompt
    h2d, ab, q_b, k_b, v_b = _in_proj(x2d, ln_in_g, ln_in_b, *w_proj, tm=tm_proj,
                                      t=t if prompt else rows, pad_rows=pad_rows)

    o_a, gconv_new, s_new = _gdn(h2d, ab, gdn_conv_w, a_log, dt_bias, norm_w,
                                 _pad_state(gconv_prev, SUBLANES), s_gdn.astype(F32), batch=b, t=t,
                                 nb=math.gcd(gdn_nb, b))

    k_b = k_b.reshape(b, -1, ATT_W)
    v_b = v_b.reshape(b, -1, ATT_W)
    if prompt:
        tq = att_chunks * CHUNK
        width = BAND + tq
        own = (jnp.arange(tq) // CHUNK)[:, None] * CHUNK
        key = jnp.arange(width)[None, :]
        in_band = (key >= own) & (key < own + BAND + CHUNK)
        bias = jnp.where(in_band, _rel_bias(rel_table, tq, BAND, width), -jnp.inf)
        o_b = _attn(q_b, k_b, v_b, bias, batch=b, t=t, tq=tq, width=width, pad=BAND,
                    koff=pad_rows - BAND)
        keep = min(BAND, t)
        k_new, v_new = k_b[:, pad_rows + t - keep:], v_b[:, pad_rows + t - keep:]
    else:
        n_past = k_prev.shape[1]
        bias = _rel_bias(rel_table, t, n_past, n_past + t)
        o_b = _attn_step(q_b, k_b.reshape(rows, ATT_W), v_b.reshape(rows, ATT_W),
                         jnp.transpose(k_prev, (0, 2, 3, 1)).astype(F32),
                         jnp.transpose(v_prev, (0, 2, 3, 1)).astype(F32), bias, batch=b, t=t)
        k_new, v_new = k_b, v_b

    x1 = _merge(x2d, ln_in_g, ln_in_b, o_a, o_b, h2d, w_a, w_b, w_o, ln1_g, ln1_b,
                tm=min(tm_merge, rows))

    y, fconv_new = _ffn(x1, pe.reshape(rows, PLE_DIM), w_up, ffn_conv_w, w_dn, w_ple, w_pg,
                        ln2_g, ln2_b, _pad_state(fconv_prev, SUBLANES), batch=b, t=t,
                        tm=min(tm_ffn, t), fc=256, ns=math.gcd(ffn_ns, b))

    keep_k = k_new.shape[1]
    return (y.reshape(b, t, D_MODEL),
            gconv_new[:, SUBLANES - (GDN_CONV - 1):],
            s_new,
            k_new.astype(F32).reshape(b, keep_k, ATT_HEADS, ATT_DH),
            v_new.astype(F32).reshape(b, keep_k, ATT_HEADS, ATT_DH),
            fconv_new[:, SUBLANES - (FFN_CONV - 1):])


def _pad_lanes(v, width=LANES):
    return jnp.pad(v.astype(F32), (0, width - v.shape[0])).reshape(1, width)


def kernel(x_prompt, x_sample, state_gdn_conv, state_gdn, cache_attn_k, cache_attn_v, state_ffn_conv,
           p_prompt, p_sample, ln_in_g, ln_in_b, w_in, gdn_conv_w, gdn_a_log, gdn_dt_bias, gdn_norm_w,
           att_rel_bias, w_branch_a, w_branch_b, w_out, ln1_g, ln1_b, w_ffn_up, ffn_conv_w, w_ffn_down,
           w_ple, w_ple_gate, ln2_g, ln2_b):
    depth = w_in.shape[0]
    assert depth == 1
    bp = x_prompt.shape[0]
    row = lambda v: v.reshape(1, -1).astype(F32)

    xp, xs = x_prompt, x_sample
    new_p, new_s = [], []
    for i in range(depth):
        wi = w_in[i]
        o_ab = 4 * GDN_W
        o_att = o_ab + 2 * GDN_HEADS
        o_gate = o_att + 3 * ATT_W
        wt = wi.T.astype(BF16)
        w_proj = (wt[:o_ab], wt[o_att:o_gate], wt[o_gate:],
                  jnp.pad(wt[o_ab:o_att], ((0, LANES - 2 * GDN_HEADS), (0, 0))))
        wts = (row(ln_in_g), row(ln_in_b), w_proj, gdn_conv_w[i].astype(F32),
               _pad_lanes(gdn_a_log[i]), _pad_lanes(gdn_dt_bias[i]), row(gdn_norm_w[i]), att_rel_bias[i],
               w_branch_a[i].astype(BF16), w_branch_b[i].astype(BF16), w_out[i].astype(BF16),
               row(ln1_g[i]), row(ln1_b[i]), w_ffn_up[i].astype(BF16), ffn_conv_w[i].astype(F32),
               w_ffn_down[i].astype(BF16), w_ple[i].astype(BF16), w_ple_gate[i].astype(BF16),
               row(ln2_g[i]), row(ln2_b[i]))
        xp, st_p = _unpack(_layer(xp, p_prompt[i],
                                  jnp.zeros((bp, GDN_CONV - 1, GDN_CONV_CH), F32),
                                  jnp.zeros((bp, GDN_HEADS, GDN_DK, GDN_DV), F32),
                                  None, None,
                                  jnp.zeros((bp, FFN_CONV - 1, D_FF), F32), wts,
                                  tm_proj=1024, tm_merge=1024, tm_ffn=512, gdn_nb=4, att_chunks=4))
        xs, st_s = _unpack(_layer(xs, p_sample[i], state_gdn_conv[i], state_gdn[i], cache_attn_k[i],
                                  cache_attn_v[i], state_ffn_conv[i], wts,
                                  tm_proj=256, tm_merge=256, tm_ffn=256, gdn_nb=4, ffn_ns=16))
        new_p.append(st_p)
        new_s.append(st_s)
    p_gconv, p_gdn, p_k, p_v, p_fconv = [jnp.stack(a) for a in zip(*new_p)]
    s_gconv, s_gdn, s_k, s_v, s_fconv = [jnp.stack(a) for a in zip(*new_s)]
    return (xp, xs, p_gconv, p_gdn, p_k, p_v, p_fconv, s_gconv, s_gdn, s_k, s_v, s_fconv)


def _unpack(res):
    return res[0], res[1:]
```

```python
import functools
import math

import jax
import jax.numpy as jnp
from jax import lax
from jax.experimental import pallas as pl
from jax.experimental.pallas import tpu as pltpu

F32 = jnp.float32
BF16 = jnp.bfloat16

D_MODEL = 1024
CHUNK = 64
BAND = 512
GDN_HEADS = 8
GDN_DK = 128
GDN_DV = 128
GDN_CONV = 4
ATT_HEADS = 8
ATT_DH = 64
REL_CLIP = 128
D_FF = 2816
FFN_CONV = 3
PLE_DIM = 256
DEEPNORM_ALPHA = 2.0 ** 0.25
LN_EPS = 1e-5
RMS_EPS = 1e-6
L2_EPS = 1e-6

GDN_W = GDN_HEADS * GDN_DK
GDN_CONV_CH = 3 * GDN_W
ATT_W = ATT_HEADS * ATT_DH
LANES = 128
SUBLANES = 8

COL_QKV = 0
COL_Z = 3 * GDN_W
COL_GA = 4 * GDN_W
COL_GB = 5 * GDN_W
H_MAIN = 6 * GDN_W

LOG2E = math.log2(math.e)
ATT_QSCALE = ATT_DH ** -0.5 * LOG2E

FFN_DOWN_GROUP = 11

VMEM_LIMIT = 56 * 1024 * 1024


def _cparams(sem):
    return pltpu.CompilerParams(dimension_semantics=sem, vmem_limit_bytes=VMEM_LIMIT)


def _sigmoid(x):
    return 1.0 / (1.0 + jnp.exp(-x))


def _silu(x):
    return x * _sigmoid(x)


def _layernorm_rows(x, g, b):
    mu = jnp.mean(x, axis=-1, keepdims=True)
    xc = x - mu
    var = jnp.mean(xc * xc, axis=-1, keepdims=True)
    return xc * lax.rsqrt(var + LN_EPS) * g + b


def _dot(a, b):
    return jnp.dot(a.astype(BF16), b.astype(BF16), preferred_element_type=F32)


def _dot_nt(a, b):
    return lax.dot_general(a.astype(BF16), b.astype(BF16), (((1,), (1,)), ((), ())),
                           preferred_element_type=F32)


def _dot_tn(a, b):
    return lax.dot_general(a.astype(BF16), b.astype(BF16), (((0,), (0,)), ((), ())),
                           preferred_element_type=F32)


def _split3(x):
    x1 = x.astype(BF16)
    r1 = x - x1.astype(F32)
    x2 = r1.astype(BF16)
    x3 = (r1 - x2.astype(F32)).astype(BF16)
    return x1, x2, x3


def _in_proj_kernel(x_ref, g_ref, b_ref, wa_ref, watt_ref, wg_ref, wab_ref, kz_ref, vz_ref,
                    h_ref, ab_ref, q_ref, k_ref, v_ref, xn_ref, *, sub, n_a):
    del kz_ref, vz_ref
    j = pl.program_id(1)
    tm = x_ref.shape[0]

    @pl.when(j == 0)
    def _():
        def body(i, c):
            r = pl.multiple_of(i * sub, sub)
            xn = _layernorm_rows(x_ref[pl.ds(r, sub), :], g_ref[...], b_ref[...])
            xn_ref[pl.ds(r, sub), :] = xn.astype(BF16)
            return c
        lax.fori_loop(0, tm // sub, body, 0)
        ab_ref[...] = _dot_nt(xn_ref[...], wab_ref[...])

    @pl.when(j < n_a)
    def _():
        h_ref[...] = _dot_nt(xn_ref[...], wa_ref[...]).astype(BF16)

    @pl.when(j == n_a)
    def _():
        for o_ref, c0, mult in ((q_ref, 0, ATT_QSCALE), (k_ref, ATT_W, None), (v_ref, 2 * ATT_W, None)):
            proj = _dot_nt(xn_ref[...], watt_ref[c0:c0 + ATT_W, :])
            o_ref[...] = (proj if mult is None else proj * mult).astype(BF16)

    @pl.when(j == n_a + 1)
    def _():
        h_ref[...] = _dot_nt(xn_ref[...], wg_ref[...]).astype(BF16)


def _in_proj(x2d, ln_g, ln_b, w_a, w_att, w_g, w_ab, *, tm, t, pad_rows):
    rows = x2d.shape[0]
    tn = w_g.shape[0]
    n_a = w_a.shape[0] // tn
    assert rows % tm == 0 and t % tm == 0 and pad_rows % tm == 0
    assert w_a.shape[0] == n_a * tn and w_a.shape[0] + tn == H_MAIN and w_att.shape[0] == 3 * ATT_W
    sub = min(tm, 128)
    per_stream = t // tm
    stride = per_stream + pad_rows // tm
    kv_rows = (rows // t) * (t + pad_rows)
    kv_block = lambda i, j: ((i // per_stream) * stride + pad_rows // tm + i % per_stream, 0)
    zeros = jnp.zeros((kv_rows, ATT_W), BF16)
    const = lambda i, j: (0, 0)
    return pl.pallas_call(
        functools.partial(_in_proj_kernel, sub=sub, n_a=n_a),
        grid=(rows // tm, n_a + 2),
        in_specs=[
            pl.BlockSpec((tm, D_MODEL), lambda i, j: (i, 0)),
            pl.BlockSpec((1, D_MODEL), const),
            pl.BlockSpec((1, D_MODEL), const),
            pl.BlockSpec((tn, D_MODEL), lambda i, j: (jnp.minimum(j, n_a - 1), 0)),
            pl.BlockSpec((3 * ATT_W, D_MODEL), const, pipeline_mode=pl.Buffered(1)),
            pl.BlockSpec((tn, D_MODEL), const, pipeline_mode=pl.Buffered(1)),
            pl.BlockSpec((LANES, D_MODEL), const),
            pl.BlockSpec(memory_space=pl.ANY),
            pl.BlockSpec(memory_space=pl.ANY),
        ],
        out_specs=[
            pl.BlockSpec((tm, tn), lambda i, j: (i, jnp.where(j < n_a, j, jnp.maximum(j - 1, n_a - 1)))),
            pl.BlockSpec((tm, LANES), lambda i, j: (i, 0)),
            pl.BlockSpec((tm, ATT_W), lambda i, j: (i, 0)),
            pl.BlockSpec((tm, ATT_W), kv_block),
            pl.BlockSpec((tm, ATT_W), kv_block),
        ],
        out_shape=[
            jax.ShapeDtypeStruct((rows, H_MAIN), BF16),
            jax.ShapeDtypeStruct((rows, LANES), F32),
            jax.ShapeDtypeStruct((rows, ATT_W), BF16),
            jax.ShapeDtypeStruct((kv_rows, ATT_W), BF16),
            jax.ShapeDtypeStruct((kv_rows, ATT_W), BF16),
        ],
        input_output_aliases={7: 3, 8: 4},
        scratch_shapes=[pltpu.VMEM((tm, D_MODEL), BF16)],
        compiler_params=_cparams(("arbitrary", "arbitrary")),
        name="in_proj",
    )(x2d, ln_g, ln_b, w_a, w_att, w_g, w_ab, zeros, zeros)


def _gdn_kernel(qkv_ref, z_ref, ab_ref, cw_ref, alog_ref, dtb_ref, nw_ref, cinit_ref, s0_ref,
                o_ref, cstate_ref, s_ref,
                xp_ref, q_st, k_st, v_st, dec_st, beta_st, dect_st, lm_st, *, c, nb):
    n = pl.program_id(1)
    pad = SUBLANES
    cur = lax.rem(n, 2)
    prv = 1 - cur
    heads = range(GDN_HEADS)
    pairs = [(bi, h) for bi in range(nb) for h in heads]

    @pl.when(n == 0)
    def _():
        xp_ref[:, 0:pad, :] = cinit_ref[...]
        s_ref[...] = s0_ref[...]
        for st in (q_st, k_st, v_st, dec_st, beta_st, dect_st, lm_st):
            st[1] = jnp.zeros(st.shape[1:], F32)

    row = lax.broadcasted_iota(jnp.int32, (c, c), 0)
    col = lax.broadcasted_iota(jnp.int32, (c, c), 1)
    causal = row >= col
    strict = row > col
    eye = jnp.where(row == col, 1.0, 0.0).astype(F32)
    tri = jnp.where(causal, 1.0, 0.0).astype(BF16)

    for bi in range(nb):
        ab = ab_ref[bi]
        sp_in = ab + dtb_ref[...]
        softplus = jnp.maximum(sp_in, 0.0) + jnp.log(1.0 + jnp.exp(-jnp.abs(sp_in)))
        g_all = -jnp.exp(alog_ref[...]) * softplus
        g1, g2, g3 = _split3(g_all)
        dec_all = (jnp.dot(tri, g1, preferred_element_type=F32)
                   + (jnp.dot(tri, g2, preferred_element_type=F32)
                      + jnp.dot(tri, g3, preferred_element_type=F32)))
        dec_t = dec_all.T
        dec_st[cur, bi] = dec_all
        dect_st[cur, bi] = dec_t
        beta_st[cur, bi] = _sigmoid(ab)
        for h in heads:
            lm_st[cur, bi * GDN_HEADS + h] = jnp.exp(
                jnp.where(causal, dec_all[:, h:h + 1] - dec_t[h:h + 1, :], -jnp.inf))

        xp_ref[bi, pad:pad + c, :] = qkv_ref[bi].astype(F32)

    def conv_silu(bi, c0, width):
        acc = xp_ref[bi, pl.ds(pad, c), pl.ds(c0, width)] * cw_ref[GDN_CONV - 1:GDN_CONV, pl.ds(c0, width)]
        for i in range(1, GDN_CONV):
            acc = acc + (xp_ref[bi, pl.ds(pad - i, c), pl.ds(c0, width)]
                         * cw_ref[GDN_CONV - 1 - i:GDN_CONV - i, pl.ds(c0, width)])
        return _silu(acc)

    def q_job(bi, h):
        lo = h * GDN_DK
        xq = conv_silu(bi, lo, GDN_DK)
        q_st[cur, bi, :, lo:lo + GDN_DK] = (
            xq * lax.rsqrt(jnp.sum(xq * xq, axis=-1, keepdims=True) + L2_EPS) * (GDN_DK ** -0.5))

    def k_job(bi, h):
        lo = h * GDN_DK
        xk = conv_silu(bi, GDN_W + lo, GDN_DK)
        k_st[cur, bi, :, lo:lo + GDN_DK] = (
            xk * lax.rsqrt(jnp.sum(xk * xk, axis=-1, keepdims=True) + L2_EPS))

    def v_job(bi, h):
        lo = h * GDN_DV
        v_st[cur, bi, :, lo:lo + GDN_DV] = conv_silu(bi, 2 * GDN_W + lo, GDN_DV)

    jobs = [functools.partial(job, bi, h) for bi, h in pairs for job in (q_job, k_job, v_job)]
    n_stage = 2 * (int(math.log2(c)) - 1) + 4
    per_stage = -(-len(jobs) // n_stage)

    def front_jobs():
        for job in jobs[:per_stage]:
            job()
        del jobs[:per_stage]

    def cols(ref, bi, h, width=GDN_DK):
        return ref[prv, bi, :, h * width:(h + 1) * width]

    qn = [cols(q_st, bi, h) for bi, h in pairs]
    kn = [cols(k_st, bi, h) for bi, h in pairs]
    xv = [cols(v_st, bi, h, GDN_DV) for bi, h in pairs]
    dcol = [dec_st[prv, bi, :, h:h + 1] for bi, h in pairs]
    beta = [beta_st[prv, bi, :, GDN_HEADS + h:GDN_HEADS + h + 1] for bi, h in pairs]
    lmask = [lm_st[prv, bi * GDN_HEADS + h] for bi, h in pairs]
    idx = range(len(pairs))
    kb = [kn[i] * beta[i] for i in idx]
    e_d = [jnp.exp(d) for d in dcol]
    gram = [_dot_nt(jnp.concatenate([kb[i], qn[i]], axis=0), kn[i]) for i in idx]
    front_jobs()
    m = [jnp.where(strict, gram[i][:c] * lmask[i], 0.0) for i in idx]
    qk = [gram[i][c:] * lmask[i] for i in idx]
    p = [eye - m[i] for i in idx]
    mk = m
    for _ in range(int(math.log2(c)) - 1):
        mk = [_dot(x, x) for x in mk]
        front_jobs()
        p = [p[i] + _dot(p[i], mk[i]) for i in idx]
        front_jobs()
    sol = [_dot(p[i], jnp.concatenate([xv[i] * beta[i], kb[i] * e_d[i]], axis=1)) for i in idx]
    front_jobs()
    s_old = [s_ref[bi, h] for bi, h in pairs]
    ws_qs = [_dot(jnp.concatenate([sol[i][:, GDN_DV:], qn[i] * e_d[i]], axis=0), s_old[i])
             for i in idx]
    front_jobs()
    v_new = [sol[i][:, :GDN_DV] - ws_qs[i][:c] for i in idx]
    dl = [d[c - 1:c, :] for d in dcol]
    kd = [kn[i] * jnp.exp(dl[i] - dcol[i]) for i in idx]
    s_upd = [_dot_tn(kd[i], v_new[i]) for i in idx]
    o_in = [_dot(qk[i], v_new[i]) for i in idx]
    front_jobs()
    assert not jobs
    for i, (bi, h) in enumerate(pairs):
        s_ref[bi, h] = s_old[i] * jnp.exp(dl[i]) + s_upd[i]
        o = ws_qs[i][c:] + o_in[i]
        zh = z_ref[bi, :, pl.ds(h * GDN_DV, GDN_DV)].astype(F32)
        o = (o * lax.rsqrt(jnp.mean(o * o, axis=-1, keepdims=True) + RMS_EPS) * nw_ref[...]
             * _silu(zh))
        o_ref[bi, :, pl.ds(h * GDN_DV, GDN_DV)] = o.astype(BF16)
    for bi in range(nb):
        tail = xp_ref[bi, c:c + pad, :]
        xp_ref[bi, 0:pad, :] = tail
        cstate_ref[bi] = tail


def _gdn(h2d, ab, conv_w, a_log, dt_bias, norm_w, conv_init, s0, *, batch, t, nb):
    c = min(CHUNK, t)
    nc = t // c
    assert batch % nb == 0
    h3d = h2d.reshape(batch, t, H_MAIN)
    ab3d = ab.reshape(batch, t, LANES)
    front = lambda b, n: (b, jnp.minimum(n, nc - 1), 0)
    back = lambda b, n: (b, jnp.maximum(n - 1, 0), 0)
    stash = lambda *shape: pltpu.VMEM((2, *shape), F32)
    o_a, cstate, s_new = pl.pallas_call(
        functools.partial(_gdn_kernel, c=c, nb=nb),
        grid=(batch // nb, nc + 1),
        in_specs=[
            pl.BlockSpec((nb, c, GDN_CONV_CH), front),
            pl.BlockSpec((nb, c, GDN_W), lambda b, n: (b, jnp.maximum(n - 1, 0), COL_Z // GDN_W)),
            pl.BlockSpec((nb, c, LANES), front),
            pl.BlockSpec((GDN_CONV, GDN_CONV_CH), lambda b, n: (0, 0)),
            pl.BlockSpec((1, LANES), lambda b, n: (0, 0)),
            pl.BlockSpec((1, LANES), lambda b, n: (0, 0)),
            pl.BlockSpec((1, GDN_DV), lambda b, n: (0, 0)),
            pl.BlockSpec((nb, SUBLANES, GDN_CONV_CH), lambda b, n: (b, 0, 0)),
            pl.BlockSpec((nb, GDN_HEADS, GDN_DK, GDN_DV), lambda b, n: (b, 0, 0, 0)),
        ],
        out_specs=[
            pl.BlockSpec((nb, c, GDN_W), back),
            pl.BlockSpec((nb, SUBLANES, GDN_CONV_CH), lambda b, n: (b, 0, 0)),
            pl.BlockSpec((nb, GDN_HEADS, GDN_DK, GDN_DV), lambda b, n: (b, 0, 0, 0)),
        ],
        out_shape=[
            jax.ShapeDtypeStruct((batch, t, GDN_W), BF16),
            jax.ShapeDtypeStruct((batch, SUBLANES, GDN_CONV_CH), F32),
            jax.ShapeDtypeStruct((batch, GDN_HEADS, GDN_DK, GDN_DV), F32),
        ],
        scratch_shapes=[
            pltpu.VMEM((nb, SUBLANES + c, GDN_CONV_CH), F32),
            stash(nb, c, GDN_W), stash(nb, c, GDN_W), stash(nb, c, GDN_W),
            stash(nb, c, LANES), stash(nb, c, LANES), stash(nb, LANES, c),
            stash(nb * GDN_HEADS, c, c),
        ],
        compiler_params=_cparams(("arbitrary", "arbitrary")),
        name="gdn",
    )(h3d, h3d, ab3d, conv_w, a_log, dt_bias, norm_w, conv_init, s0)
    return o_a.reshape(batch * t, GDN_W), cstate, s_new


def _attn_kernel(q_ref, k_ref, v_ref, bias_ref, o_ref, *, tq, width, pad, koff):
    n = pl.program_id(1)
    start = pl.multiple_of(koff + n * tq, tq)
    heads = range(ATT_HEADS)
    cols = [slice(h * ATT_DH, (h + 1) * ATT_DH) for h in heads]
    has_pad_keys = n * tq < pad

    def tile(mask_pad_keys):
        kwin = k_ref[0, pl.ds(start, width), :]
        vwin = v_ref[0, pl.ds(start, width), :]
        s = [_dot_nt(q_ref[:, cols[h]], kwin[:, cols[h]]) + bias_ref[h] for h in heads]
        if mask_pad_keys:
            kpos = n * tq - pad + lax.broadcasted_iota(jnp.int32, (tq, width), 1)
            s = [jnp.where(kpos >= 0, x, -jnp.inf) for x in s]
        p = [jnp.exp2(x - jnp.max(x, axis=-1, keepdims=True)) for x in s]
        l = [jnp.sum(x, axis=-1, keepdims=True) for x in p]
        pv = [_dot(p[h], vwin[:, cols[h]]) for h in heads]
        o_ref[...] = jnp.concatenate([pv[h] / l[h] for h in heads], axis=1).astype(BF16)

    pl.when(has_pad_keys)(functools.partial(tile, True))
    pl.when(jnp.logical_not(has_pad_keys))(functools.partial(tile, False))


def _attn(q2d, k3d, v3d, bias, *, batch, t, tq, width, pad, koff):
    nq = t // tq
    tk = k3d.shape[1]
    assert koff % tq == 0 and koff + (nq - 1) * tq + width <= tk
    return pl.pallas_call(
        functools.partial(_attn_kernel, tq=tq, width=width, pad=pad, koff=koff),
        grid=(batch, nq),
        in_specs=[
            pl.BlockSpec((tq, ATT_W), lambda b, n: (b * nq + n, 0)),
            pl.BlockSpec((1, tk, ATT_W), lambda b, n: (b, 0, 0)),
            pl.BlockSpec((1, tk, ATT_W), lambda b, n: (b, 0, 0)),
            pl.BlockSpec((ATT_HEADS, tq, width), lambda b, n: (0, 0, 0)),
        ],
        out_specs=pl.BlockSpec((tq, ATT_W), lambda b, n: (b * nq + n, 0)),
        out_shape=jax.ShapeDtypeStruct((batch * t, ATT_W), BF16),
        compiler_params=_cparams(("arbitrary", "arbitrary")),
        name="attn",
    )(q2d, k3d, v3d, bias)


def _attn_step_kernel(q_ref, kn_ref, vn_ref, kt_ref, vt_ref, bc_ref, bn_ref, o_ref):
    heads = range(ATT_HEADS)
    cols = [slice(h * ATT_DH, (h + 1) * ATT_DH) for h in heads]
    q = [q_ref[:, cols[h]] for h in heads]
    s_c = [_dot(q[h], kt_ref[0, h]) + bc_ref[h] for h in heads]
    s_n = [_dot_nt(q[h], kn_ref[:, cols[h]]) + bn_ref[h] for h in heads]
    mx = [jnp.maximum(jnp.max(s_c[h], axis=-1, keepdims=True), jnp.max(s_n[h], axis=-1, keepdims=True))
          for h in heads]
    p_c = [jnp.exp2(s_c[h] - mx[h]) for h in heads]
    p_n = [jnp.exp2(s_n[h] - mx[h]) for h in heads]
    l = [jnp.sum(p_c[h], axis=-1, keepdims=True) + jnp.sum(p_n[h], axis=-1, keepdims=True) for h in heads]
    pv = [_dot_nt(p_c[h], vt_ref[0, h]) + _dot(p_n[h], vn_ref[:, cols[h]]) for h in heads]
    o_ref[...] = jnp.concatenate([pv[h] / l[h] for h in heads], axis=1).astype(BF16)


def _attn_step(q2d, kn2d, vn2d, kt, vt, bias, *, batch, t):
    n_past = kt.shape[-1]
    rows = lambda b: (b, 0)
    cache = lambda b: (b, 0, 0, 0)
    const = lambda b: (0, 0, 0)
    return pl.pallas_call(
        _attn_step_kernel,
        grid=(batch,),
        in_specs=[
            pl.BlockSpec((t, ATT_W), rows),
            pl.BlockSpec((t, ATT_W), rows),
            pl.BlockSpec((t, ATT_W), rows),
            pl.BlockSpec((1, ATT_HEADS, ATT_DH, n_past), cache),
            pl.BlockSpec((1, ATT_HEADS, ATT_DH, n_past), cache),
            pl.BlockSpec((ATT_HEADS, t, n_past), const),
            pl.BlockSpec((ATT_HEADS, t, t), const),
        ],
        out_specs=pl.BlockSpec((t, ATT_W), rows),
        out_shape=jax.ShapeDtypeStruct((batch * t, ATT_W), BF16),
        compiler_params=_cparams(("arbitrary",)),
        name="attn_step",
    )(q2d, kn2d, vn2d, kt, vt, bias[:, :, :n_past], bias[:, :, n_past:])


def _merge_kernel(x_ref, g_ref, b_ref, oa_ref, ob_ref, ga_ref, gb_ref, wa_ref, wb_ref, wo_ref,
                  g1_ref, b1_ref, x1_ref, *, parts):
    sub = x_ref.shape[0] // parts
    for s in range(parts):
        r = pl.ds(s * sub, sub)
        ya = jnp.dot(oa_ref[r, :], wa_ref[...], preferred_element_type=F32)
        yb = jnp.dot(ob_ref[r, :], wb_ref[...], preferred_element_type=F32)
        mixed = (_sigmoid(ga_ref[r, :].astype(F32)) * ya + _sigmoid(gb_ref[r, :].astype(F32)) * yb)
        xn = _layernorm_rows(x_ref[r, :], g_ref[...], b_ref[...])
        y = DEEPNORM_ALPHA * xn + _dot(mixed, wo_ref[...])
        x1_ref[r, :] = _layernorm_rows(y, g1_ref[...], b1_ref[...])


def _merge(x2d, ln_g, ln_b, oa, ob, h2d, wa, wb, wo, ln1_g, ln1_b, *, tm):
    rows = x2d.shape[0]
    full = lambda i: (0, 0)
    return pl.pallas_call(
        functools.partial(_merge_kernel, parts=2 if tm >= 512 else 1),
        grid=(rows // tm,),
        in_specs=[
            pl.BlockSpec((tm, D_MODEL), lambda i: (i, 0)),
            pl.BlockSpec((1, D_MODEL), full),
            pl.BlockSpec((1, D_MODEL), full),
            pl.BlockSpec((tm, GDN_W), lambda i: (i, 0)),
            pl.BlockSpec((tm, ATT_W), lambda i: (i, 0)),
            pl.BlockSpec((tm, D_MODEL), lambda i: (i, COL_GA // D_MODEL)),
            pl.BlockSpec((tm, D_MODEL), lambda i: (i, COL_GB // D_MODEL)),
            pl.BlockSpec((GDN_W, D_MODEL), full),
            pl.BlockSpec((ATT_W, D_MODEL), full),
            pl.BlockSpec((D_MODEL, D_MODEL), full),
            pl.BlockSpec((1, D_MODEL), full),
            pl.BlockSpec((1, D_MODEL), full),
        ],
        out_specs=pl.BlockSpec((tm, D_MODEL), lambda i: (i, 0)),
        out_shape=jax.ShapeDtypeStruct((rows, D_MODEL), F32),
        compiler_params=_cparams(("arbitrary",)),
        name="merge",
    )(x2d, ln_g, ln_b, oa, ob, h2d, h2d, wa, wb, wo, ln1_g, ln1_b)


def _gelu_tanh(x):
    return 0.5 * x * (1.0 + jnp.tanh(math.sqrt(2.0 / math.pi) * (x + 0.044715 * (x * x * x))))


def _ffn_kernel(x1_ref, pe_ref, wup_ref, cw_ref, wdn_ref, wple_ref, wpg_ref, g2_ref, b2_ref,
                finit_ref, y_ref, fstate_ref, gp_ref, acc_ref, act_ref, *, tm, fc, ns, down_group):
    n = pl.program_id(1)
    pad = SUBLANES
    streams = range(ns)

    @pl.when(n == 0)
    def _():
        gp_ref[:, 0:pad, :] = finit_ref[...]

    x1 = x1_ref[...]
    x1b = x1.astype(BF16)

    def up(k):
        c0 = k * fc
        gate = jnp.dot(x1b, wup_ref[:, c0:c0 + fc], preferred_element_type=F32)
        val = jnp.dot(x1b, wup_ref[:, D_FF + c0:D_FF + c0 + fc], preferred_element_type=F32)
        for s in streams:
            gp_ref[s, pad:pad + tm, c0:c0 + fc] = gate[s * tm:(s + 1) * tm]
        return gate, val

    nk = D_FF // fc
    nxt = up(0)
    ple = jnp.dot(pe_ref[...].astype(BF16), wple_ref[...], preferred_element_type=F32)
    for k in range(nk):
        c0 = k * fc
        gate, val = nxt
        if k + 1 < nk:
            nxt = up(k + 1)
        conv = gate * cw_ref[FFN_CONV - 1:FFN_CONV, c0:c0 + fc]
        for i in range(1, FFN_CONV):
            prev = [gp_ref[s, pl.ds(pad - i, tm), c0:c0 + fc] for s in streams]
            conv = conv + (prev[0] if ns == 1 else jnp.concatenate(prev, axis=0)) * cw_ref[
                FFN_CONV - 1 - i:FFN_CONV - i, c0:c0 + fc]
        act_ref[:, c0:c0 + fc] = (_gelu_tanh(conv) * val).astype(BF16)
        if (k + 1) % down_group == 0 or k + 1 == nk:
            g0 = (k // down_group) * down_group * fc
            down = jnp.dot(act_ref[:, g0:c0 + fc], wdn_ref[g0:c0 + fc, :], preferred_element_type=F32)
            if g0 == 0:
                acc_ref[...] = down
            else:
                acc_ref[...] += down
    tail = gp_ref[:, tm:tm + pad, :]
    gp_ref[:, 0:pad, :] = tail
    fstate_ref[...] = tail

    r = DEEPNORM_ALPHA * x1 + acc_ref[...]
    gate_p = _sigmoid(_dot(r, wpg_ref[...]))
    r = r + gate_p * ple
    y_ref[...] = _layernorm_rows(r, g2_ref[...], b2_ref[...])


def _ffn(x1, pe, wup, conv_w, wdn, wple, wpg, ln2_g, ln2_b, finit, *, batch, t, tm, fc, ns):
    nt = t // tm
    assert batch % ns == 0 and (ns == 1 or nt == 1)
    rows = batch * t
    tile = ns * tm
    full = lambda b, n: (0, 0)
    return pl.pallas_call(
        functools.partial(_ffn_kernel, tm=tm, fc=fc, ns=ns, down_group=FFN_DOWN_GROUP),
        grid=(batch // ns, nt),
        in_specs=[
            pl.BlockSpec((tile, D_MODEL), lambda b, n: (b * nt + n, 0)),
            pl.BlockSpec((tile, PLE_DIM), lambda b, n: (b * nt + n, 0)),
            pl.BlockSpec((D_MODEL, 2 * D_FF), full, pipeline_mode=pl.Buffered(1)),
            pl.BlockSpec((FFN_CONV, D_FF), full),
            pl.BlockSpec((D_FF, D_MODEL), full, pipeline_mode=pl.Buffered(1)),
            pl.BlockSpec((PLE_DIM, D_MODEL), full),
            pl.BlockSpec((D_MODEL, D_MODEL), full),
            pl.BlockSpec((1, D_MODEL), full),
            pl.BlockSpec((1, D_MODEL), full),
            pl.BlockSpec((ns, SUBLANES, D_FF), lambda b, n: (b, 0, 0)),
        ],
        out_specs=[
            pl.BlockSpec((tile, D_MODEL), lambda b, n: (b * nt + n, 0)),
            pl.BlockSpec((ns, SUBLANES, D_FF), lambda b, n: (b, 0, 0)),
        ],
        out_shape=[
            jax.ShapeDtypeStruct((rows, D_MODEL), F32),
            jax.ShapeDtypeStruct((batch, SUBLANES, D_FF), F32),
        ],
        scratch_shapes=[pltpu.VMEM((ns, SUBLANES + tm, D_FF), F32), pltpu.VMEM((tile, D_MODEL), F32),
                        pltpu.VMEM((tile, D_FF), BF16)],
        compiler_params=_cparams(("arbitrary", "arbitrary")),
        name="ffn",
    )(x1, pe, wup, conv_w, wdn, wple, wpg, ln2_g, ln2_b, finit)


def _rel_bias(table, n_q, n_past, n_k):
    span = n_q + n_k - 1
    rel = n_q - 1 + n_past - jnp.arange(span)
    rev = table[:, jnp.clip(rel, -REL_CLIP, REL_CLIP) + REL_CLIP].astype(F32)
    heads = table.shape[0]
    period = jnp.pad(rev, ((0, 0), (0, 1)))
    skew = jnp.tile(period, (1, n_q))[:, :n_q * span].reshape(heads, n_q, span)
    return skew[:, :, n_q - 1:n_q - 1 + n_k]


def _pad_state(state, rows):
    b, r, ch = state.shape
    return jnp.concatenate([jnp.zeros((b, rows - r, ch), F32), state.astype(F32)], axis=1)


def _layer(x, pe, gconv_prev, s_gdn, k_prev, v_prev, fconv_prev, wts, *, tm_proj, tm_merge, tm_ffn, gdn_nb,
           att_chunks=1, ffn_ns=1):
    (ln_in_g, ln_in_b, w_proj, gdn_conv_w, a_log, dt_bias, norm_w, rel_table, w_a, w_b, w_o,
     ln1_g, ln1_b, w_up, ffn_conv_w, w_dn, w_ple, w_pg, ln2_g, ln2_b) = wts
    b, t, _ = x.shape
    rows = b * t
    x2d = x.reshape(rows, D_MODEL)

    prompt = k_prev is None
    tm_proj = min(tm_proj, t if prompt else rows)
    pad_rows = tm_proj if prompt else 0
    assert pad_rows >= BAND or not prompt
    h2d, ab, q_b, k_b, v_b = _in_proj(x2d, ln_in_g, ln_in_b, *w_proj, tm=tm_proj,
                                      t=t if prompt else rows, pad_rows=pad_rows)

    o_a, gconv_new, s_new = _gdn(h2d, ab, gdn_conv_w, a_log, dt_bias, norm_w,
                                 _pad_state(gconv_prev, SUBLANES), s_gdn.astype(F32), batch=b, t=t,
                                 nb=math.gcd(gdn_nb, b))

    k_b = k_b.reshape(b, -1, ATT_W)
    v_b = v_b.reshape(b, -1, ATT_W)
    if prompt:
        tq = att_chunks * CHUNK
        width = BAND + tq
        own = (jnp.arange(tq) // CHUNK)[:, None] * CHUNK
        key = jnp.arange(width)[None, :]
        in_band = (key >= own) & (key < own + BAND + CHUNK)
        bias = jnp.where(in_band, _rel_bias(rel_table, tq, BAND, width) * LOG2E, -jnp.inf)
        o_b = _attn(q_b, k_b, v_b, bias, batch=b, t=t, tq=tq, width=width, pad=BAND,
                    koff=pad_rows - BAND)
        keep = min(BAND, t)
        k_new, v_new = k_b[:, pad_rows + t - keep:], v_b[:, pad_rows + t - keep:]
    else:
        n_past = k_prev.shape[1]
        bias = _rel_bias(rel_table, t, n_past, n_past + t) * LOG2E
        o_b = _attn_step(q_b, k_b.reshape(rows, ATT_W), v_b.reshape(rows, ATT_W),
                         jnp.transpose(k_prev, (0, 2, 3, 1)).astype(F32),
                         jnp.transpose(v_prev, (0, 2, 3, 1)).astype(F32), bias, batch=b, t=t)
        k_new, v_new = k_b, v_b

    x1 = _merge(x2d, ln_in_g, ln_in_b, o_a, o_b, h2d, w_a, w_b, w_o, ln1_g, ln1_b,
                tm=min(tm_merge, rows))

    y, fconv_new = _ffn(x1, pe.reshape(rows, PLE_DIM), w_up, ffn_conv_w, w_dn, w_ple, w_pg,
                        ln2_g, ln2_b, _pad_state(fconv_prev, SUBLANES), batch=b, t=t,
                        tm=min(tm_ffn, t), fc=256, ns=math.gcd(ffn_ns, b))

    keep_k = k_new.shape[1]
    return (y.reshape(b, t, D_MODEL),
            gconv_new[:, SUBLANES - (GDN_CONV - 1):],
            s_new,
            k_new.astype(F32).reshape(b, keep_k, ATT_HEADS, ATT_DH),
            v_new.astype(F32).reshape(b, keep_k, ATT_HEADS, ATT_DH),
            fconv_new[:, SUBLANES - (FFN_CONV - 1):])


def _pad_lanes(v, width=LANES):
    return jnp.pad(v.astype(F32), (0, width - v.shape[0])).reshape(1, width)


def kernel(x_prompt, x_sample, state_gdn_conv, state_gdn, cache_attn_k, cache_attn_v, state_ffn_conv,
           p_prompt, p_sample, ln_in_g, ln_in_b, w_in, gdn_conv_w, gdn_a_log, gdn_dt_bias, gdn_norm_w,
           att_rel_bias, w_branch_a, w_branch_b, w_out, ln1_g, ln1_b, w_ffn_up, ffn_conv_w, w_ffn_down,
           w_ple, w_ple_gate, ln2_g, ln2_b):
    depth = w_in.shape[0]
    assert depth == 1
    bp = x_prompt.shape[0]
    row = lambda v: v.reshape(1, -1).astype(F32)

    xp, xs = x_prompt, x_sample
    new_p, new_s = [], []
    for i in range(depth):
        wi = w_in[i]
        o_ab = 4 * GDN_W
        o_att = o_ab + 2 * GDN_HEADS
        o_gate = o_att + 3 * ATT_W
        wt = wi.T.astype(BF16)
        w_proj = (wt[:o_ab], wt[o_att:o_gate], wt[o_gate:],
                  jnp.pad(wt[o_ab:o_att], ((0, LANES - 2 * GDN_HEADS), (0, 0))))
        wts = (row(ln_in_g), row(ln_in_b), w_proj, gdn_conv_w[i].astype(F32),
               _pad_lanes(gdn_a_log[i]), _pad_lanes(gdn_dt_bias[i]), row(gdn_norm_w[i]), att_rel_bias[i],
               w_branch_a[i].astype(BF16), w_branch_b[i].astype(BF16), w_out[i].astype(BF16),
               row(ln1_g[i]), row(ln1_b[i]), w_ffn_up[i].astype(BF16), ffn_conv_w[i].astype(F32),
               w_ffn_down[i].astype(BF16), w_ple[i].astype(BF16), w_ple_gate[i].astype(BF16),
               row(ln2_g[i]), row(ln2_b[i]))
        xp, st_p = _unpack(_layer(xp, p_prompt[i],
                                  jnp.zeros((bp, GDN_CONV - 1, GDN_CONV_CH), F32),
                                  jnp.zeros((bp, GDN_HEADS, GDN_DK, GDN_DV), F32),
                                  None, None,
                                  jnp.zeros((bp, FFN_CONV - 1, D_FF), F32), wts,
                                  tm_proj=1024, tm_merge=1024, tm_ffn=512, gdn_nb=4, att_chunks=4))
        xs, st_s = _unpack(_layer(xs, p_sample[i], state_gdn_conv[i], state_gdn[i], cache_attn_k[i],
                                  cache_attn_v[i], state_ffn_conv[i], wts,
                                  tm_proj=256, tm_merge=256, tm_ffn=256, gdn_nb=4, ffn_ns=16))
        new_p.append(st_p)
        new_s.append(st_s)
    p_gconv, p_gdn, p_k, p_v, p_fconv = [jnp.stack(a) for a in zip(*new_p)]
    s_gconv, s_gdn, s_k, s_v, s_fconv = [jnp.stack(a) for a in zip(*new_s)]
    return (xp, xs, p_gconv, p_gdn, p_k, p_v, p_fconv, s_gconv, s_gdn, s_k, s_v, s_fconv)


def _unpack(res):
    return res[0], res[1:]
```

```python
import functools
import math

import jax
import jax.numpy as jnp
from jax import lax
from jax.experimental import pallas as pl
from jax.experimental.pallas import tpu as pltpu

F32 = jnp.float32
BF16 = jnp.bfloat16

D_MODEL = 1024
CHUNK = 64
BAND = 512
GDN_HEADS = 8
GDN_DK = 128
GDN_DV = 128
GDN_CONV = 4
ATT_HEADS = 8
ATT_DH = 64
REL_CLIP = 128
D_FF = 2816
FFN_CONV = 3
PLE_DIM = 256
DEEPNORM_ALPHA = 2.0 ** 0.25
LN_EPS = 1e-5
RMS_EPS = 1e-6
L2_EPS = 1e-6

GDN_W = GDN_HEADS * GDN_DK
GDN_CONV_CH = 3 * GDN_W
ATT_W = ATT_HEADS * ATT_DH
LANES = 128
SUBLANES = 8

COL_QKV = 0
COL_Z = 3 * GDN_W
COL_GA = 4 * GDN_W
COL_GB = 5 * GDN_W
H_MAIN = 6 * GDN_W

LOG2E = math.log2(math.e)
ATT_QSCALE = ATT_DH ** -0.5 * LOG2E

FFN_DOWN_GROUP = 11

VMEM_LIMIT = 56 * 1024 * 1024


def _cparams(sem):
    return pltpu.CompilerParams(dimension_semantics=sem, vmem_limit_bytes=VMEM_LIMIT)


def _sigmoid(x):
    return 1.0 / (1.0 + jnp.exp(-x))


def _silu(x):
    return x * _sigmoid(x)


def _layernorm_rows(x, g, b):
    mu = jnp.mean(x, axis=-1, keepdims=True)
    xc = x - mu
    var = jnp.mean(xc * xc, axis=-1, keepdims=True)
    return xc * lax.rsqrt(var + LN_EPS) * g + b


def _dot(a, b):
    return jnp.dot(a.astype(BF16), b.astype(BF16), preferred_element_type=F32)


def _dot_nt(a, b):
    return lax.dot_general(a.astype(BF16), b.astype(BF16), (((1,), (1,)), ((), ())),
                           preferred_element_type=F32)


def _dot_tn(a, b):
    return lax.dot_general(a.astype(BF16), b.astype(BF16), (((0,), (0,)), ((), ())),
                           preferred_element_type=F32)


def _split3(x):
    x1 = x.astype(BF16)
    r1 = x - x1.astype(F32)
    x2 = r1.astype(BF16)
    x3 = (r1 - x2.astype(F32)).astype(BF16)
    return x1, x2, x3


def _in_proj_kernel(x_ref, g_ref, b_ref, wa_ref, watt_ref, wg_ref, wab_ref, kz_ref, vz_ref,
                    h_ref, ab_ref, q_ref, k_ref, v_ref, xn_ref, *, sub, n_a):
    del kz_ref, vz_ref
    j = pl.program_id(1)
    tm = x_ref.shape[0]

    @pl.when(j == 0)
    def _():
        def body(i, c):
            r = pl.multiple_of(i * sub, sub)
            xn = _layernorm_rows(x_ref[pl.ds(r, sub), :], g_ref[...], b_ref[...])
            xn_ref[pl.ds(r, sub), :] = xn.astype(BF16)
            return c
        lax.fori_loop(0, tm // sub, body, 0)
        ab_ref[...] = _dot_nt(xn_ref[...], wab_ref[...])

    @pl.when(j < n_a)
    def _():
        tn = h_ref.shape[1]
        w_rows = wa_ref[pl.ds(pl.multiple_of(j * tn, tn), tn), :]
        h_ref[...] = _dot_nt(xn_ref[...], w_rows).astype(BF16)

    @pl.when(j == n_a)
    def _():
        for o_ref, c0, mult in ((q_ref, 0, ATT_QSCALE), (k_ref, ATT_W, None), (v_ref, 2 * ATT_W, None)):
            proj = _dot_nt(xn_ref[...], watt_ref[c0:c0 + ATT_W, :])
            o_ref[...] = (proj if mult is None else proj * mult).astype(BF16)

    @pl.when(j == n_a + 1)
    def _():
        h_ref[...] = _dot_nt(xn_ref[...], wg_ref[...]).astype(BF16)


def _in_proj(x2d, ln_g, ln_b, w_a, w_att, w_g, w_ab, *, tm, t, pad_rows):
    rows = x2d.shape[0]
    tn = w_g.shape[0]
    n_a = w_a.shape[0] // tn
    assert rows % tm == 0 and t % tm == 0 and pad_rows % tm == 0
    assert w_a.shape[0] == n_a * tn and w_a.shape[0] + tn == H_MAIN and w_att.shape[0] == 3 * ATT_W
    sub = min(tm, 128)
    per_stream = t // tm
    stride = per_stream + pad_rows // tm
    kv_rows = (rows // t) * (t + pad_rows)
    kv_block = lambda i, j: ((i // per_stream) * stride + pad_rows // tm + i % per_stream, 0)
    zeros = jnp.zeros((kv_rows, ATT_W), BF16)
    const = lambda i, j: (0, 0)
    return pl.pallas_call(
        functools.partial(_in_proj_kernel, sub=sub, n_a=n_a),
        grid=(rows // tm, n_a + 2),
        in_specs=[
            pl.BlockSpec((tm, D_MODEL), lambda i, j: (i, 0)),
            pl.BlockSpec((1, D_MODEL), const),
            pl.BlockSpec((1, D_MODEL), const),
            pl.BlockSpec((n_a * tn, D_MODEL), const, pipeline_mode=pl.Buffered(1)),
            pl.BlockSpec((3 * ATT_W, D_MODEL), const, pipeline_mode=pl.Buffered(1)),
            pl.BlockSpec((tn, D_MODEL), const, pipeline_mode=pl.Buffered(1)),
            pl.BlockSpec((LANES, D_MODEL), const),
            pl.BlockSpec(memory_space=pl.ANY),
            pl.BlockSpec(memory_space=pl.ANY),
        ],
        out_specs=[
            pl.BlockSpec((tm, tn), lambda i, j: (i, jnp.where(j < n_a, j, jnp.maximum(j - 1, n_a - 1)))),
            pl.BlockSpec((tm, LANES), lambda i, j: (i, 0)),
            pl.BlockSpec((tm, ATT_W), lambda i, j: (i, 0)),
            pl.BlockSpec((tm, ATT_W), kv_block),
            pl.BlockSpec((tm, ATT_W), kv_block),
        ],
        out_shape=[
            jax.ShapeDtypeStruct((rows, H_MAIN), BF16),
            jax.ShapeDtypeStruct((rows, LANES), F32),
            jax.ShapeDtypeStruct((rows, ATT_W), BF16),
            jax.ShapeDtypeStruct((kv_rows, ATT_W), BF16),
            jax.ShapeDtypeStruct((kv_rows, ATT_W), BF16),
        ],
        input_output_aliases={7: 3, 8: 4},
        scratch_shapes=[pltpu.VMEM((tm, D_MODEL), BF16)],
        compiler_params=_cparams(("arbitrary", "arbitrary")),
        name="in_proj",
    )(x2d, ln_g, ln_b, w_a, w_att, w_g, w_ab, zeros, zeros)


def _gdn_kernel(qkv_ref, z_ref, ab_ref, cw_ref, alog_ref, dtb_ref, nw_ref, cinit_ref, s0_ref,
                o_ref, cstate_ref, s_ref,
                xp_ref, q_st, k_st, v_st, dec_st, beta_st, dect_st, lm_st, *, c, nb):
    n = pl.program_id(1)
    pad = SUBLANES
    cur = lax.rem(n, 2)
    prv = 1 - cur
    heads = range(GDN_HEADS)
    pairs = [(bi, h) for bi in range(nb) for h in heads]

    @pl.when(n == 0)
    def _():
        xp_ref[:, 0:pad, :] = cinit_ref[...]
        s_ref[...] = s0_ref[...]
        for st in (q_st, k_st, v_st, dec_st, beta_st, dect_st, lm_st):
            st[1] = jnp.zeros(st.shape[1:], F32)

    row = lax.broadcasted_iota(jnp.int32, (c, c), 0)
    col = lax.broadcasted_iota(jnp.int32, (c, c), 1)
    causal = row >= col
    strict = row > col
    eye = jnp.where(row == col, 1.0, 0.0).astype(F32)
    tri = jnp.where(causal, 1.0, 0.0).astype(BF16)

    for bi in range(nb):
        ab = ab_ref[bi]
        sp_in = ab + dtb_ref[...]
        softplus = jnp.maximum(sp_in, 0.0) + jnp.log(1.0 + jnp.exp(-jnp.abs(sp_in)))
        g_all = -jnp.exp(alog_ref[...]) * softplus
        g1, g2, g3 = _split3(g_all)
        dec_all = (jnp.dot(tri, g1, preferred_element_type=F32)
                   + (jnp.dot(tri, g2, preferred_element_type=F32)
                      + jnp.dot(tri, g3, preferred_element_type=F32)))
        dec_t = dec_all.T
        dec_st[cur, bi] = dec_all
        dect_st[cur, bi] = dec_t
        beta_st[cur, bi] = _sigmoid(ab)
        for h in heads:
            lm_st[cur, bi * GDN_HEADS + h] = jnp.exp(
                jnp.where(causal, dec_all[:, h:h + 1] - dec_t[h:h + 1, :], -jnp.inf))

        xp_ref[bi, pad:pad + c, :] = qkv_ref[bi].astype(F32)

    def conv_silu(bi, c0, width):
        acc = xp_ref[bi, pl.ds(pad, c), pl.ds(c0, width)] * cw_ref[GDN_CONV - 1:GDN_CONV, pl.ds(c0, width)]
        for i in range(1, GDN_CONV):
            acc = acc + (xp_ref[bi, pl.ds(pad - i, c), pl.ds(c0, width)]
                         * cw_ref[GDN_CONV - 1 - i:GDN_CONV - i, pl.ds(c0, width)])
        return _silu(acc)

    def q_job(bi, h):
        lo = h * GDN_DK
        xq = conv_silu(bi, lo, GDN_DK)
        q_st[cur, bi, :, lo:lo + GDN_DK] = (
            xq * lax.rsqrt(jnp.sum(xq * xq, axis=-1, keepdims=True) + L2_EPS) * (GDN_DK ** -0.5))

    def k_job(bi, h):
        lo = h * GDN_DK
        xk = conv_silu(bi, GDN_W + lo, GDN_DK)
        k_st[cur, bi, :, lo:lo + GDN_DK] = (
            xk * lax.rsqrt(jnp.sum(xk * xk, axis=-1, keepdims=True) + L2_EPS))

    def v_job(bi, h):
        lo = h * GDN_DV
        v_st[cur, bi, :, lo:lo + GDN_DV] = conv_silu(bi, 2 * GDN_W + lo, GDN_DV)

    jobs = [functools.partial(job, bi, h) for bi, h in pairs for job in (q_job, k_job, v_job)]
    n_stage = 2 * (int(math.log2(c)) - 1) + 4
    per_stage = -(-len(jobs) // n_stage)

    def front_jobs():
        for job in jobs[:per_stage]:
            job()
        del jobs[:per_stage]

    def cols(ref, bi, h, width=GDN_DK):
        return ref[prv, bi, :, h * width:(h + 1) * width]

    qn = [cols(q_st, bi, h) for bi, h in pairs]
    kn = [cols(k_st, bi, h) for bi, h in pairs]
    xv = [cols(v_st, bi, h, GDN_DV) for bi, h in pairs]
    dcol = [dec_st[prv, bi, :, h:h + 1] for bi, h in pairs]
    beta = [beta_st[prv, bi, :, GDN_HEADS + h:GDN_HEADS + h + 1] for bi, h in pairs]
    lmask = [lm_st[prv, bi * GDN_HEADS + h] for bi, h in pairs]
    idx = range(len(pairs))
    kb = [kn[i] * beta[i] for i in idx]
    e_d = [jnp.exp(d) for d in dcol]
    gram = [_dot_nt(jnp.concatenate([kb[i], qn[i]], axis=0), kn[i]) for i in idx]
    front_jobs()
    m = [jnp.where(strict, gram[i][:c] * lmask[i], 0.0) for i in idx]
    qk = [gram[i][c:] * lmask[i] for i in idx]
    p = [eye - m[i] for i in idx]
    mk = m
    for _ in range(int(math.log2(c)) - 1):
        mk = [_dot(x, x) for x in mk]
        front_jobs()
        p = [p[i] + _dot(p[i], mk[i]) for i in idx]
        front_jobs()
    sol = [_dot(p[i], jnp.concatenate([xv[i] * beta[i], kb[i] * e_d[i]], axis=1)) for i in idx]
    front_jobs()
    s_old = [s_ref[bi, h] for bi, h in pairs]
    ws_qs = [_dot(jnp.concatenate([sol[i][:, GDN_DV:], qn[i] * e_d[i]], axis=0), s_old[i])
             for i in idx]
    front_jobs()
    v_new = [sol[i][:, :GDN_DV] - ws_qs[i][:c] for i in idx]
    dl = [d[c - 1:c, :] for d in dcol]
    kd = [kn[i] * jnp.exp(dl[i] - dcol[i]) for i in idx]
    s_upd = [_dot_tn(kd[i], v_new[i]) for i in idx]
    o_in = [_dot(qk[i], v_new[i]) for i in idx]
    front_jobs()
    assert not jobs
    for i, (bi, h) in enumerate(pairs):
        s_ref[bi, h] = s_old[i] * jnp.exp(dl[i]) + s_upd[i]
        o = ws_qs[i][c:] + o_in[i]
        zh = z_ref[bi, :, pl.ds(h * GDN_DV, GDN_DV)].astype(F32)
        o = (o * lax.rsqrt(jnp.mean(o * o, axis=-1, keepdims=True) + RMS_EPS) * nw_ref[...]
             * _silu(zh))
        o_ref[bi, :, pl.ds(h * GDN_DV, GDN_DV)] = o.astype(BF16)
    for bi in range(nb):
        tail = xp_ref[bi, c:c + pad, :]
        xp_ref[bi, 0:pad, :] = tail
        cstate_ref[bi] = tail


def _gdn(h2d, ab, conv_w, a_log, dt_bias, norm_w, conv_init, s0, *, batch, t, nb):
    c = min(CHUNK, t)
    nc = t // c
    assert batch % nb == 0
    h3d = h2d.reshape(batch, t, H_MAIN)
    ab3d = ab.reshape(batch, t, LANES)
    front = lambda b, n: (b, jnp.minimum(n, nc - 1), 0)
    back = lambda b, n: (b, jnp.maximum(n - 1, 0), 0)
    stash = lambda *shape: pltpu.VMEM((2, *shape), F32)
    o_a, cstate, s_new = pl.pallas_call(
        functools.partial(_gdn_kernel, c=c, nb=nb),
        grid=(batch // nb, nc + 1),
        in_specs=[
            pl.BlockSpec((nb, c, GDN_CONV_CH), front),
            pl.BlockSpec((nb, c, GDN_W), lambda b, n: (b, jnp.maximum(n - 1, 0), COL_Z // GDN_W)),
            pl.BlockSpec((nb, c, LANES), front),
            pl.BlockSpec((GDN_CONV, GDN_CONV_CH), lambda b, n: (0, 0)),
            pl.BlockSpec((1, LANES), lambda b, n: (0, 0)),
            pl.BlockSpec((1, LANES), lambda b, n: (0, 0)),
            pl.BlockSpec((1, GDN_DV), lambda b, n: (0, 0)),
            pl.BlockSpec((nb, SUBLANES, GDN_CONV_CH), lambda b, n: (b, 0, 0)),
            pl.BlockSpec((nb, GDN_HEADS, GDN_DK, GDN_DV), lambda b, n: (b, 0, 0, 0)),
        ],
        out_specs=[
            pl.BlockSpec((nb, c, GDN_W), back),
            pl.BlockSpec((nb, SUBLANES, GDN_CONV_CH), lambda b, n: (b, 0, 0)),
            pl.BlockSpec((nb, GDN_HEADS, GDN_DK, GDN_DV), lambda b, n: (b, 0, 0, 0)),
        ],
        out_shape=[
            jax.ShapeDtypeStruct((batch, t, GDN_W), BF16),
            jax.ShapeDtypeStruct((batch, SUBLANES, GDN_CONV_CH), F32),
            jax.ShapeDtypeStruct((batch, GDN_HEADS, GDN_DK, GDN_DV), F32),
        ],
        scratch_shapes=[
            pltpu.VMEM((nb, SUBLANES + c, GDN_CONV_CH), F32),
            stash(nb, c, GDN_W), stash(nb, c, GDN_W), stash(nb, c, GDN_W),
            stash(nb, c, LANES), stash(nb, c, LANES), stash(nb, LANES, c),
            stash(nb * GDN_HEADS, c, c),
        ],
        compiler_params=_cparams(("arbitrary", "arbitrary")),
        name="gdn",
    )(h3d, h3d, ab3d, conv_w, a_log, dt_bias, norm_w, conv_init, s0)
    return o_a.reshape(batch * t, GDN_W), cstate, s_new


def _attn_kernel(q_ref, k_ref, v_ref, bias_ref, o_ref, *, tq, width, pad, koff):
    n = pl.program_id(1)
    start = pl.multiple_of(koff + n * tq, tq)
    heads = range(ATT_HEADS)
    cols = [slice(h * ATT_DH, (h + 1) * ATT_DH) for h in heads]
    has_pad_keys = n * tq < pad

    def tile(mask_pad_keys):
        kwin = k_ref[0, pl.ds(start, width), :]
        vwin = v_ref[0, pl.ds(start, width), :]
        s = [_dot_nt(q_ref[:, cols[h]], kwin[:, cols[h]]) + bias_ref[h] for h in heads]
        if mask_pad_keys:
            kpos = n * tq - pad + lax.broadcasted_iota(jnp.int32, (tq, width), 1)
            s = [jnp.where(kpos >= 0, x, -jnp.inf) for x in s]
        p = [jnp.exp2(x - jnp.max(x, axis=-1, keepdims=True)) for x in s]
        l = [jnp.sum(x, axis=-1, keepdims=True) for x in p]
        pv = [_dot(p[h], vwin[:, cols[h]]) for h in heads]
        o_ref[...] = jnp.concatenate([pv[h] / l[h] for h in heads], axis=1).astype(BF16)

    pl.when(has_pad_keys)(functools.partial(tile, True))
    pl.when(jnp.logical_not(has_pad_keys))(functools.partial(tile, False))


def _attn(q2d, k3d, v3d, bias, *, batch, t, tq, width, pad, koff):
    nq = t // tq
    tk = k3d.shape[1]
    assert koff % tq == 0 and koff + (nq - 1) * tq + width <= tk
    return pl.pallas_call(
        functools.partial(_attn_kernel, tq=tq, width=width, pad=pad, koff=koff),
        grid=(batch, nq),
        in_specs=[
            pl.BlockSpec((tq, ATT_W), lambda b, n: (b * nq + n, 0)),
            pl.BlockSpec((1, tk, ATT_W), lambda b, n: (b, 0, 0)),
            pl.BlockSpec((1, tk, ATT_W), lambda b, n: (b, 0, 0)),
            pl.BlockSpec((ATT_HEADS, tq, width), lambda b, n: (0, 0, 0)),
        ],
        out_specs=pl.BlockSpec((tq, ATT_W), lambda b, n: (b * nq + n, 0)),
        out_shape=jax.ShapeDtypeStruct((batch * t, ATT_W), BF16),
        compiler_params=_cparams(("arbitrary", "arbitrary")),
        name="attn",
    )(q2d, k3d, v3d, bias)


def _attn_step_kernel(q_ref, kn_ref, vn_ref, kt_ref, vt_ref, bc_ref, bn_ref, o_ref):
    heads = range(ATT_HEADS)
    cols = [slice(h * ATT_DH, (h + 1) * ATT_DH) for h in heads]
    q = [q_ref[:, cols[h]] for h in heads]
    s_c = [_dot(q[h], kt_ref[0, h]) + bc_ref[h] for h in heads]
    s_n = [_dot_nt(q[h], kn_ref[:, cols[h]]) + bn_ref[h] for h in heads]
    mx = [jnp.maximum(jnp.max(s_c[h], axis=-1, keepdims=True), jnp.max(s_n[h], axis=-1, keepdims=True))
          for h in heads]
    p_c = [jnp.exp2(s_c[h] - mx[h]) for h in heads]
    p_n = [jnp.exp2(s_n[h] - mx[h]) for h in heads]
    l = [jnp.sum(p_c[h], axis=-1, keepdims=True) + jnp.sum(p_n[h], axis=-1, keepdims=True) for h in heads]
    pv = [_dot_nt(p_c[h], vt_ref[0, h]) + _dot(p_n[h], vn_ref[:, cols[h]]) for h in heads]
    o_ref[...] = jnp.concatenate([pv[h] / l[h] for h in heads], axis=1).astype(BF16)


def _attn_step(q2d, kn2d, vn2d, kt, vt, bias, *, batch, t):
    n_past = kt.shape[-1]
    rows = lambda b: (b, 0)
    cache = lambda b: (b, 0, 0, 0)
    const = lambda b: (0, 0, 0)
    return pl.pallas_call(
        _attn_step_kernel,
        grid=(batch,),
        in_specs=[
            pl.BlockSpec((t, ATT_W), rows),
            pl.BlockSpec((t, ATT_W), rows),
            pl.BlockSpec((t, ATT_W), rows),
            pl.BlockSpec((1, ATT_HEADS, ATT_DH, n_past), cache),
            pl.BlockSpec((1, ATT_HEADS, ATT_DH, n_past), cache),
            pl.BlockSpec((ATT_HEADS, t, n_past), const),
            pl.BlockSpec((ATT_HEADS, t, t), const),
        ],
        out_specs=pl.BlockSpec((t, ATT_W), rows),
        out_shape=jax.ShapeDtypeStruct((batch * t, ATT_W), BF16),
        compiler_params=_cparams(("arbitrary",)),
        name="attn_step",
    )(q2d, kn2d, vn2d, kt, vt, bias[:, :, :n_past], bias[:, :, n_past:])


def _merge_kernel(x_ref, g_ref, b_ref, oa_ref, ob_ref, ga_ref, gb_ref, wa_ref, wb_ref, wo_ref,
                  g1_ref, b1_ref, x1_ref, *, parts):
    sub = x_ref.shape[0] // parts
    for s in range(parts):
        r = pl.ds(s * sub, sub)
        ya = jnp.dot(oa_ref[r, :], wa_ref[...], preferred_element_type=F32)
        yb = jnp.dot(ob_ref[r, :], wb_ref[...], preferred_element_type=F32)
        mixed = (_sigmoid(ga_ref[r, :].astype(F32)) * ya + _sigmoid(gb_ref[r, :].astype(F32)) * yb)
        xn = _layernorm_rows(x_ref[r, :], g_ref[...], b_ref[...])
        y = DEEPNORM_ALPHA * xn + _dot(mixed, wo_ref[...])
        x1_ref[r, :] = _layernorm_rows(y, g1_ref[...], b1_ref[...])


def _merge(x2d, ln_g, ln_b, oa, ob, h2d, wa, wb, wo, ln1_g, ln1_b, *, tm):
    rows = x2d.shape[0]
    full = lambda i: (0, 0)
    return pl.pallas_call(
        functools.partial(_merge_kernel, parts=2 if tm >= 512 else 1),
        grid=(rows // tm,),
        in_specs=[
            pl.BlockSpec((tm, D_MODEL), lambda i: (i, 0)),
            pl.BlockSpec((1, D_MODEL), full),
            pl.BlockSpec((1, D_MODEL), full),
            pl.BlockSpec((tm, GDN_W), lambda i: (i, 0)),
            pl.BlockSpec((tm, ATT_W), lambda i: (i, 0)),
            pl.BlockSpec((tm, D_MODEL), lambda i: (i, COL_GA // D_MODEL)),
            pl.BlockSpec((tm, D_MODEL), lambda i: (i, COL_GB // D_MODEL)),
            pl.BlockSpec((GDN_W, D_MODEL), full),
            pl.BlockSpec((ATT_W, D_MODEL), full),
            pl.BlockSpec((D_MODEL, D_MODEL), full),
            pl.BlockSpec((1, D_MODEL), full),
            pl.BlockSpec((1, D_MODEL), full),
        ],
        out_specs=pl.BlockSpec((tm, D_MODEL), lambda i: (i, 0)),
        out_shape=jax.ShapeDtypeStruct((rows, D_MODEL), F32),
        compiler_params=_cparams(("arbitrary",)),
        name="merge",
    )(x2d, ln_g, ln_b, oa, ob, h2d, h2d, wa, wb, wo, ln1_g, ln1_b)


def _gelu_tanh(x):
    return 0.5 * x * (1.0 + jnp.tanh(math.sqrt(2.0 / math.pi) * (x + 0.044715 * (x * x * x))))


def _ffn_kernel(x1_ref, pe_ref, wup_ref, cw_ref, wdn_ref, wple_ref, wpg_ref, g2_ref, b2_ref,
                finit_ref, y_ref, fstate_ref, gp_ref, acc_ref, act_ref, *, tm, fc, ns, down_group):
    n = pl.program_id(1)
    pad = SUBLANES
    streams = range(ns)

    @pl.when(n == 0)
    def _():
        gp_ref[:, 0:pad, :] = finit_ref[...]

    x1 = x1_ref[...]
    x1b = x1.astype(BF16)

    def up(k):
        c0 = k * fc
        gate = jnp.dot(x1b, wup_ref[:, c0:c0 + fc], preferred_element_type=F32)
        val = jnp.dot(x1b, wup_ref[:, D_FF + c0:D_FF + c0 + fc], preferred_element_type=F32)
        for s in streams:
            gp_ref[s, pad:pad + tm, c0:c0 + fc] = gate[s * tm:(s + 1) * tm]
        return gate, val

    nk = D_FF // fc
    nxt = up(0)
    ple = jnp.dot(pe_ref[...].astype(BF16), wple_ref[...], preferred_element_type=F32)
    for k in range(nk):
        c0 = k * fc
        gate, val = nxt
        if k + 1 < nk:
            nxt = up(k + 1)
        conv = gate * cw_ref[FFN_CONV - 1:FFN_CONV, c0:c0 + fc]
        for i in range(1, FFN_CONV):
            prev = [gp_ref[s, pl.ds(pad - i, tm), c0:c0 + fc] for s in streams]
            conv = conv + (prev[0] if ns == 1 else jnp.concatenate(prev, axis=0)) * cw_ref[
                FFN_CONV - 1 - i:FFN_CONV - i, c0:c0 + fc]
        act_ref[:, c0:c0 + fc] = (_gelu_tanh(conv) * val).astype(BF16)
        if (k + 1) % down_group == 0 or k + 1 == nk:
            g0 = (k // down_group) * down_group * fc
            down = jnp.dot(act_ref[:, g0:c0 + fc], wdn_ref[g0:c0 + fc, :], preferred_element_type=F32)
            if g0 == 0:
                acc_ref[...] = down
            else:
                acc_ref[...] += down
    tail = gp_ref[:, tm:tm + pad, :]
    gp_ref[:, 0:pad, :] = tail
    fstate_ref[...] = tail

    r = DEEPNORM_ALPHA * x1 + acc_ref[...]
    gate_p = _sigmoid(_dot(r, wpg_ref[...]))
    r = r + gate_p * ple
    y_ref[...] = _layernorm_rows(r, g2_ref[...], b2_ref[...])


def _ffn(x1, pe, wup, conv_w, wdn, wple, wpg, ln2_g, ln2_b, finit, *, batch, t, tm, fc, ns):
    nt = t // tm
    assert batch % ns == 0 and (ns == 1 or nt == 1)
    rows = batch * t
    tile = ns * tm
    full = lambda b, n: (0, 0)
    return pl.pallas_call(
        functools.partial(_ffn_kernel, tm=tm, fc=fc, ns=ns, down_group=FFN_DOWN_GROUP),
        grid=(batch // ns, nt),
        in_specs=[
            pl.BlockSpec((tile, D_MODEL), lambda b, n: (b * nt + n, 0)),
            pl.BlockSpec((tile, PLE_DIM), lambda b, n: (b * nt + n, 0)),
            pl.BlockSpec((D_MODEL, 2 * D_FF), full, pipeline_mode=pl.Buffered(1)),
            pl.BlockSpec((FFN_CONV, D_FF), full),
            pl.BlockSpec((D_FF, D_MODEL), full, pipeline_mode=pl.Buffered(1)),
            pl.BlockSpec((PLE_DIM, D_MODEL), full),
            pl.BlockSpec((D_MODEL, D_MODEL), full),
            pl.BlockSpec((1, D_MODEL), full),
            pl.BlockSpec((1, D_MODEL), full),
            pl.BlockSpec((ns, SUBLANES, D_FF), lambda b, n: (b, 0, 0)),
        ],
        out_specs=[
            pl.BlockSpec((tile, D_MODEL), lambda b, n: (b * nt + n, 0)),
            pl.BlockSpec((ns, SUBLANES, D_FF), lambda b, n: (b, 0, 0)),
        ],
        out_shape=[
            jax.ShapeDtypeStruct((rows, D_MODEL), F32),
            jax.ShapeDtypeStruct((batch, SUBLANES, D_FF), F32),
        ],
        scratch_shapes=[pltpu.VMEM((ns, SUBLANES + tm, D_FF), F32), pltpu.VMEM((tile, D_MODEL), F32),
                        pltpu.VMEM((tile, D_FF), BF16)],
        compiler_params=_cparams(("arbitrary", "arbitrary")),
        name="ffn",
    )(x1, pe, wup, conv_w, wdn, wple, wpg, ln2_g, ln2_b, finit)


def _rel_bias(table, n_q, n_past, n_k):
    span = n_q + n_k - 1
    rel = n_q - 1 + n_past - jnp.arange(span)
    rev = table[:, jnp.clip(rel, -REL_CLIP, REL_CLIP) + REL_CLIP].astype(F32)
    heads = table.shape[0]
    period = jnp.pad(rev, ((0, 0), (0, 1)))
    skew = jnp.tile(period, (1, n_q))[:, :n_q * span].reshape(heads, n_q, span)
    return skew[:, :, n_q - 1:n_q - 1 + n_k]


def _pad_state(state, rows):
    b, r, ch = state.shape
    return jnp.concatenate([jnp.zeros((b, rows - r, ch), F32), state.astype(F32)], axis=1)


def _layer(x, pe, gconv_prev, s_gdn, k_prev, v_prev, fconv_prev, wts, *, tm_proj, tm_merge, tm_ffn, gdn_nb,
           att_chunks=1, ffn_ns=1):
    (ln_in_g, ln_in_b, w_proj, gdn_conv_w, a_log, dt_bias, norm_w, rel_table, w_a, w_b, w_o,
     ln1_g, ln1_b, w_up, ffn_conv_w, w_dn, w_ple, w_pg, ln2_g, ln2_b) = wts
    b, t, _ = x.shape
    rows = b * t
    x2d = x.reshape(rows, D_MODEL)

    prompt = k_prev is None
    tm_proj = min(tm_proj, t if prompt else rows)
    pad_rows = tm_proj if prompt else 0
    assert pad_rows >= BAND or not prompt
    h2d, ab, q_b, k_b, v_b = _in_proj(x2d, ln_in_g, ln_in_b, *w_proj, tm=tm_proj,
                                      t=t if prompt else rows, pad_rows=pad_rows)

    o_a, gconv_new, s_new = _gdn(h2d, ab, gdn_conv_w, a_log, dt_bias, norm_w,
                                 _pad_state(gconv_prev, SUBLANES), s_gdn.astype(F32), batch=b, t=t,
                                 nb=math.gcd(gdn_nb, b))

    k_b = k_b.reshape(b, -1, ATT_W)
    v_b = v_b.reshape(b, -1, ATT_W)
    if prompt:
        tq = att_chunks * CHUNK
        width = BAND + tq
        own = (jnp.arange(tq) // CHUNK)[:, None] * CHUNK
        key = jnp.arange(width)[None, :]
        in_band = (key >= own) & (key < own + BAND + CHUNK)
        bias = jnp.where(in_band, _rel_bias(rel_table, tq, BAND, width) * LOG2E, -jnp.inf)
        o_b = _attn(q_b, k_b, v_b, bias, batch=b, t=t, tq=tq, width=width, pad=BAND,
                    koff=pad_rows - BAND)
        keep = min(BAND, t)
        k_new, v_new = k_b[:, pad_rows + t - keep:], v_b[:, pad_rows + t - keep:]
    else:
        n_past = k_prev.shape[1]
        bias = _rel_bias(rel_table, t, n_past, n_past + t) * LOG2E
        o_b = _attn_step(q_b, k_b.reshape(rows, ATT_W), v_b.reshape(rows, ATT_W),
                         jnp.transpose(k_prev, (0, 2, 3, 1)).astype(F32),
                         jnp.transpose(v_prev, (0, 2, 3, 1)).astype(F32), bias, batch=b, t=t)
        k_new, v_new = k_b, v_b

    x1 = _merge(x2d, ln_in_g, ln_in_b, o_a, o_b, h2d, w_a, w_b, w_o, ln1_g, ln1_b,
                tm=min(tm_merge, rows))

    y, fconv_new = _ffn(x1, pe.reshape(rows, PLE_DIM), w_up, ffn_conv_w, w_dn, w_ple, w_pg,
                        ln2_g, ln2_b, _pad_state(fconv_prev, SUBLANES), batch=b, t=t,
                        tm=min(tm_ffn, t), fc=256, ns=math.gcd(ffn_ns, b))

    keep_k = k_new.shape[1]
    return (y.reshape(b, t, D_MODEL),
            gconv_new[:, SUBLANES - (GDN_CONV - 1):],
            s_new,
            k_new.astype(F32).reshape(b, keep_k, ATT_HEADS, ATT_DH),
            v_new.astype(F32).reshape(b, keep_k, ATT_HEADS, ATT_DH),
            fconv_new[:, SUBLANES - (FFN_CONV - 1):])


def _pad_lanes(v, width=LANES):
    return jnp.pad(v.astype(F32), (0, width - v.shape[0])).reshape(1, width)


def kernel(x_prompt, x_sample, state_gdn_conv, state_gdn, cache_attn_k, cache_attn_v, state_ffn_conv,
           p_prompt, p_sample, ln_in_g, ln_in_b, w_in, gdn_conv_w, gdn_a_log, gdn_dt_bias, gdn_norm_w,
           att_rel_bias, w_branch_a, w_branch_b, w_out, ln1_g, ln1_b, w_ffn_up, ffn_conv_w, w_ffn_down,
           w_ple, w_ple_gate, ln2_g, ln2_b):
    depth = w_in.shape[0]
    assert depth == 1
    bp = x_prompt.shape[0]
    row = lambda v: v.reshape(1, -1).astype(F32)

    xp, xs = x_prompt, x_sample
    new_p, new_s = [], []
    for i in range(depth):
        wi = w_in[i]
        o_ab = 4 * GDN_W
        o_att = o_ab + 2 * GDN_HEADS
        o_gate = o_att + 3 * ATT_W
        wt = wi.T.astype(BF16)
        w_proj = (wt[:o_ab], wt[o_att:o_gate], wt[o_gate:],
                  jnp.pad(wt[o_ab:o_att], ((0, LANES - 2 * GDN_HEADS), (0, 0))))
        wts = (row(ln_in_g), row(ln_in_b), w_proj, gdn_conv_w[i].astype(F32),
               _pad_lanes(gdn_a_log[i]), _pad_lanes(gdn_dt_bias[i]), row(gdn_norm_w[i]), att_rel_bias[i],
               w_branch_a[i].astype(BF16), w_branch_b[i].astype(BF16), w_out[i].astype(BF16),
               row(ln1_g[i]), row(ln1_b[i]), w_ffn_up[i].astype(BF16), ffn_conv_w[i].astype(F32),
               w_ffn_down[i].astype(BF16), w_ple[i].astype(BF16), w_ple_gate[i].astype(BF16),
               row(ln2_g[i]), row(ln2_b[i]))
        xp, st_p = _unpack(_layer(xp, p_prompt[i],
                                  jnp.zeros((bp, GDN_CONV - 1, GDN_CONV_CH), F32),
                                  jnp.zeros((bp, GDN_HEADS, GDN_DK, GDN_DV), F32),
                                  None, None,
                                  jnp.zeros((bp, FFN_CONV - 1, D_FF), F32), wts,
                                  tm_proj=1024, tm_merge=1024, tm_ffn=512, gdn_nb=4, att_chunks=4))
        xs, st_s = _unpack(_layer(xs, p_sample[i], state_gdn_conv[i], state_gdn[i], cache_attn_k[i],
                                  cache_attn_v[i], state_ffn_conv[i], wts,
                                  tm_proj=256, tm_merge=256, tm_ffn=256, gdn_nb=4, ffn_ns=16))
        new_p.append(st_p)
        new_s.append(st_s)
    p_gconv, p_gdn, p_k, p_v, p_fconv = [jnp.stack(a) for a in zip(*new_p)]
    s_gconv, s_gdn, s_k, s_v, s_fconv = [jnp.stack(a) for a in zip(*new_s)]
    return (xp, xs, p_gconv, p_gdn, p_k, p_v, p_fconv, s_gconv, s_gdn, s_k, s_v, s_fconv)


def _unpack(res):
    return res[0], res[1:]
```

```python
import functools
import math

import jax
import jax.numpy as jnp
from jax import lax
from jax.experimental import pallas as pl
from jax.experimental.pallas import tpu as pltpu

F32 = jnp.float32
BF16 = jnp.bfloat16

D_MODEL = 1024
CHUNK = 64
BAND = 512
GDN_HEADS = 8
GDN_DK = 128
GDN_DV = 128
GDN_CONV = 4
ATT_HEADS = 8
ATT_DH = 64
REL_CLIP = 128
D_FF = 2816
FFN_CONV = 3
PLE_DIM = 256
DEEPNORM_ALPHA = 2.0 ** 0.25
LN_EPS = 1e-5
RMS_EPS = 1e-6
L2_EPS = 1e-6

GDN_W = GDN_HEADS * GDN_DK
GDN_CONV_CH = 3 * GDN_W
ATT_W = ATT_HEADS * ATT_DH
LANES = 128
SUBLANES = 8

H_SECTIONS = ("q", "gate_a", "k", "gate_b", "v", "z")
COL = {name: idx * GDN_W for idx, name in enumerate(H_SECTIONS)}
CONV_COL = {"q": 0, "k": GDN_W, "v": 2 * GDN_W}
H_MAIN = len(H_SECTIONS) * GDN_W

LOG2E = math.log2(math.e)
ATT_QSCALE = ATT_DH ** -0.5 * LOG2E

FFN_DOWN_GROUP = 11

VMEM_LIMIT = 56 * 1024 * 1024


def _cparams(sem):
    return pltpu.CompilerParams(dimension_semantics=sem, vmem_limit_bytes=VMEM_LIMIT)


def _sigmoid(x):
    return 1.0 / (1.0 + jnp.exp(-x))


def _silu(x):
    return x * _sigmoid(x)


def _layernorm_rows(x, g, b):
    mu = jnp.mean(x, axis=-1, keepdims=True)
    xc = x - mu
    var = jnp.mean(xc * xc, axis=-1, keepdims=True)
    return xc * lax.rsqrt(var + LN_EPS) * g + b


def _dot(a, b):
    return jnp.dot(a.astype(BF16), b.astype(BF16), preferred_element_type=F32)


def _dot_nt(a, b):
    return lax.dot_general(a.astype(BF16), b.astype(BF16), (((1,), (1,)), ((), ())),
                           preferred_element_type=F32)


def _dot_tn(a, b):
    return lax.dot_general(a.astype(BF16), b.astype(BF16), (((0,), (0,)), ((), ())),
                           preferred_element_type=F32)


def _split3(x):
    x1 = x.astype(BF16)
    r1 = x - x1.astype(F32)
    x2 = r1.astype(BF16)
    x3 = (r1 - x2.astype(F32)).astype(BF16)
    return x1, x2, x3


CONV_COLS = 256


def _in_proj_kernel(x_ref, g_ref, b_ref, wm_ref, watt_ref, wab_ref, cw_ref, cinit_ref,
                    kz_ref, vz_ref,
                    h_ref, ab_ref, q_ref, k_ref, v_ref, cs_ref,
                    xn_ref, carry_ref, xp0_ref, xp1_ref, *, sub, n_main, ns, first_tile_every):
    del kz_ref, vz_ref
    i = pl.program_id(0)
    j = pl.program_id(1)
    tm = x_ref.shape[0]
    tn = h_ref.shape[1]
    tr = tm // ns
    pad = SUBLANES
    streams = range(ns)

    @pl.when(j == 0)
    def _():
        def body(r_i, c):
            r = pl.multiple_of(r_i * sub, sub)
            xn = _layernorm_rows(x_ref[pl.ds(r, sub), :], g_ref[...], b_ref[...])
            xn_ref[pl.ds(r, sub), :] = xn.astype(BF16)
            return c
        lax.fori_loop(0, tm // sub, body, 0)
        ab_ref[...] = _dot_nt(xn_ref[...], wab_ref[...])

    def project(g0):
        return _dot_nt(xn_ref[...], wm_ref[g0:g0 + CONV_COLS, :])

    def conv_silu(pre, cc, buf):
        outs = []
        for s in streams:
            buf[s, 0:pad, :] = carry_ref[s, :, cc:cc + CONV_COLS]
            buf[s, pad:pad + tr, :] = pre[s * tr:(s + 1) * tr]
        for s in streams:
            acc = buf[s, pad:pad + tr, :] * cw_ref[GDN_CONV - 1:GDN_CONV, cc:cc + CONV_COLS]
            for t in range(1, GDN_CONV):
                acc = acc + (buf[s, pl.ds(pad - t, tr), :]
                             * cw_ref[GDN_CONV - 1 - t:GDN_CONV - t, cc:cc + CONV_COLS])
            outs.append(acc)
            carry_ref[s, :, cc:cc + CONV_COLS] = buf[s, tr:tr + pad, :]
        return _silu(outs[0] if ns == 1 else jnp.concatenate(outs, axis=0))

    def l2norm_heads(y, mult):
        parts = []
        for lo in range(0, CONV_COLS, GDN_DK):
            yh = y[:, lo:lo + GDN_DK]
            parts.append(yh * (lax.rsqrt(jnp.sum(yh * yh, axis=-1, keepdims=True) + L2_EPS) * mult))
        return jnp.concatenate(parts, axis=1)

    def main_step(step):
        base = step * tn
        kinds = [H_SECTIONS[(base + c0) // GDN_W] for c0 in range(0, tn, GDN_W)]
        convs = [k for k in kinds if k in CONV_COL]

        @pl.when(lax.rem(i, first_tile_every) == 0)
        def _():
            for k in convs:
                cc = CONV_COL[k]
                carry_ref[:, :, cc:cc + GDN_W] = cinit_ref[:, :, cc:cc + GDN_W]

        order = [sec * GDN_W + c for c in range(0, GDN_W, CONV_COLS) for sec in range(tn // GDN_W)]
        nxt = project(base + order[0])
        for pos, c0 in enumerate(order):
            g0 = base + c0
            kind = H_SECTIONS[g0 // GDN_W]
            pre = nxt
            if pos + 1 < len(order):
                nxt = project(base + order[pos + 1])
            if kind in CONV_COL:
                y = conv_silu(pre, CONV_COL[kind] + g0 % GDN_W, xp0_ref if pos % 2 == 0 else xp1_ref)
                if kind == "q":
                    y = l2norm_heads(y, GDN_DK ** -0.5)
                elif kind == "k":
                    y = l2norm_heads(y, 1.0)
            elif kind == "z":
                y = _silu(pre)
            else:
                y = pre
            h_ref[:, c0:c0 + CONV_COLS] = y.astype(BF16)
        for k in convs:
            cc = CONV_COL[k]
            cs_ref[:, :, cc:cc + GDN_W] = carry_ref[:, :, cc:cc + GDN_W]

    for step in range(n_main):
        pl.when(j == step)(functools.partial(main_step, step))

    @pl.when(j == n_main)
    def _():
        for o_ref, c0, mult in ((q_ref, 0, ATT_QSCALE), (k_ref, ATT_W, None), (v_ref, 2 * ATT_W, None)):
            proj = _dot_nt(xn_ref[...], watt_ref[c0:c0 + ATT_W, :])
            o_ref[...] = (proj if mult is None else proj * mult).astype(BF16)


def _in_proj(x2d, ln_g, ln_b, w_main, w_att, w_ab, conv_w, conv_init, *, tm, t, pad_rows, ns):
    rows = x2d.shape[0]
    tn = 2 * GDN_W
    n_main = H_MAIN // tn
    n_streams = conv_init.shape[0]
    assert rows % tm == 0 and t % tm == 0 and pad_rows % tm == 0
    assert w_main.shape[0] == H_MAIN and w_att.shape[0] == 3 * ATT_W
    assert (ns == 1 and rows == n_streams * t) or (rows == n_streams * (tm // ns) and rows == tm)
    sub = min(tm, 128)
    per_stream = t // tm
    stride = per_stream + pad_rows // tm
    kv_rows = (rows // t) * (t + pad_rows)
    kv_block = lambda i, j: ((i // per_stream) * stride + pad_rows // tm + i % per_stream, 0)
    stream_block = lambda i, j: (i // per_stream, 0, 0)
    zeros = jnp.zeros((kv_rows, ATT_W), BF16)
    const = lambda i, j: (0, 0)
    return pl.pallas_call(
        functools.partial(_in_proj_kernel, sub=sub, n_main=n_main, ns=ns, first_tile_every=per_stream),
        grid=(rows // tm, n_main + 1),
        in_specs=[
            pl.BlockSpec((tm, D_MODEL), lambda i, j: (i, 0)),
            pl.BlockSpec((1, D_MODEL), const),
            pl.BlockSpec((1, D_MODEL), const),
            pl.BlockSpec((H_MAIN, D_MODEL), const, pipeline_mode=pl.Buffered(1)),
            pl.BlockSpec((3 * ATT_W, D_MODEL), const, pipeline_mode=pl.Buffered(1)),
            pl.BlockSpec((LANES, D_MODEL), const),
            pl.BlockSpec((GDN_CONV, GDN_CONV_CH), const),
            pl.BlockSpec((ns, SUBLANES, GDN_CONV_CH), stream_block),
            pl.BlockSpec(memory_space=pl.ANY),
            pl.BlockSpec(memory_space=pl.ANY),
        ],
        out_specs=[
            pl.BlockSpec((tm, tn), lambda i, j: (i, jnp.minimum(j, n_main - 1))),
            pl.BlockSpec((tm, LANES), lambda i, j: (i, 0)),
            pl.BlockSpec((tm, ATT_W), lambda i, j: (i, 0)),
            pl.BlockSpec((tm, ATT_W), kv_block),
            pl.BlockSpec((tm, ATT_W), kv_block),
            pl.BlockSpec((ns, SUBLANES, GDN_CONV_CH), stream_block),
        ],
        out_shape=[
            jax.ShapeDtypeStruct((rows, H_MAIN), BF16),
            jax.ShapeDtypeStruct((rows, LANES), F32),
            jax.ShapeDtypeStruct((rows, ATT_W), BF16),
            jax.ShapeDtypeStruct((kv_rows, ATT_W), BF16),
            jax.ShapeDtypeStruct((kv_rows, ATT_W), BF16),
            jax.ShapeDtypeStruct((n_streams, SUBLANES, GDN_CONV_CH), F32),
        ],
        input_output_aliases={8: 3, 9: 4},
        scratch_shapes=[
            pltpu.VMEM((tm, D_MODEL), BF16),
            pltpu.VMEM((ns, SUBLANES, GDN_CONV_CH), F32),
            pltpu.VMEM((ns, SUBLANES + tm // ns, CONV_COLS), F32),
            pltpu.VMEM((ns, SUBLANES + tm // ns, CONV_COLS), F32),
        ],
        compiler_params=_cparams(("arbitrary", "arbitrary")),
        name="in_proj",
    )(x2d, ln_g, ln_b, w_main, w_att, w_ab, conv_w, conv_init, zeros, zeros)


def _gdn_kernel(q_ref, k_ref, v_ref, z_ref, ab_ref, alog_ref, dtb_ref, nw_ref, s0_ref,
                o_ref, s_ref, *, c, nb):
    n = pl.program_id(1)
    heads = range(GDN_HEADS)
    pairs = [(bi, h) for bi in range(nb) for h in heads]

    @pl.when(n == 0)
    def _():
        s_ref[...] = s0_ref[...]

    row = lax.broadcasted_iota(jnp.int32, (c, c), 0)
    col = lax.broadcasted_iota(jnp.int32, (c, c), 1)
    causal = row >= col
    strict = row > col
    eye = jnp.where(row == col, 1.0, 0.0).astype(F32)
    tri = jnp.where(causal, 1.0, 0.0).astype(BF16)

    dec_all, dec_t, beta_all = [], [], []
    for bi in range(nb):
        ab = ab_ref[bi]
        sp_in = ab + dtb_ref[...]
        softplus = jnp.maximum(sp_in, 0.0) + jnp.log(1.0 + jnp.exp(-jnp.abs(sp_in)))
        g_all = -jnp.exp(alog_ref[...]) * softplus
        g1, g2, g3 = _split3(g_all)
        dec = (jnp.dot(tri, g1, preferred_element_type=F32)
               + (jnp.dot(tri, g2, preferred_element_type=F32)
                  + jnp.dot(tri, g3, preferred_element_type=F32)))
        dec_all.append(dec)
        dec_t.append(dec.T)
        beta_all.append(_sigmoid(ab))

    def cols(ref, bi, h, width=GDN_DK):
        return ref[bi, :, h * width:(h + 1) * width].astype(F32)

    qn = [cols(q_ref, bi, h) for bi, h in pairs]
    kn = [cols(k_ref, bi, h) for bi, h in pairs]
    xv = [cols(v_ref, bi, h, GDN_DV) for bi, h in pairs]
    dcol = [dec_all[bi][:, h:h + 1] for bi, h in pairs]
    beta = [beta_all[bi][:, GDN_HEADS + h:GDN_HEADS + h + 1] for bi, h in pairs]
    lmask = [jnp.exp(jnp.where(causal, dec_all[bi][:, h:h + 1] - dec_t[bi][h:h + 1, :], -jnp.inf))
             for bi, h in pairs]
    idx = range(len(pairs))
    kb = [kn[i] * beta[i] for i in idx]
    e_d = [jnp.exp(d) for d in dcol]
    gram = [_dot_nt(jnp.concatenate([kb[i], qn[i]], axis=0), kn[i]) for i in idx]
    m = [jnp.where(strict, gram[i][:c] * lmask[i], 0.0) for i in idx]
    qk = [gram[i][c:] * lmask[i] for i in idx]
    p = [eye - m[i] for i in idx]
    mk = m
    for _ in range(int(math.log2(c)) - 1):
        mk = [_dot(x, x) for x in mk]
        p = [p[i] + _dot(p[i], mk[i]) for i in idx]
    sol = [_dot(p[i], jnp.concatenate([xv[i] * beta[i], kb[i] * e_d[i]], axis=1)) for i in idx]
    s_old = [s_ref[bi, h] for bi, h in pairs]
    ws_qs = [_dot(jnp.concatenate([sol[i][:, GDN_DV:], qn[i] * e_d[i]], axis=0), s_old[i])
             for i in idx]
    v_new = [sol[i][:, :GDN_DV] - ws_qs[i][:c] for i in idx]
    dl = [d[c - 1:c, :] for d in dcol]
    kd = [kn[i] * jnp.exp(dl[i] - dcol[i]) for i in idx]
    s_upd = [_dot_tn(kd[i], v_new[i]) for i in idx]
    o_in = [_dot(qk[i], v_new[i]) for i in idx]
    for i, (bi, h) in enumerate(pairs):
        s_ref[bi, h] = s_old[i] * jnp.exp(dl[i]) + s_upd[i]
        o = ws_qs[i][c:] + o_in[i]
        o = (o * lax.rsqrt(jnp.mean(o * o, axis=-1, keepdims=True) + RMS_EPS) * nw_ref[...]
             * cols(z_ref, bi, h, GDN_DV))
        o_ref[bi, :, pl.ds(h * GDN_DV, GDN_DV)] = o.astype(BF16)


def _gdn(h2d, ab, a_log, dt_bias, norm_w, s0, *, batch, t, nb):
    c = min(CHUNK, t)
    nc = t // c
    assert batch % nb == 0
    h3d = h2d.reshape(batch, t, H_MAIN)
    ab3d = ab.reshape(batch, t, LANES)
    chunk = lambda col: (lambda b, n: (b, n, col))
    o_a, s_new = pl.pallas_call(
        functools.partial(_gdn_kernel, c=c, nb=nb),
        grid=(batch // nb, nc),
        in_specs=[
            pl.BlockSpec((nb, c, GDN_W), chunk(COL["q"] // GDN_W)),
            pl.BlockSpec((nb, c, GDN_W), chunk(COL["k"] // GDN_W)),
            pl.BlockSpec((nb, c, GDN_W), chunk(COL["v"] // GDN_W)),
            pl.BlockSpec((nb, c, GDN_W), chunk(COL["z"] // GDN_W)),
            pl.BlockSpec((nb, c, LANES), chunk(0)),
            pl.BlockSpec((1, LANES), lambda b, n: (0, 0)),
            pl.BlockSpec((1, LANES), lambda b, n: (0, 0)),
            pl.BlockSpec((1, GDN_DV), lambda b, n: (0, 0)),
            pl.BlockSpec((nb, GDN_HEADS, GDN_DK, GDN_DV), lambda b, n: (b, 0, 0, 0)),
        ],
        out_specs=[
            pl.BlockSpec((nb, c, GDN_W), chunk(0)),
            pl.BlockSpec((nb, GDN_HEADS, GDN_DK, GDN_DV), lambda b, n: (b, 0, 0, 0)),
        ],
        out_shape=[
            jax.ShapeDtypeStruct((batch, t, GDN_W), BF16),
            jax.ShapeDtypeStruct((batch, GDN_HEADS, GDN_DK, GDN_DV), F32),
        ],
        compiler_params=_cparams(("arbitrary", "arbitrary")),
        name="gdn",
    )(h3d, h3d, h3d, h3d, ab3d, a_log, dt_bias, norm_w, s0)
    return o_a.reshape(batch * t, GDN_W), s_new


def _attn_kernel(q_ref, k_ref, v_ref, bias_ref, o_ref, *, tq, width, pad, koff):
    n = pl.program_id(1)
    start = pl.multiple_of(koff + n * tq, tq)
    heads = range(ATT_HEADS)
    cols = [slice(h * ATT_DH, (h + 1) * ATT_DH) for h in heads]
    has_pad_keys = n * tq < pad

    def tile(mask_pad_keys):
        kwin = k_ref[0, pl.ds(start, width), :]
        vwin = v_ref[0, pl.ds(start, width), :]
        s = [_dot_nt(q_ref[:, cols[h]], kwin[:, cols[h]]) + bias_ref[h] for h in heads]
        if mask_pad_keys:
            kpos = n * tq - pad + lax.broadcasted_iota(jnp.int32, (tq, width), 1)
            s = [jnp.where(kpos >= 0, x, -jnp.inf) for x in s]
        p = [jnp.exp2(x - jnp.max(x, axis=-1, keepdims=True)) for x in s]
        l = [jnp.sum(x, axis=-1, keepdims=True) for x in p]
        pv = [_dot(p[h], vwin[:, cols[h]]) for h in heads]
        o_ref[...] = jnp.concatenate([pv[h] / l[h] for h in heads], axis=1).astype(BF16)

    pl.when(has_pad_keys)(functools.partial(tile, True))
    pl.when(jnp.logical_not(has_pad_keys))(functools.partial(tile, False))


def _attn(q2d, k3d, v3d, bias, *, batch, t, tq, width, pad, koff):
    nq = t // tq
    tk = k3d.shape[1]
    assert koff % tq == 0 and koff + (nq - 1) * tq + width <= tk
    return pl.pallas_call(
        functools.partial(_attn_kernel, tq=tq, width=width, pad=pad, koff=koff),
        grid=(batch, nq),
        in_specs=[
            pl.BlockSpec((tq, ATT_W), lambda b, n: (b * nq + n, 0)),
            pl.BlockSpec((1, tk, ATT_W), lambda b, n: (b, 0, 0)),
            pl.BlockSpec((1, tk, ATT_W), lambda b, n: (b, 0, 0)),
            pl.BlockSpec((ATT_HEADS, tq, width), lambda b, n: (0, 0, 0)),
        ],
        out_specs=pl.BlockSpec((tq, ATT_W), lambda b, n: (b * nq + n, 0)),
        out_shape=jax.ShapeDtypeStruct((batch * t, ATT_W), BF16),
        compiler_params=_cparams(("arbitrary", "arbitrary")),
        name="attn",
    )(q2d, k3d, v3d, bias)


def _attn_step_kernel(q_ref, kn_ref, vn_ref, kt_ref, vt_ref, bc_ref, bn_ref, o_ref):
    heads = range(ATT_HEADS)
    cols = [slice(h * ATT_DH, (h + 1) * ATT_DH) for h in heads]
    q = [q_ref[:, cols[h]] for h in heads]
    s_c = [_dot(q[h], kt_ref[0, h]) + bc_ref[h] for h in heads]
    s_n = [_dot_nt(q[h], kn_ref[:, cols[h]]) + bn_ref[h] for h in heads]
    mx = [jnp.maximum(jnp.max(s_c[h], axis=-1, keepdims=True), jnp.max(s_n[h], axis=-1, keepdims=True))
          for h in heads]
    p_c = [jnp.exp2(s_c[h] - mx[h]) for h in heads]
    p_n = [jnp.exp2(s_n[h] - mx[h]) for h in heads]
    l = [jnp.sum(p_c[h], axis=-1, keepdims=True) + jnp.sum(p_n[h], axis=-1, keepdims=True) for h in heads]
    pv = [_dot_nt(p_c[h], vt_ref[0, h]) + _dot(p_n[h], vn_ref[:, cols[h]]) for h in heads]
    o_ref[...] = jnp.concatenate([pv[h] / l[h] for h in heads], axis=1).astype(BF16)


def _attn_step(q2d, kn2d, vn2d, kt, vt, bias, *, batch, t):
    n_past = kt.shape[-1]
    rows = lambda b: (b, 0)
    cache = lambda b: (b, 0, 0, 0)
    const = lambda b: (0, 0, 0)
    return pl.pallas_call(
        _attn_step_kernel,
        grid=(batch,),
        in_specs=[
            pl.BlockSpec((t, ATT_W), rows),
            pl.BlockSpec((t, ATT_W), rows),
            pl.BlockSpec((t, ATT_W), rows),
            pl.BlockSpec((1, ATT_HEADS, ATT_DH, n_past), cache),
            pl.BlockSpec((1, ATT_HEADS, ATT_DH, n_past), cache),
            pl.BlockSpec((ATT_HEADS, t, n_past), const),
            pl.BlockSpec((ATT_HEADS, t, t), const),
        ],
        out_specs=pl.BlockSpec((t, ATT_W), rows),
        out_shape=jax.ShapeDtypeStruct((batch * t, ATT_W), BF16),
        compiler_params=_cparams(("arbitrary",)),
        name="attn_step",
    )(q2d, kn2d, vn2d, kt, vt, bias[:, :, :n_past], bias[:, :, n_past:])


def _merge_kernel(x_ref, g_ref, b_ref, oa_ref, ob_ref, ga_ref, gb_ref, wa_ref, wb_ref, wo_ref,
                  g1_ref, b1_ref, x1_ref, *, parts):
    sub = x_ref.shape[0] // parts
    for s in range(parts):
        r = pl.ds(s * sub, sub)
        ya = jnp.dot(oa_ref[r, :], wa_ref[...], preferred_element_type=F32)
        yb = jnp.dot(ob_ref[r, :], wb_ref[...], preferred_element_type=F32)
        mixed = (_sigmoid(ga_ref[r, :].astype(F32)) * ya + _sigmoid(gb_ref[r, :].astype(F32)) * yb)
        xn = _layernorm_rows(x_ref[r, :], g_ref[...], b_ref[...])
        y = DEEPNORM_ALPHA * xn + _dot(mixed, wo_ref[...])
        x1_ref[r, :] = _layernorm_rows(y, g1_ref[...], b1_ref[...])


def _merge(x2d, ln_g, ln_b, oa, ob, h2d, wa, wb, wo, ln1_g, ln1_b, *, tm):
    rows = x2d.shape[0]
    full = lambda i: (0, 0)
    return pl.pallas_call(
        functools.partial(_merge_kernel, parts=2 if tm >= 512 else 1),
        grid=(rows // tm,),
        in_specs=[
            pl.BlockSpec((tm, D_MODEL), lambda i: (i, 0)),
            pl.BlockSpec((1, D_MODEL), full),
            pl.BlockSpec((1, D_MODEL), full),
            pl.BlockSpec((tm, GDN_W), lambda i: (i, 0)),
            pl.BlockSpec((tm, ATT_W), lambda i: (i, 0)),
            pl.BlockSpec((tm, D_MODEL), lambda i: (i, COL["gate_a"] // D_MODEL)),
            pl.BlockSpec((tm, D_MODEL), lambda i: (i, COL["gate_b"] // D_MODEL)),
            pl.BlockSpec((GDN_W, D_MODEL), full),
            pl.BlockSpec((ATT_W, D_MODEL), full),
            pl.BlockSpec((D_MODEL, D_MODEL), full),
            pl.BlockSpec((1, D_MODEL), full),
            pl.BlockSpec((1, D_MODEL), full),
        ],
        out_specs=pl.BlockSpec((tm, D_MODEL), lambda i: (i, 0)),
        out_shape=jax.ShapeDtypeStruct((rows, D_MODEL), F32),
        compiler_params=_cparams(("arbitrary",)),
        name="merge",
    )(x2d, ln_g, ln_b, oa, ob, h2d, h2d, wa, wb, wo, ln1_g, ln1_b)


def _gelu_tanh(x):
    return 0.5 * x * (1.0 + jnp.tanh(math.sqrt(2.0 / math.pi) * (x + 0.044715 * (x * x * x))))


def _ffn_kernel(x1_ref, pe_ref, wup_ref, cw_ref, wdn_ref, wple_ref, wpg_ref, g2_ref, b2_ref,
                finit_ref, y_ref, fstate_ref, gp_ref, acc_ref, act_ref, *, tm, fc, ns, down_group):
    n = pl.program_id(1)
    pad = SUBLANES
    streams = range(ns)

    @pl.when(n == 0)
    def _():
        gp_ref[:, 0:pad, :] = finit_ref[...]

    x1 = x1_ref[...]
    x1b = x1.astype(BF16)

    def up(k):
        c0 = k * fc
        gate = jnp.dot(x1b, wup_ref[:, c0:c0 + fc], preferred_element_type=F32)
        val = jnp.dot(x1b, wup_ref[:, D_FF + c0:D_FF + c0 + fc], preferred_element_type=F32)
        for s in streams:
            gp_ref[s, pad:pad + tm, c0:c0 + fc] = gate[s * tm:(s + 1) * tm]
        return gate, val

    nk = D_FF // fc
    nxt = up(0)
    ple = jnp.dot(pe_ref[...].astype(BF16), wple_ref[...], preferred_element_type=F32)
    for k in range(nk):
        c0 = k * fc
        gate, val = nxt
        if k + 1 < nk:
            nxt = up(k + 1)
        conv = gate * cw_ref[FFN_CONV - 1:FFN_CONV, c0:c0 + fc]
        for i in range(1, FFN_CONV):
            prev = [gp_ref[s, pl.ds(pad - i, tm), c0:c0 + fc] for s in streams]
            conv = conv + (prev[0] if ns == 1 else jnp.concatenate(prev, axis=0)) * cw_ref[
                FFN_CONV - 1 - i:FFN_CONV - i, c0:c0 + fc]
        act_ref[:, c0:c0 + fc] = (_gelu_tanh(conv) * val).astype(BF16)
        if (k + 1) % down_group == 0 or k + 1 == nk:
            g0 = (k // down_group) * down_group * fc
            down = jnp.dot(act_ref[:, g0:c0 + fc], wdn_ref[g0:c0 + fc, :], preferred_element_type=F32)
            if g0 == 0:
                acc_ref[...] = down
            else:
                acc_ref[...] += down
    tail = gp_ref[:, tm:tm + pad, :]
    gp_ref[:, 0:pad, :] = tail
    fstate_ref[...] = tail

    r = DEEPNORM_ALPHA * x1 + acc_ref[...]
    gate_p = _sigmoid(_dot(r, wpg_ref[...]))
    r = r + gate_p * ple
    y_ref[...] = _layernorm_rows(r, g2_ref[...], b2_ref[...])


def _ffn(x1, pe, wup, conv_w, wdn, wple, wpg, ln2_g, ln2_b, finit, *, batch, t, tm, fc, ns):
    nt = t // tm
    assert batch % ns == 0 and (ns == 1 or nt == 1)
    rows = batch * t
    tile = ns * tm
    full = lambda b, n: (0, 0)
    return pl.pallas_call(
        functools.partial(_ffn_kernel, tm=tm, fc=fc, ns=ns, down_group=FFN_DOWN_GROUP),
        grid=(batch // ns, nt),
        in_specs=[
            pl.BlockSpec((tile, D_MODEL), lambda b, n: (b * nt + n, 0)),
            pl.BlockSpec((tile, PLE_DIM), lambda b, n: (b * nt + n, 0)),
            pl.BlockSpec((D_MODEL, 2 * D_FF), full, pipeline_mode=pl.Buffered(1)),
            pl.BlockSpec((FFN_CONV, D_FF), full),
            pl.BlockSpec((D_FF, D_MODEL), full, pipeline_mode=pl.Buffered(1)),
            pl.BlockSpec((PLE_DIM, D_MODEL), full),
            pl.BlockSpec((D_MODEL, D_MODEL), full),
            pl.BlockSpec((1, D_MODEL), full),
            pl.BlockSpec((1, D_MODEL), full),
            pl.BlockSpec((ns, SUBLANES, D_FF), lambda b, n: (b, 0, 0)),
        ],
        out_specs=[
            pl.BlockSpec((tile, D_MODEL), lambda b, n: (b * nt + n, 0)),
            pl.BlockSpec((ns, SUBLANES, D_FF), lambda b, n: (b, 0, 0)),
        ],
        out_shape=[
            jax.ShapeDtypeStruct((rows, D_MODEL), F32),
            jax.ShapeDtypeStruct((batch, SUBLANES, D_FF), F32),
        ],
        scratch_shapes=[pltpu.VMEM((ns, SUBLANES + tm, D_FF), F32), pltpu.VMEM((tile, D_MODEL), F32),
                        pltpu.VMEM((tile, D_FF), BF16)],
        compiler_params=_cparams(("arbitrary", "arbitrary")),
        name="ffn",
    )(x1, pe, wup, conv_w, wdn, wple, wpg, ln2_g, ln2_b, finit)


def _rel_bias(table, n_q, n_past, n_k):
    span = n_q + n_k - 1
    rel = n_q - 1 + n_past - jnp.arange(span)
    rev = table[:, jnp.clip(rel, -REL_CLIP, REL_CLIP) + REL_CLIP].astype(F32)
    heads = table.shape[0]
    period = jnp.pad(rev, ((0, 0), (0, 1)))
    skew = jnp.tile(period, (1, n_q))[:, :n_q * span].reshape(heads, n_q, span)
    return skew[:, :, n_q - 1:n_q - 1 + n_k]


def _pad_state(state, rows):
    b, r, ch = state.shape
    return jnp.concatenate([jnp.zeros((b, rows - r, ch), F32), state.astype(F32)], axis=1)


def _layer(x, pe, gconv_prev, s_gdn, k_prev, v_prev, fconv_prev, wts, *, tm_proj, tm_merge, tm_ffn, gdn_nb,
           att_chunks=1, ffn_ns=1):
    (ln_in_g, ln_in_b, w_proj, gdn_conv_w, a_log, dt_bias, norm_w, rel_table, w_a, w_b, w_o,
     ln1_g, ln1_b, w_up, ffn_conv_w, w_dn, w_ple, w_pg, ln2_g, ln2_b) = wts
    b, t, _ = x.shape
    rows = b * t
    x2d = x.reshape(rows, D_MODEL)

    prompt = k_prev is None
    tm_proj = min(tm_proj, t if prompt else rows)
    pad_rows = tm_proj if prompt else 0
    assert pad_rows >= BAND or not prompt
    h2d, ab, q_b, k_b, v_b, gconv_new = _in_proj(
        x2d, ln_in_g, ln_in_b, *w_proj, gdn_conv_w, _pad_state(gconv_prev, SUBLANES), tm=tm_proj,
        t=t if prompt else rows, pad_rows=pad_rows, ns=1 if prompt else b)

    o_a, s_new = _gdn(h2d, ab, a_log, dt_bias, norm_w, s_gdn.astype(F32), batch=b, t=t,
                      nb=math.gcd(gdn_nb, b))

    k_b = k_b.reshape(b, -1, ATT_W)
    v_b = v_b.reshape(b, -1, ATT_W)
    if prompt:
        tq = att_chunks * CHUNK
        width = BAND + tq
        own = (jnp.arange(tq) // CHUNK)[:, None] * CHUNK
        key = jnp.arange(width)[None, :]
        in_band = (key >= own) & (key < own + BAND + CHUNK)
        bias = jnp.where(in_band, _rel_bias(rel_table, tq, BAND, width) * LOG2E, -jnp.inf)
        o_b = _attn(q_b, k_b, v_b, bias, batch=b, t=t, tq=tq, width=width, pad=BAND,
                    koff=pad_rows - BAND)
        keep = min(BAND, t)
        k_new, v_new = k_b[:, pad_rows + t - keep:], v_b[:, pad_rows + t - keep:]
    else:
        n_past = k_prev.shape[1]
        bias = _rel_bias(rel_table, t, n_past, n_past + t) * LOG2E
        o_b = _attn_step(q_b, k_b.reshape(rows, ATT_W), v_b.reshape(rows, ATT_W),
                         jnp.transpose(k_prev, (0, 2, 3, 1)).astype(F32),
                         jnp.transpose(v_prev, (0, 2, 3, 1)).astype(F32), bias, batch=b, t=t)
        k_new, v_new = k_b, v_b

    x1 = _merge(x2d, ln_in_g, ln_in_b, o_a, o_b, h2d, w_a, w_b, w_o, ln1_g, ln1_b,
                tm=min(tm_merge, rows))

    y, fconv_new = _ffn(x1, pe.reshape(rows, PLE_DIM), w_up, ffn_conv_w, w_dn, w_ple, w_pg,
                        ln2_g, ln2_b, _pad_state(fconv_prev, SUBLANES), batch=b, t=t,
                        tm=min(tm_ffn, t), fc=256, ns=math.gcd(ffn_ns, b))

    keep_k = k_new.shape[1]
    return (y.reshape(b, t, D_MODEL),
            gconv_new[:, SUBLANES - (GDN_CONV - 1):],
            s_new,
            k_new.astype(F32).reshape(b, keep_k, ATT_HEADS, ATT_DH),
            v_new.astype(F32).reshape(b, keep_k, ATT_HEADS, ATT_DH),
            fconv_new[:, SUBLANES - (FFN_CONV - 1):])


def _pad_lanes(v, width=LANES):
    return jnp.pad(v.astype(F32), (0, width - v.shape[0])).reshape(1, width)


def kernel(x_prompt, x_sample, state_gdn_conv, state_gdn, cache_attn_k, cache_attn_v, state_ffn_conv,
           p_prompt, p_sample, ln_in_g, ln_in_b, w_in, gdn_conv_w, gdn_a_log, gdn_dt_bias, gdn_norm_w,
           att_rel_bias, w_branch_a, w_branch_b, w_out, ln1_g, ln1_b, w_ffn_up, ffn_conv_w, w_ffn_down,
           w_ple, w_ple_gate, ln2_g, ln2_b):
    depth = w_in.shape[0]
    assert depth == 1
    bp = x_prompt.shape[0]
    row = lambda v: v.reshape(1, -1).astype(F32)

    xp, xs = x_prompt, x_sample
    new_p, new_s = [], []
    for i in range(depth):
        wi = w_in[i]
        o_ab = 4 * GDN_W
        o_att = o_ab + 2 * GDN_HEADS
        o_gate = o_att + 3 * ATT_W
        wt = wi.T.astype(BF16)
        src = {"q": 0, "k": GDN_W, "v": 2 * GDN_W, "z": 3 * GDN_W, "gate_a": o_gate, "gate_b": o_gate + D_MODEL}
        w_proj = (jnp.concatenate([wt[src[name]:src[name] + GDN_W] for name in H_SECTIONS], axis=0),
                  wt[o_att:o_gate],
                  jnp.pad(wt[o_ab:o_att], ((0, LANES - 2 * GDN_HEADS), (0, 0))))
        wts = (row(ln_in_g), row(ln_in_b), w_proj, gdn_conv_w[i].astype(F32),
               _pad_lanes(gdn_a_log[i]), _pad_lanes(gdn_dt_bias[i]), row(gdn_norm_w[i]), att_rel_bias[i],
               w_branch_a[i].astype(BF16), w_branch_b[i].astype(BF16), w_out[i].astype(BF16),
               row(ln1_g[i]), row(ln1_b[i]), w_ffn_up[i].astype(BF16), ffn_conv_w[i].astype(F32),
               w_ffn_down[i].astype(BF16), w_ple[i].astype(BF16), w_ple_gate[i].astype(BF16),
               row(ln2_g[i]), row(ln2_b[i]))
        xp, st_p = _unpack(_layer(xp, p_prompt[i],
                                  jnp.zeros((bp, GDN_CONV - 1, GDN_CONV_CH), F32),
                                  jnp.zeros((bp, GDN_HEADS, GDN_DK, GDN_DV), F32),
                                  None, None,
                                  jnp.zeros((bp, FFN_CONV - 1, D_FF), F32), wts,
                                  tm_proj=1024, tm_merge=1024, tm_ffn=512, gdn_nb=4, att_chunks=4))
        xs, st_s = _unpack(_layer(xs, p_sample[i], state_gdn_conv[i], state_gdn[i], cache_attn_k[i],
                                  cache_attn_v[i], state_ffn_conv[i], wts,
                                  tm_proj=256, tm_merge=256, tm_ffn=256, gdn_nb=4, ffn_ns=16))
        new_p.append(st_p)
        new_s.append(st_s)
    p_gconv, p_gdn, p_k, p_v, p_fconv = [jnp.stack(a) for a in zip(*new_p)]
    s_gconv, s_gdn, s_k, s_v, s_fconv = [jnp.stack(a) for a in zip(*new_s)]
    return (xp, xs, p_gconv, p_gdn, p_k, p_v, p_fconv, s_gconv, s_gdn, s_k, s_v, s_fconv)


def _unpack(res):
    return res[0], res[1:]
```

```python
import functools
import math

import jax
import jax.numpy as jnp
from jax import lax
from jax.experimental import pallas as pl
from jax.experimental.pallas import tpu as pltpu

F32 = jnp.float32
BF16 = jnp.bfloat16

D_MODEL = 1024
CHUNK = 64
BAND = 512
GDN_HEADS = 8
GDN_DK = 128
GDN_DV = 128
GDN_CONV = 4
ATT_HEADS = 8
ATT_DH = 64
REL_CLIP = 128
D_FF = 2816
FFN_CONV = 3
PLE_DIM = 256
DEEPNORM_ALPHA = 2.0 ** 0.25
LN_EPS = 1e-5
RMS_EPS = 1e-6
L2_EPS = 1e-6

GDN_W = GDN_HEADS * GDN_DK
GDN_CONV_CH = 3 * GDN_W
ATT_W = ATT_HEADS * ATT_DH
LANES = 128
SUBLANES = 8

H_SECTIONS = ("q", "gate_a", "k", "gate_b", "v", "z")
COL = {name: idx * GDN_W for idx, name in enumerate(H_SECTIONS)}
CONV_COL = {"q": 0, "k": GDN_W, "v": 2 * GDN_W}
H_MAIN = len(H_SECTIONS) * GDN_W

LOG2E = math.log2(math.e)
ATT_QSCALE = ATT_DH ** -0.5 * LOG2E

VMEM_LIMIT = 56 * 1024 * 1024


def _cparams(sem):
    return pltpu.CompilerParams(dimension_semantics=sem, vmem_limit_bytes=VMEM_LIMIT)


def _sigmoid(x):
    return 1.0 / (1.0 + jnp.exp(-x))


def _silu(x):
    return x * _sigmoid(x)


def _layernorm_rows(x, g, b):
    mu = jnp.mean(x, axis=-1, keepdims=True)
    xc = x - mu
    var = jnp.mean(xc * xc, axis=-1, keepdims=True)
    return xc * lax.rsqrt(var + LN_EPS) * g + b


def _dot(a, b):
    return jnp.dot(a.astype(BF16), b.astype(BF16), preferred_element_type=F32)


def _dot_nt(a, b):
    return lax.dot_general(a.astype(BF16), b.astype(BF16), (((1,), (1,)), ((), ())),
                           preferred_element_type=F32)


def _dot_tn(a, b):
    return lax.dot_general(a.astype(BF16), b.astype(BF16), (((0,), (0,)), ((), ())),
                           preferred_element_type=F32)


def _split3(x):
    x1 = x.astype(BF16)
    r1 = x - x1.astype(F32)
    x2 = r1.astype(BF16)
    x3 = (r1 - x2.astype(F32)).astype(BF16)
    return x1, x2, x3


CONV_COLS = 256


def _in_proj_kernel(x_ref, g_ref, b_ref, wm_ref, watt_ref, wab_ref, cw_ref, cinit_ref,
                    kz_ref, vz_ref,
                    h_ref, ab_ref, q_ref, k_ref, v_ref, cs_ref, xnf_ref,
                    xn_ref, carry_ref, xp0_ref, xp1_ref, *, sub, n_main, ns, first_tile_every):
    del kz_ref, vz_ref
    i = pl.program_id(0)
    j = pl.program_id(1)
    tm = x_ref.shape[0]
    tn = h_ref.shape[1]
    tr = tm // ns
    pad = SUBLANES
    streams = range(ns)

    @pl.when(j == 0)
    def _():
        def body(r_i, c):
            r = pl.multiple_of(r_i * sub, sub)
            xn = _layernorm_rows(x_ref[pl.ds(r, sub), :], g_ref[...], b_ref[...])
            xnf_ref[pl.ds(r, sub), :] = xn
            xn_ref[pl.ds(r, sub), :] = xn.astype(BF16)
            return c
        lax.fori_loop(0, tm // sub, body, 0)
        ab_ref[...] = _dot_nt(xn_ref[...], wab_ref[...])

    def project(g0):
        return _dot_nt(xn_ref[...], wm_ref[g0:g0 + CONV_COLS, :])

    def conv_silu(pre, cc, buf):
        outs = []
        for s in streams:
            buf[s, 0:pad, :] = carry_ref[s, :, cc:cc + CONV_COLS]
            buf[s, pad:pad + tr, :] = pre[s * tr:(s + 1) * tr]
        for s in streams:
            acc = buf[s, pad:pad + tr, :] * cw_ref[GDN_CONV - 1:GDN_CONV, cc:cc + CONV_COLS]
            for t in range(1, GDN_CONV):
                acc = acc + (buf[s, pl.ds(pad - t, tr), :]
                             * cw_ref[GDN_CONV - 1 - t:GDN_CONV - t, cc:cc + CONV_COLS])
            outs.append(acc)
            carry_ref[s, :, cc:cc + CONV_COLS] = buf[s, tr:tr + pad, :]
        return _silu(outs[0] if ns == 1 else jnp.concatenate(outs, axis=0))

    def l2norm_heads(y, mult):
        parts = []
        for lo in range(0, CONV_COLS, GDN_DK):
            yh = y[:, lo:lo + GDN_DK]
            parts.append(yh * (lax.rsqrt(jnp.sum(yh * yh, axis=-1, keepdims=True) + L2_EPS) * mult))
        return jnp.concatenate(parts, axis=1)

    def main_step(step):
        base = step * tn
        kinds = [H_SECTIONS[(base + c0) // GDN_W] for c0 in range(0, tn, GDN_W)]
        convs = [k for k in kinds if k in CONV_COL]

        @pl.when(lax.rem(i, first_tile_every) == 0)
        def _():
            for k in convs:
                cc = CONV_COL[k]
                carry_ref[:, :, cc:cc + GDN_W] = cinit_ref[:, :, cc:cc + GDN_W]

        order = [sec * GDN_W + c for c in range(0, GDN_W, CONV_COLS) for sec in range(tn // GDN_W)]
        nxt = project(base + order[0])
        for pos, c0 in enumerate(order):
            g0 = base + c0
            kind = H_SECTIONS[g0 // GDN_W]
            pre = nxt
            if pos + 1 < len(order):
                nxt = project(base + order[pos + 1])
            if kind in CONV_COL:
                y = conv_silu(pre, CONV_COL[kind] + g0 % GDN_W, xp0_ref if pos % 2 == 0 else xp1_ref)
                if kind == "q":
                    y = l2norm_heads(y, GDN_DK ** -0.5)
                elif kind == "k":
                    y = l2norm_heads(y, 1.0)
            elif kind == "z":
                y = _silu(pre)
            else:
                y = pre
            h_ref[:, c0:c0 + CONV_COLS] = y.astype(BF16)
        for k in convs:
            cc = CONV_COL[k]
            cs_ref[:, :, cc:cc + GDN_W] = carry_ref[:, :, cc:cc + GDN_W]

    for step in range(n_main):
        pl.when(j == step)(functools.partial(main_step, step))

    @pl.when(j == n_main)
    def _():
        for o_ref, c0, mult in ((q_ref, 0, ATT_QSCALE), (k_ref, ATT_W, None), (v_ref, 2 * ATT_W, None)):
            proj = _dot_nt(xn_ref[...], watt_ref[c0:c0 + ATT_W, :])
            o_ref[...] = (proj if mult is None else proj * mult).astype(BF16)


def _in_proj(x2d, ln_g, ln_b, w_main, w_att, w_ab, conv_w, conv_init, *, tm, t, pad_rows, ns):
    rows = x2d.shape[0]
    tn = 2 * GDN_W
    n_main = H_MAIN // tn
    n_streams = conv_init.shape[0]
    assert rows % tm == 0 and t % tm == 0 and pad_rows % tm == 0
    assert w_main.shape[0] == H_MAIN and w_att.shape[0] == 3 * ATT_W
    assert (ns == 1 and rows == n_streams * t) or (rows == n_streams * (tm // ns) and rows == tm)
    sub = min(tm, 128)
    per_stream = t // tm
    stride = per_stream + pad_rows // tm
    kv_rows = (rows // t) * (t + pad_rows)
    kv_block = lambda i, j: ((i // per_stream) * stride + pad_rows // tm + i % per_stream, 0)
    stream_block = lambda i, j: (i // per_stream, 0, 0)
    zeros = jnp.zeros((kv_rows, ATT_W), BF16)
    const = lambda i, j: (0, 0)
    return pl.pallas_call(
        functools.partial(_in_proj_kernel, sub=sub, n_main=n_main, ns=ns, first_tile_every=per_stream),
        grid=(rows // tm, n_main + 1),
        in_specs=[
            pl.BlockSpec((tm, D_MODEL), lambda i, j: (i, 0)),
            pl.BlockSpec((1, D_MODEL), const),
            pl.BlockSpec((1, D_MODEL), const),
            pl.BlockSpec((H_MAIN, D_MODEL), const, pipeline_mode=pl.Buffered(1)),
            pl.BlockSpec((3 * ATT_W, D_MODEL), const, pipeline_mode=pl.Buffered(1)),
            pl.BlockSpec((LANES, D_MODEL), const),
            pl.BlockSpec((GDN_CONV, GDN_CONV_CH), const),
            pl.BlockSpec((ns, SUBLANES, GDN_CONV_CH), stream_block),
            pl.BlockSpec(memory_space=pl.ANY),
            pl.BlockSpec(memory_space=pl.ANY),
        ],
        out_specs=[
            pl.BlockSpec((tm, tn), lambda i, j: (i, jnp.minimum(j, n_main - 1))),
            pl.BlockSpec((tm, LANES), lambda i, j: (i, 0)),
            pl.BlockSpec((tm, ATT_W), lambda i, j: (i, 0)),
            pl.BlockSpec((tm, ATT_W), kv_block),
            pl.BlockSpec((tm, ATT_W), kv_block),
            pl.BlockSpec((ns, SUBLANES, GDN_CONV_CH), stream_block),
            pl.BlockSpec((tm, D_MODEL), lambda i, j: (i, 0)),
        ],
        out_shape=[
            jax.ShapeDtypeStruct((rows, H_MAIN), BF16),
            jax.ShapeDtypeStruct((rows, LANES), F32),
            jax.ShapeDtypeStruct((rows, ATT_W), BF16),
            jax.ShapeDtypeStruct((kv_rows, ATT_W), BF16),
            jax.ShapeDtypeStruct((kv_rows, ATT_W), BF16),
            jax.ShapeDtypeStruct((n_streams, SUBLANES, GDN_CONV_CH), F32),
            jax.ShapeDtypeStruct((rows, D_MODEL), F32),
        ],
        input_output_aliases={8: 3, 9: 4},
        scratch_shapes=[
            pltpu.VMEM((tm, D_MODEL), BF16),
            pltpu.VMEM((ns, SUBLANES, GDN_CONV_CH), F32),
            pltpu.VMEM((ns, SUBLANES + tm // ns, CONV_COLS), F32),
            pltpu.VMEM((ns, SUBLANES + tm // ns, CONV_COLS), F32),
        ],
        compiler_params=_cparams(("arbitrary", "arbitrary")),
        name="in_proj",
    )(x2d, ln_g, ln_b, w_main, w_att, w_ab, conv_w, conv_init, zeros, zeros)


def _gdn_kernel(q_ref, k_ref, v_ref, z_ref, ab_ref, alog_ref, dtb_ref, nw_ref, s0_ref,
                o_ref, s_ref, *, c, nb):
    n = pl.program_id(1)
    heads = range(GDN_HEADS)
    pairs = [(bi, h) for bi in range(nb) for h in heads]

    @pl.when(n == 0)
    def _():
        s_ref[...] = s0_ref[...]

    row = lax.broadcasted_iota(jnp.int32, (c, c), 0)
    col = lax.broadcasted_iota(jnp.int32, (c, c), 1)
    causal = row >= col
    strict = row > col
    eye = jnp.where(row == col, 1.0, 0.0).astype(F32)
    tri = jnp.where(causal, 1.0, 0.0).astype(BF16)

    dec_all, dec_t, beta_all = [], [], []
    for bi in range(nb):
        ab = ab_ref[bi]
        sp_in = ab + dtb_ref[...]
        softplus = jnp.maximum(sp_in, 0.0) + jnp.log(1.0 + jnp.exp(-jnp.abs(sp_in)))
        g_all = -jnp.exp(alog_ref[...]) * softplus
        g1, g2, g3 = _split3(g_all)
        dec = (jnp.dot(tri, g1, preferred_element_type=F32)
               + (jnp.dot(tri, g2, preferred_element_type=F32)
                  + jnp.dot(tri, g3, preferred_element_type=F32)))
        dec_all.append(dec)
        dec_t.append(dec.T)
        beta_all.append(_sigmoid(ab))

    def cols(ref, bi, h, width=GDN_DK):
        return ref[bi, :, h * width:(h + 1) * width].astype(F32)

    qn = [cols(q_ref, bi, h) for bi, h in pairs]
    kn = [cols(k_ref, bi, h) for bi, h in pairs]
    xv = [cols(v_ref, bi, h, GDN_DV) for bi, h in pairs]
    dcol = [dec_all[bi][:, h:h + 1] for bi, h in pairs]
    beta = [beta_all[bi][:, GDN_HEADS + h:GDN_HEADS + h + 1] for bi, h in pairs]
    lmask = [jnp.exp(jnp.where(causal, dec_all[bi][:, h:h + 1] - dec_t[bi][h:h + 1, :], -jnp.inf))
             for bi, h in pairs]
    idx = range(len(pairs))
    kb = [kn[i] * beta[i] for i in idx]
    e_d = [jnp.exp(d) for d in dcol]
    gram = [_dot_nt(jnp.concatenate([kb[i], qn[i]], axis=0), kn[i]) for i in idx]
    m = [jnp.where(strict, gram[i][:c] * lmask[i], 0.0) for i in idx]
    qk = [gram[i][c:] * lmask[i] for i in idx]
    p = [eye - m[i] for i in idx]
    mk = m
    for _ in range(int(math.log2(c)) - 1):
        mk = [_dot(x, x) for x in mk]
        p = [p[i] + _dot(p[i], mk[i]) for i in idx]
    sol = [_dot(p[i], jnp.concatenate([xv[i] * beta[i], kb[i] * e_d[i]], axis=1)) for i in idx]
    s_old = [s_ref[bi, h] for bi, h in pairs]
    ws_qs = [_dot(jnp.concatenate([sol[i][:, GDN_DV:], qn[i] * e_d[i]], axis=0), s_old[i])
             for i in idx]
    v_new = [sol[i][:, :GDN_DV] - ws_qs[i][:c] for i in idx]
    dl = [d[c - 1:c, :] for d in dcol]
    kd = [kn[i] * jnp.exp(dl[i] - dcol[i]) for i in idx]
    s_upd = [_dot_tn(kd[i], v_new[i]) for i in idx]
    o_in = [_dot(qk[i], v_new[i]) for i in idx]
    for i, (bi, h) in enumerate(pairs):
        s_ref[bi, h] = s_old[i] * jnp.exp(dl[i]) + s_upd[i]
        o = ws_qs[i][c:] + o_in[i]
        o = (o * lax.rsqrt(jnp.mean(o * o, axis=-1, keepdims=True) + RMS_EPS) * nw_ref[...]
             * cols(z_ref, bi, h, GDN_DV))
        o_ref[bi, :, pl.ds(h * GDN_DV, GDN_DV)] = o.astype(BF16)


def _gdn(h2d, ab, a_log, dt_bias, norm_w, s0, *, batch, t, nb):
    c = min(CHUNK, t)
    nc = t // c
    assert batch % nb == 0
    h3d = h2d.reshape(batch, t, H_MAIN)
    ab3d = ab.reshape(batch, t, LANES)
    chunk = lambda col: (lambda b, n: (b, n, col))
    o_a, s_new = pl.pallas_call(
        functools.partial(_gdn_kernel, c=c, nb=nb),
        grid=(batch // nb, nc),
        in_specs=[
            pl.BlockSpec((nb, c, GDN_W), chunk(COL["q"] // GDN_W)),
            pl.BlockSpec((nb, c, GDN_W), chunk(COL["k"] // GDN_W)),
            pl.BlockSpec((nb, c, GDN_W), chunk(COL["v"] // GDN_W)),
            pl.BlockSpec((nb, c, GDN_W), chunk(COL["z"] // GDN_W)),
            pl.BlockSpec((nb, c, LANES), chunk(0)),
            pl.BlockSpec((1, LANES), lambda b, n: (0, 0)),
            pl.BlockSpec((1, LANES), lambda b, n: (0, 0)),
            pl.BlockSpec((1, GDN_DV), lambda b, n: (0, 0)),
            pl.BlockSpec((nb, GDN_HEADS, GDN_DK, GDN_DV), lambda b, n: (b, 0, 0, 0)),
        ],
        out_specs=[
            pl.BlockSpec((nb, c, GDN_W), chunk(0)),
            pl.BlockSpec((nb, GDN_HEADS, GDN_DK, GDN_DV), lambda b, n: (b, 0, 0, 0)),
        ],
        out_shape=[
            jax.ShapeDtypeStruct((batch, t, GDN_W), BF16),
            jax.ShapeDtypeStruct((batch, GDN_HEADS, GDN_DK, GDN_DV), F32),
        ],
        compiler_params=_cparams(("arbitrary", "arbitrary")),
        name="gdn",
    )(h3d, h3d, h3d, h3d, ab3d, a_log, dt_bias, norm_w, s0)
    return o_a.reshape(batch * t, GDN_W), s_new


def _attn_kernel(q_ref, k_ref, v_ref, bias_ref, o_ref, *, tq, width, pad, koff):
    n = pl.program_id(1)
    start = pl.multiple_of(koff + n * tq, tq)
    heads = range(ATT_HEADS)
    cols = [slice(h * ATT_DH, (h + 1) * ATT_DH) for h in heads]
    has_pad_keys = n * tq < pad

    def tile(mask_pad_keys):
        kwin = k_ref[0, pl.ds(start, width), :]
        vwin = v_ref[0, pl.ds(start, width), :]
        s = [_dot_nt(q_ref[:, cols[h]], kwin[:, cols[h]]) + bias_ref[h] for h in heads]
        if mask_pad_keys:
            kpos = n * tq - pad + lax.broadcasted_iota(jnp.int32, (tq, width), 1)
            s = [jnp.where(kpos >= 0, x, -jnp.inf) for x in s]
        p = [jnp.exp2(x - jnp.max(x, axis=-1, keepdims=True)) for x in s]
        l = [jnp.sum(x, axis=-1, keepdims=True) for x in p]
        pv = [_dot(p[h], vwin[:, cols[h]]) for h in heads]
        o_ref[...] = jnp.concatenate([pv[h] / l[h] for h in heads], axis=1).astype(BF16)

    pl.when(has_pad_keys)(functools.partial(tile, True))
    pl.when(jnp.logical_not(has_pad_keys))(functools.partial(tile, False))


def _attn(q2d, k3d, v3d, bias, *, batch, t, tq, width, pad, koff):
    nq = t // tq
    tk = k3d.shape[1]
    assert koff % tq == 0 and koff + (nq - 1) * tq + width <= tk
    return pl.pallas_call(
        functools.partial(_attn_kernel, tq=tq, width=width, pad=pad, koff=koff),
        grid=(batch, nq),
        in_specs=[
            pl.BlockSpec((tq, ATT_W), lambda b, n: (b * nq + n, 0)),
            pl.BlockSpec((1, tk, ATT_W), lambda b, n: (b, 0, 0)),
            pl.BlockSpec((1, tk, ATT_W), lambda b, n: (b, 0, 0)),
            pl.BlockSpec((ATT_HEADS, tq, width), lambda b, n: (0, 0, 0)),
        ],
        out_specs=pl.BlockSpec((tq, ATT_W), lambda b, n: (b * nq + n, 0)),
        out_shape=jax.ShapeDtypeStruct((batch * t, ATT_W), BF16),
        compiler_params=_cparams(("arbitrary", "arbitrary")),
        name="attn",
    )(q2d, k3d, v3d, bias)


def _attn_step_kernel(q_ref, kn_ref, vn_ref, kt_ref, vt_ref, bc_ref, bn_ref, o_ref, *, nb, t):
    heads = range(ATT_HEADS)
    cols = [slice(h * ATT_DH, (h + 1) * ATT_DH) for h in heads]
    chains = [(bi, h) for bi in range(nb) for h in heads]
    rows = [pl.ds(bi * t, t) for bi in range(nb)]
    q = [q_ref[rows[bi], cols[h]] for bi, h in chains]
    s_c = [_dot(q[i], kt_ref[bi, h]) + bc_ref[h] for i, (bi, h) in enumerate(chains)]
    s_n = [_dot_nt(q[i], kn_ref[rows[bi], cols[h]]) + bn_ref[h] for i, (bi, h) in enumerate(chains)]
    idx = range(len(chains))
    mx = [jnp.maximum(jnp.max(s_c[i], axis=-1, keepdims=True), jnp.max(s_n[i], axis=-1, keepdims=True))
          for i in idx]
    p_c = [jnp.exp2(s_c[i] - mx[i]) for i in idx]
    p_n = [jnp.exp2(s_n[i] - mx[i]) for i in idx]
    l = [jnp.sum(p_c[i], axis=-1, keepdims=True) + jnp.sum(p_n[i], axis=-1, keepdims=True) for i in idx]
    pv = [_dot_nt(p_c[i], vt_ref[bi, h]) + _dot(p_n[i], vn_ref[rows[bi], cols[h]])
          for i, (bi, h) in enumerate(chains)]
    for bi in range(nb):
        o_ref[rows[bi], :] = jnp.concatenate(
            [pv[bi * ATT_HEADS + h] / l[bi * ATT_HEADS + h] for h in heads], axis=1).astype(BF16)


def _attn_step(q2d, kn2d, vn2d, kt, vt, bias, *, batch, t, nb):
    n_past = kt.shape[-1]
    assert batch % nb == 0
    rows = lambda b: (b, 0)
    cache = lambda b: (b, 0, 0, 0)
    const = lambda b: (0, 0, 0)
    return pl.pallas_call(
        functools.partial(_attn_step_kernel, nb=nb, t=t),
        grid=(batch // nb,),
        in_specs=[
            pl.BlockSpec((nb * t, ATT_W), rows),
            pl.BlockSpec((nb * t, ATT_W), rows),
            pl.BlockSpec((nb * t, ATT_W), rows),
            pl.BlockSpec((nb, ATT_HEADS, ATT_DH, n_past), cache),
            pl.BlockSpec((nb, ATT_HEADS, ATT_DH, n_past), cache),
            pl.BlockSpec((ATT_HEADS, t, n_past), const),
            pl.BlockSpec((ATT_HEADS, t, t), const),
        ],
        out_specs=pl.BlockSpec((nb * t, ATT_W), rows),
        out_shape=jax.ShapeDtypeStruct((batch * t, ATT_W), BF16),
        compiler_params=_cparams(("arbitrary",)),
        name="attn_step",
    )(q2d, kn2d, vn2d, kt, vt, bias[:, :, :n_past], bias[:, :, n_past:])


def _merge_kernel(xn_ref, oa_ref, ob_ref, ga_ref, gb_ref, wa_ref, wb_ref, wo_ref,
                  g1_ref, b1_ref, x1_ref, *, parts):
    sub = xn_ref.shape[0] // parts
    for s in range(parts):
        r = pl.ds(s * sub, sub)
        ya = jnp.dot(oa_ref[r, :], wa_ref[...], preferred_element_type=F32)
        yb = jnp.dot(ob_ref[r, :], wb_ref[...], preferred_element_type=F32)
        mixed = (_sigmoid(ga_ref[r, :].astype(F32)) * ya + _sigmoid(gb_ref[r, :].astype(F32)) * yb)
        y = DEEPNORM_ALPHA * xn_ref[r, :] + _dot(mixed, wo_ref[...])
        x1_ref[r, :] = _layernorm_rows(y, g1_ref[...], b1_ref[...])


def _merge(xn2d, oa, ob, h2d, wa, wb, wo, ln1_g, ln1_b, *, tm):
    rows = xn2d.shape[0]
    full = lambda i: (0, 0)
    return pl.pallas_call(
        functools.partial(_merge_kernel, parts=2 if tm >= 512 else 1),
        grid=(rows // tm,),
        in_specs=[
            pl.BlockSpec((tm, D_MODEL), lambda i: (i, 0)),
            pl.BlockSpec((tm, GDN_W), lambda i: (i, 0)),
            pl.BlockSpec((tm, ATT_W), lambda i: (i, 0)),
            pl.BlockSpec((tm, D_MODEL), lambda i: (i, COL["gate_a"] // D_MODEL)),
            pl.BlockSpec((tm, D_MODEL), lambda i: (i, COL["gate_b"] // D_MODEL)),
            pl.BlockSpec((GDN_W, D_MODEL), full),
            pl.BlockSpec((ATT_W, D_MODEL), full),
            pl.BlockSpec((D_MODEL, D_MODEL), full),
            pl.BlockSpec((1, D_MODEL), full),
            pl.BlockSpec((1, D_MODEL), full),
        ],
        out_specs=pl.BlockSpec((tm, D_MODEL), lambda i: (i, 0)),
        out_shape=jax.ShapeDtypeStruct((rows, D_MODEL), F32),
        compiler_params=_cparams(("arbitrary",)),
        name="merge",
    )(xn2d, oa, ob, h2d, h2d, wa, wb, wo, ln1_g, ln1_b)


def _gelu_tanh(x):
    return 0.5 * x * (1.0 + jnp.tanh(math.sqrt(2.0 / math.pi) * (x + 0.044715 * (x * x * x))))


def _ffn_kernel(x1_ref, pe_ref, wup_ref, cw_ref, wdn_ref, wple_ref, wpg_ref, g2_ref, b2_ref,
                finit_ref, y_ref, fstate_ref, gp_ref, act_ref, *, tm, fc, ns):
    n = pl.program_id(1)
    pad = SUBLANES
    streams = range(ns)

    @pl.when(n == 0)
    def _():
        gp_ref[:, 0:pad, :] = finit_ref[...]

    x1 = x1_ref[...]
    x1b = x1.astype(BF16)

    def up(k):
        c0 = k * fc
        gate = jnp.dot(x1b, wup_ref[:, c0:c0 + fc], preferred_element_type=F32)
        val = jnp.dot(x1b, wup_ref[:, D_FF + c0:D_FF + c0 + fc], preferred_element_type=F32)
        for s in streams:
            gp_ref[s, pad:pad + tm, c0:c0 + fc] = gate[s * tm:(s + 1) * tm]
        return gate, val

    nk = D_FF // fc
    nxt = up(0)
    ple = jnp.dot(pe_ref[...].astype(BF16), wple_ref[...], preferred_element_type=F32)
    for k in range(nk):
        c0 = k * fc
        gate, val = nxt
        if k + 1 < nk:
            nxt = up(k + 1)
        conv = gate * cw_ref[FFN_CONV - 1:FFN_CONV, c0:c0 + fc]
        for i in range(1, FFN_CONV):
            prev = [gp_ref[s, pl.ds(pad - i, tm), c0:c0 + fc] for s in streams]
            conv = conv + (prev[0] if ns == 1 else jnp.concatenate(prev, axis=0)) * cw_ref[
                FFN_CONV - 1 - i:FFN_CONV - i, c0:c0 + fc]
        act_ref[:, c0:c0 + fc] = (_gelu_tanh(conv) * val).astype(BF16)
    tail = gp_ref[:, tm:tm + pad, :]
    gp_ref[:, 0:pad, :] = tail
    fstate_ref[...] = tail

    down = jnp.dot(act_ref[...], wdn_ref[...], preferred_element_type=F32)
    r = DEEPNORM_ALPHA * x1 + down
    gate_p = _sigmoid(_dot(r, wpg_ref[...]))
    r = r + gate_p * ple
    y_ref[...] = _layernorm_rows(r, g2_ref[...], b2_ref[...])


def _ffn(x1, pe, wup, conv_w, wdn, wple, wpg, ln2_g, ln2_b, finit, *, batch, t, tm, fc, ns):
    nt = t // tm
    assert batch % ns == 0 and (ns == 1 or nt == 1)
    rows = batch * t
    tile = ns * tm
    full = lambda b, n: (0, 0)
    return pl.pallas_call(
        functools.partial(_ffn_kernel, tm=tm, fc=fc, ns=ns),
        grid=(batch // ns, nt),
        in_specs=[
            pl.BlockSpec((tile, D_MODEL), lambda b, n: (b * nt + n, 0)),
            pl.BlockSpec((tile, PLE_DIM), lambda b, n: (b * nt + n, 0)),
            pl.BlockSpec((D_MODEL, 2 * D_FF), full, pipeline_mode=pl.Buffered(1)),
            pl.BlockSpec((FFN_CONV, D_FF), full),
            pl.BlockSpec((D_FF, D_MODEL), full, pipeline_mode=pl.Buffered(1)),
            pl.BlockSpec((PLE_DIM, D_MODEL), full),
            pl.BlockSpec((D_MODEL, D_MODEL), full),
            pl.BlockSpec((1, D_MODEL), full),
            pl.BlockSpec((1, D_MODEL), full),
            pl.BlockSpec((ns, SUBLANES, D_FF), lambda b, n: (b, 0, 0)),
        ],
        out_specs=[
            pl.BlockSpec((tile, D_MODEL), lambda b, n: (b * nt + n, 0)),
            pl.BlockSpec((ns, SUBLANES, D_FF), lambda b, n: (b, 0, 0)),
        ],
        out_shape=[
            jax.ShapeDtypeStruct((rows, D_MODEL), F32),
            jax.ShapeDtypeStruct((batch, SUBLANES, D_FF), F32),
        ],
        scratch_shapes=[pltpu.VMEM((ns, SUBLANES + tm, D_FF), F32), pltpu.VMEM((tile, D_FF), BF16)],
        compiler_params=_cparams(("arbitrary", "arbitrary")),
        name="ffn",
    )(x1, pe, wup, conv_w, wdn, wple, wpg, ln2_g, ln2_b, finit)


def _rel_bias(table, n_q, n_past, n_k):
    span = n_q + n_k - 1
    rel = n_q - 1 + n_past - jnp.arange(span)
    rev = table[:, jnp.clip(rel, -REL_CLIP, REL_CLIP) + REL_CLIP].astype(F32)
    heads = table.shape[0]
    period = jnp.pad(rev, ((0, 0), (0, 1)))
    skew = jnp.tile(period, (1, n_q))[:, :n_q * span].reshape(heads, n_q, span)
    return skew[:, :, n_q - 1:n_q - 1 + n_k]


def _pad_state(state, rows):
    b, r, ch = state.shape
    return jnp.concatenate([jnp.zeros((b, rows - r, ch), F32), state.astype(F32)], axis=1)


def _layer(x, pe, gconv_prev, s_gdn, k_prev, v_prev, fconv_prev, wts, *, tm_proj, tm_merge, tm_ffn, gdn_nb,
           att_chunks=1, ffn_ns=1):
    (ln_in_g, ln_in_b, w_proj, gdn_conv_w, a_log, dt_bias, norm_w, rel_table, w_a, w_b, w_o,
     ln1_g, ln1_b, w_up, ffn_conv_w, w_dn, w_ple, w_pg, ln2_g, ln2_b) = wts
    b, t, _ = x.shape
    rows = b * t
    x2d = x.reshape(rows, D_MODEL)

    prompt = k_prev is None
    tm_proj = min(tm_proj, t if prompt else rows)
    pad_rows = tm_proj if prompt else 0
    assert pad_rows >= BAND or not prompt
    h2d, ab, q_b, k_b, v_b, gconv_new, xn2d = _in_proj(
        x2d, ln_in_g, ln_in_b, *w_proj, gdn_conv_w, _pad_state(gconv_prev, SUBLANES), tm=tm_proj,
        t=t if prompt else rows, pad_rows=pad_rows, ns=1 if prompt else b)

    o_a, s_new = _gdn(h2d, ab, a_log, dt_bias, norm_w, s_gdn.astype(F32), batch=b, t=t,
                      nb=math.gcd(gdn_nb, b))

    k_b = k_b.reshape(b, -1, ATT_W)
    v_b = v_b.reshape(b, -1, ATT_W)
    if prompt:
        tq = att_chunks * CHUNK
        width = BAND + tq
        own = (jnp.arange(tq) // CHUNK)[:, None] * CHUNK
        key = jnp.arange(width)[None, :]
        in_band = (key >= own) & (key < own + BAND + CHUNK)
        bias = jnp.where(in_band, _rel_bias(rel_table, tq, BAND, width) * LOG2E, -jnp.inf)
        o_b = _attn(q_b, k_b, v_b, bias, batch=b, t=t, tq=tq, width=width, pad=BAND,
                    koff=pad_rows - BAND)
        keep = min(BAND, t)
        k_new, v_new = k_b[:, pad_rows + t - keep:], v_b[:, pad_rows + t - keep:]
    else:
        n_past = k_prev.shape[1]
        bias = _rel_bias(rel_table, t, n_past, n_past + t) * LOG2E
        o_b = _attn_step(q_b, k_b.reshape(rows, ATT_W), v_b.reshape(rows, ATT_W),
                         jnp.transpose(k_prev, (0, 2, 3, 1)).astype(F32),
                         jnp.transpose(v_prev, (0, 2, 3, 1)).astype(F32), bias, batch=b, t=t,
                         nb=math.gcd(4, b))
        k_new, v_new = k_b, v_b

    x1 = _merge(xn2d, o_a, o_b, h2d, w_a, w_b, w_o, ln1_g, ln1_b,
                tm=min(tm_merge, rows))

    y, fconv_new = _ffn(x1, pe.reshape(rows, PLE_DIM), w_up, ffn_conv_w, w_dn, w_ple, w_pg,
                        ln2_g, ln2_b, _pad_state(fconv_prev, SUBLANES), batch=b, t=t,
                        tm=min(tm_ffn, t), fc=256, ns=math.gcd(ffn_ns, b))

    keep_k = k_new.shape[1]
    return (y.reshape(b, t, D_MODEL),
            gconv_new[:, SUBLANES - (GDN_CONV - 1):],
            s_new,
            k_new.astype(F32).reshape(b, keep_k, ATT_HEADS, ATT_DH),
            v_new.astype(F32).reshape(b, keep_k, ATT_HEADS, ATT_DH),
            fconv_new[:, SUBLANES - (FFN_CONV - 1):])


def _pad_lanes(v, width=LANES):
    return jnp.pad(v.astype(F32), (0, width - v.shape[0])).reshape(1, width)


def kernel(x_prompt, x_sample, state_gdn_conv, state_gdn, cache_attn_k, cache_attn_v, state_ffn_conv,
           p_prompt, p_sample, ln_in_g, ln_in_b, w_in, gdn_conv_w, gdn_a_log, gdn_dt_bias, gdn_norm_w,
           att_rel_bias, w_branch_a, w_branch_b, w_out, ln1_g, ln1_b, w_ffn_up, ffn_conv_w, w_ffn_down,
           w_ple, w_ple_gate, ln2_g, ln2_b):
    depth = w_in.shape[0]
    assert depth == 1
    bp = x_prompt.shape[0]
    row = lambda v: v.reshape(1, -1).astype(F32)

    xp, xs = x_prompt, x_sample
    new_p, new_s = [], []
    for i in range(depth):
        wi = w_in[i]
        o_ab = 4 * GDN_W
        o_att = o_ab + 2 * GDN_HEADS
        o_gate = o_att + 3 * ATT_W
        wt = wi.T.astype(BF16)
        src = {"q": 0, "k": GDN_W, "v": 2 * GDN_W, "z": 3 * GDN_W, "gate_a": o_gate, "gate_b": o_gate + D_MODEL}
        w_proj = (jnp.concatenate([wt[src[name]:src[name] + GDN_W] for name in H_SECTIONS], axis=0),
                  wt[o_att:o_gate],
                  jnp.pad(wt[o_ab:o_att], ((0, LANES - 2 * GDN_HEADS), (0, 0))))
        wts = (row(ln_in_g), row(ln_in_b), w_proj, gdn_conv_w[i].astype(F32),
               _pad_lanes(gdn_a_log[i]), _pad_lanes(gdn_dt_bias[i]), row(gdn_norm_w[i]), att_rel_bias[i],
               w_branch_a[i].astype(BF16), w_branch_b[i].astype(BF16), w_out[i].astype(BF16),
               row(ln1_g[i]), row(ln1_b[i]), w_ffn_up[i].astype(BF16), ffn_conv_w[i].astype(F32),
               w_ffn_down[i].astype(BF16), w_ple[i].astype(BF16), w_ple_gate[i].astype(BF16),
               row(ln2_g[i]), row(ln2_b[i]))
        xp, st_p = _unpack(_layer(xp, p_prompt[i],
                                  jnp.zeros((bp, GDN_CONV - 1, GDN_CONV_CH), F32),
                                  jnp.zeros((bp, GDN_HEADS, GDN_DK, GDN_DV), F32),
                                  None, None,
                                  jnp.zeros((bp, FFN_CONV - 1, D_FF), F32), wts,
                                  tm_proj=1024, tm_merge=1024, tm_ffn=512, gdn_nb=4, att_chunks=4))
        xs, st_s = _unpack(_layer(xs, p_sample[i], state_gdn_conv[i], state_gdn[i], cache_attn_k[i],
                                  cache_attn_v[i], state_ffn_conv[i], wts,
                                  tm_proj=256, tm_merge=256, tm_ffn=256, gdn_nb=4, ffn_ns=16))
        new_p.append(st_p)
        new_s.append(st_s)
    p_gconv, p_gdn, p_k, p_v, p_fconv = [jnp.stack(a) for a in zip(*new_p)]
    s_gconv, s_gdn, s_k, s_v, s_fconv = [jnp.stack(a) for a in zip(*new_s)]
    return (xp, xs, p_gconv, p_gdn, p_k, p_v, p_fconv, s_gconv, s_gdn, s_k, s_v, s_fconv)


def _unpack(res):
    return res[0], res[1:]
```

```python
import functools
import math

import jax
import jax.numpy as jnp
from jax import lax
from jax.experimental import pallas as pl
from jax.experimental.pallas import tpu as pltpu

F32 = jnp.float32
BF16 = jnp.bfloat16

D_MODEL = 1024
CHUNK = 64
BAND = 512
GDN_HEADS = 8
GDN_DK = 128
GDN_DV = 128
GDN_CONV = 4
ATT_HEADS = 8
ATT_DH = 64
REL_CLIP = 128
D_FF = 2816
FFN_CONV = 3
PLE_DIM = 256
DEEPNORM_ALPHA = 2.0 ** 0.25
LN_EPS = 1e-5
RMS_EPS = 1e-6
L2_EPS = 1e-6

GDN_W = GDN_HEADS * GDN_DK
GDN_CONV_CH = 3 * GDN_W
ATT_W = ATT_HEADS * ATT_DH
LANES = 128
SUBLANES = 8

H_SECTIONS = ("q", "gate_a", "k", "gate_b", "v", "z")
COL = {name: idx * GDN_W for idx, name in enumerate(H_SECTIONS)}
CONV_COL = {"q": 0, "k": GDN_W, "v": 2 * GDN_W}
H_MAIN = len(H_SECTIONS) * GDN_W

LOG2E = math.log2(math.e)
ATT_QSCALE = ATT_DH ** -0.5 * LOG2E

VMEM_LIMIT = 56 * 1024 * 1024


def _cparams(sem):
    return pltpu.CompilerParams(dimension_semantics=sem, vmem_limit_bytes=VMEM_LIMIT)


def _sigmoid(x):
    return 1.0 / (1.0 + jnp.exp(-x))


def _silu(x):
    return x * _sigmoid(x)


def _layernorm_rows(x, g, b):
    mu = jnp.mean(x, axis=-1, keepdims=True)
    xc = x - mu
    var = jnp.mean(xc * xc, axis=-1, keepdims=True)
    return xc * lax.rsqrt(var + LN_EPS) * g + b


def _dot(a, b):
    return jnp.dot(a.astype(BF16), b.astype(BF16), preferred_element_type=F32)


def _dot_nt(a, b):
    return lax.dot_general(a.astype(BF16), b.astype(BF16), (((1,), (1,)), ((), ())),
                           preferred_element_type=F32)


def _dot_tn(a, b):
    return lax.dot_general(a.astype(BF16), b.astype(BF16), (((0,), (0,)), ((), ())),
                           preferred_element_type=F32)


def _split3(x):
    x1 = x.astype(BF16)
    r1 = x - x1.astype(F32)
    x2 = r1.astype(BF16)
    x3 = (r1 - x2.astype(F32)).astype(BF16)
    return x1, x2, x3


CONV_COLS = 256


def _in_proj_kernel(x_ref, g_ref, b_ref, wm_ref, watt_ref, wab_ref, cw_ref, cinit_ref,
                    kz_ref, vz_ref,
                    h_ref, ab_ref, q_ref, k_ref, v_ref, cs_ref, xnf_ref,
                    xn_ref, carry_ref, xp0_ref, xp1_ref, *, sub, n_main, ns, first_tile_every):
    del kz_ref, vz_ref
    i = pl.program_id(0)
    j = pl.program_id(1)
    tm = x_ref.shape[0]
    tn = h_ref.shape[1]
    tr = tm // ns
    pad = SUBLANES
    streams = range(ns)

    @pl.when(j == 0)
    def _():
        def body(r_i, c):
            r = pl.multiple_of(r_i * sub, sub)
            xn = _layernorm_rows(x_ref[pl.ds(r, sub), :], g_ref[...], b_ref[...])
            xnf_ref[pl.ds(r, sub), :] = xn
            xn_ref[pl.ds(r, sub), :] = xn.astype(BF16)
            return c
        lax.fori_loop(0, tm // sub, body, 0)
        ab_ref[...] = _dot_nt(xn_ref[...], wab_ref[...])

    def project(g0):
        return _dot_nt(xn_ref[...], wm_ref[g0:g0 + CONV_COLS, :])

    def conv_silu(pre, cc, buf):
        outs = []
        for s in streams:
            buf[s, 0:pad, :] = carry_ref[s, :, cc:cc + CONV_COLS]
            buf[s, pad:pad + tr, :] = pre[s * tr:(s + 1) * tr]
        for s in streams:
            acc = buf[s, pad:pad + tr, :] * cw_ref[GDN_CONV - 1:GDN_CONV, cc:cc + CONV_COLS]
            for t in range(1, GDN_CONV):
                acc = acc + (buf[s, pl.ds(pad - t, tr), :]
                             * cw_ref[GDN_CONV - 1 - t:GDN_CONV - t, cc:cc + CONV_COLS])
            outs.append(acc)
            carry_ref[s, :, cc:cc + CONV_COLS] = buf[s, tr:tr + pad, :]
        return _silu(outs[0] if ns == 1 else jnp.concatenate(outs, axis=0))

    def main_step(step):
        base = step * tn
        kinds = [H_SECTIONS[(base + c0) // GDN_W] for c0 in range(0, tn, GDN_W)]
        convs = [k for k in kinds if k in CONV_COL]

        @pl.when(lax.rem(i, first_tile_every) == 0)
        def _():
            for k in convs:
                cc = CONV_COL[k]
                carry_ref[:, :, cc:cc + GDN_W] = cinit_ref[:, :, cc:cc + GDN_W]

        order = [sec * GDN_W + c for c in range(0, GDN_W, CONV_COLS) for sec in range(tn // GDN_W)]
        nxt = project(base + order[0])
        for pos, c0 in enumerate(order):
            g0 = base + c0
            kind = H_SECTIONS[g0 // GDN_W]
            pre = nxt
            if pos + 1 < len(order):
                nxt = project(base + order[pos + 1])
            if kind in CONV_COL:
                y = conv_silu(pre, CONV_COL[kind] + g0 % GDN_W, xp0_ref if pos % 2 == 0 else xp1_ref)
            elif kind == "z":
                y = _silu(pre)
            else:
                y = pre
            h_ref[:, c0:c0 + CONV_COLS] = y.astype(BF16)
        for k in convs:
            cc = CONV_COL[k]
            cs_ref[:, :, cc:cc + GDN_W] = carry_ref[:, :, cc:cc + GDN_W]

    for step in range(n_main):
        pl.when(j == step)(functools.partial(main_step, step))

    @pl.when(j == n_main)
    def _():
        for o_ref, c0, mult in ((q_ref, 0, ATT_QSCALE), (k_ref, ATT_W, None), (v_ref, 2 * ATT_W, None)):
            proj = _dot_nt(xn_ref[...], watt_ref[c0:c0 + ATT_W, :])
            o_ref[...] = (proj if mult is None else proj * mult).astype(BF16)


def _in_proj(x2d, ln_g, ln_b, w_main, w_att, w_ab, conv_w, conv_init, *, tm, t, pad_rows, ns):
    rows = x2d.shape[0]
    tn = 2 * GDN_W
    n_main = H_MAIN // tn
    n_streams = conv_init.shape[0]
    assert rows % tm == 0 and t % tm == 0 and pad_rows % tm == 0
    assert w_main.shape[0] == H_MAIN and w_att.shape[0] == 3 * ATT_W
    assert (ns == 1 and rows == n_streams * t) or (rows == n_streams * (tm // ns) and rows == tm)
    sub = min(tm, 128)
    per_stream = t // tm
    stride = per_stream + pad_rows // tm
    kv_rows = (rows // t) * (t + pad_rows)
    kv_block = lambda i, j: ((i // per_stream) * stride + pad_rows // tm + i % per_stream, 0)
    stream_block = lambda i, j: (i // per_stream, 0, 0)
    zeros = jnp.zeros((kv_rows, ATT_W), BF16)
    const = lambda i, j: (0, 0)
    return pl.pallas_call(
        functools.partial(_in_proj_kernel, sub=sub, n_main=n_main, ns=ns, first_tile_every=per_stream),
        grid=(rows // tm, n_main + 1),
        in_specs=[
            pl.BlockSpec((tm, D_MODEL), lambda i, j: (i, 0)),
            pl.BlockSpec((1, D_MODEL), const),
            pl.BlockSpec((1, D_MODEL), const),
            pl.BlockSpec((H_MAIN, D_MODEL), const, pipeline_mode=pl.Buffered(1)),
            pl.BlockSpec((3 * ATT_W, D_MODEL), const, pipeline_mode=pl.Buffered(1)),
            pl.BlockSpec((LANES, D_MODEL), const),
            pl.BlockSpec((GDN_CONV, GDN_CONV_CH), const),
            pl.BlockSpec((ns, SUBLANES, GDN_CONV_CH), stream_block),
            pl.BlockSpec(memory_space=pl.ANY),
            pl.BlockSpec(memory_space=pl.ANY),
        ],
        out_specs=[
            pl.BlockSpec((tm, tn), lambda i, j: (i, jnp.minimum(j, n_main - 1))),
            pl.BlockSpec((tm, LANES), lambda i, j: (i, 0)),
            pl.BlockSpec((tm, ATT_W), lambda i, j: (i, 0)),
            pl.BlockSpec((tm, ATT_W), kv_block),
            pl.BlockSpec((tm, ATT_W), kv_block),
            pl.BlockSpec((ns, SUBLANES, GDN_CONV_CH), stream_block),
            pl.BlockSpec((tm, D_MODEL), lambda i, j: (i, 0)),
        ],
        out_shape=[
            jax.ShapeDtypeStruct((rows, H_MAIN), BF16),
            jax.ShapeDtypeStruct((rows, LANES), F32),
            jax.ShapeDtypeStruct((rows, ATT_W), BF16),
            jax.ShapeDtypeStruct((kv_rows, ATT_W), BF16),
            jax.ShapeDtypeStruct((kv_rows, ATT_W), BF16),
            jax.ShapeDtypeStruct((n_streams, SUBLANES, GDN_CONV_CH), F32),
            jax.ShapeDtypeStruct((rows, D_MODEL), F32),
        ],
        input_output_aliases={8: 3, 9: 4},
        scratch_shapes=[
            pltpu.VMEM((tm, D_MODEL), BF16),
            pltpu.VMEM((ns, SUBLANES, GDN_CONV_CH), F32),
            pltpu.VMEM((ns, SUBLANES + tm // ns, CONV_COLS), F32),
            pltpu.VMEM((ns, SUBLANES + tm // ns, CONV_COLS), F32),
        ],
        compiler_params=_cparams(("arbitrary", "arbitrary")),
        name="in_proj",
    )(x2d, ln_g, ln_b, w_main, w_att, w_ab, conv_w, conv_init, zeros, zeros)


def _gdn_kernel(q_ref, k_ref, v_ref, z_ref, ab_ref, alog_ref, dtb_ref, nw_ref, s0_ref,
                o_ref, s_ref, *, c, nb):
    n = pl.program_id(1)
    heads = range(GDN_HEADS)
    pairs = [(bi, h) for bi in range(nb) for h in heads]

    @pl.when(n == 0)
    def _():
        s_ref[...] = s0_ref[...]

    row = lax.broadcasted_iota(jnp.int32, (c, c), 0)
    col = lax.broadcasted_iota(jnp.int32, (c, c), 1)
    causal = row >= col
    strict = row > col
    eye = jnp.where(row == col, 1.0, 0.0).astype(F32)
    tri = jnp.where(causal, 1.0, 0.0).astype(BF16)

    dec_all, dec_t, beta_all = [], [], []
    for bi in range(nb):
        ab = ab_ref[bi]
        sp_in = ab + dtb_ref[...]
        softplus = jnp.maximum(sp_in, 0.0) + jnp.log(1.0 + jnp.exp(-jnp.abs(sp_in)))
        g_all = -jnp.exp(alog_ref[...]) * softplus
        g1, g2, g3 = _split3(g_all)
        dec = (jnp.dot(tri, g1, preferred_element_type=F32)
               + (jnp.dot(tri, g2, preferred_element_type=F32)
                  + jnp.dot(tri, g3, preferred_element_type=F32)))
        dec_all.append(dec)
        dec_t.append(dec.T)
        beta_all.append(_sigmoid(ab))

    def cols(ref, bi, h, width=GDN_DK):
        return ref[bi, :, h * width:(h + 1) * width].astype(F32)

    def l2norm(x, mult):
        return x * (lax.rsqrt(jnp.sum(x * x, axis=-1, keepdims=True) + L2_EPS) * mult)

    qn = [l2norm(cols(q_ref, bi, h), GDN_DK ** -0.5) for bi, h in pairs]
    kn = [l2norm(cols(k_ref, bi, h), 1.0) for bi, h in pairs]
    xv = [cols(v_ref, bi, h, GDN_DV) for bi, h in pairs]
    dcol = [dec_all[bi][:, h:h + 1] for bi, h in pairs]
    beta = [beta_all[bi][:, GDN_HEADS + h:GDN_HEADS + h + 1] for bi, h in pairs]
    lmask = [jnp.exp(jnp.where(causal, dec_all[bi][:, h:h + 1] - dec_t[bi][h:h + 1, :], -jnp.inf))
             for bi, h in pairs]
    idx = range(len(pairs))
    kb = [kn[i] * beta[i] for i in idx]
    e_d = [jnp.exp(d) for d in dcol]
    gram = [_dot_nt(jnp.concatenate([kb[i], qn[i]], axis=0), kn[i]) for i in idx]
    m = [jnp.where(strict, gram[i][:c] * lmask[i], 0.0) for i in idx]
    qk = [gram[i][c:] * lmask[i] for i in idx]
    p = [eye - m[i] for i in idx]
    mk = m
    for _ in range(int(math.log2(c)) - 1):
        mk = [_dot(x, x) for x in mk]
        p = [p[i] + _dot(p[i], mk[i]) for i in idx]
    sol = [_dot(p[i], jnp.concatenate([xv[i] * beta[i], kb[i] * e_d[i]], axis=1)) for i in idx]
    s_old = [s_ref[bi, h] for bi, h in pairs]
    ws_qs = [_dot(jnp.concatenate([sol[i][:, GDN_DV:], qn[i] * e_d[i]], axis=0), s_old[i])
             for i in idx]
    v_new = [sol[i][:, :GDN_DV] - ws_qs[i][:c] for i in idx]
    dl = [d[c - 1:c, :] for d in dcol]
    kd = [kn[i] * jnp.exp(dl[i] - dcol[i]) for i in idx]
    s_upd = [_dot_tn(kd[i], v_new[i]) for i in idx]
    o_in = [_dot(qk[i], v_new[i]) for i in idx]
    for i, (bi, h) in enumerate(pairs):
        s_ref[bi, h] = s_old[i] * jnp.exp(dl[i]) + s_upd[i]
        o = ws_qs[i][c:] + o_in[i]
        o = (o * lax.rsqrt(jnp.mean(o * o, axis=-1, keepdims=True) + RMS_EPS) * nw_ref[...]
             * cols(z_ref, bi, h, GDN_DV))
        o_ref[bi, :, pl.ds(h * GDN_DV, GDN_DV)] = o.astype(BF16)


def _gdn(h2d, ab, a_log, dt_bias, norm_w, s0, *, batch, t, nb):
    c = min(CHUNK, t)
    nc = t // c
    assert batch % nb == 0
    h3d = h2d.reshape(batch, t, H_MAIN)
    ab3d = ab.reshape(batch, t, LANES)
    chunk = lambda col: (lambda b, n: (b, n, col))
    o_a, s_new = pl.pallas_call(
        functools.partial(_gdn_kernel, c=c, nb=nb),
        grid=(batch // nb, nc),
        in_specs=[
            pl.BlockSpec((nb, c, GDN_W), chunk(COL["q"] // GDN_W)),
            pl.BlockSpec((nb, c, GDN_W), chunk(COL["k"] // GDN_W)),
            pl.BlockSpec((nb, c, GDN_W), chunk(COL["v"] // GDN_W)),
            pl.BlockSpec((nb, c, GDN_W), chunk(COL["z"] // GDN_W)),
            pl.BlockSpec((nb, c, LANES), chunk(0)),
            pl.BlockSpec((1, LANES), lambda b, n: (0, 0)),
            pl.BlockSpec((1, LANES), lambda b, n: (0, 0)),
            pl.BlockSpec((1, GDN_DV), lambda b, n: (0, 0)),
            pl.BlockSpec((nb, GDN_HEADS, GDN_DK, GDN_DV), lambda b, n: (b, 0, 0, 0)),
        ],
        out_specs=[
            pl.BlockSpec((nb, c, GDN_W), chunk(0)),
            pl.BlockSpec((nb, GDN_HEADS, GDN_DK, GDN_DV), lambda b, n: (b, 0, 0, 0)),
        ],
        out_shape=[
            jax.ShapeDtypeStruct((batch, t, GDN_W), BF16),
            jax.ShapeDtypeStruct((batch, GDN_HEADS, GDN_DK, GDN_DV), F32),
        ],
        compiler_params=_cparams(("arbitrary", "arbitrary")),
        name="gdn",
    )(h3d, h3d, h3d, h3d, ab3d, a_log, dt_bias, norm_w, s0)
    return o_a.reshape(batch * t, GDN_W), s_new


def _attn_kernel(q_ref, k_ref, v_ref, bias_ref, o_ref, *, tq, width, pad, koff):
    n = pl.program_id(1)
    start = pl.multiple_of(koff + n * tq, tq)
    heads = range(ATT_HEADS)
    cols = [slice(h * ATT_DH, (h + 1) * ATT_DH) for h in heads]
    has_pad_keys = n * tq < pad

    def tile(mask_pad_keys):
        kwin = k_ref[0, pl.ds(start, width), :]
        vwin = v_ref[0, pl.ds(start, width), :]
        s = [_dot_nt(q_ref[:, cols[h]], kwin[:, cols[h]]) + bias_ref[h] for h in heads]
        if mask_pad_keys:
            kpos = n * tq - pad + lax.broadcasted_iota(jnp.int32, (tq, width), 1)
            s = [jnp.where(kpos >= 0, x, -jnp.inf) for x in s]
        p = [jnp.exp2(x - jnp.max(x, axis=-1, keepdims=True)) for x in s]
        l = [jnp.sum(x, axis=-1, keepdims=True) for x in p]
        pv = [_dot(p[h], vwin[:, cols[h]]) for h in heads]
        o_ref[...] = jnp.concatenate([pv[h] / l[h] for h in heads], axis=1).astype(BF16)

    pl.when(has_pad_keys)(functools.partial(tile, True))
    pl.when(jnp.logical_not(has_pad_keys))(functools.partial(tile, False))


def _attn(q2d, k3d, v3d, bias, *, batch, t, tq, width, pad, koff):
    nq = t // tq
    tk = k3d.shape[1]
    assert koff % tq == 0 and koff + (nq - 1) * tq + width <= tk
    return pl.pallas_call(
        functools.partial(_attn_kernel, tq=tq, width=width, pad=pad, koff=koff),
        grid=(batch, nq),
        in_specs=[
            pl.BlockSpec((tq, ATT_W), lambda b, n: (b * nq + n, 0)),
            pl.BlockSpec((1, tk, ATT_W), lambda b, n: (b, 0, 0)),
            pl.BlockSpec((1, tk, ATT_W), lambda b, n: (b, 0, 0)),
            pl.BlockSpec((ATT_HEADS, tq, width), lambda b, n: (0, 0, 0)),
        ],
        out_specs=pl.BlockSpec((tq, ATT_W), lambda b, n: (b * nq + n, 0)),
        out_shape=jax.ShapeDtypeStruct((batch * t, ATT_W), BF16),
        compiler_params=_cparams(("arbitrary", "arbitrary")),
        name="attn",
    )(q2d, k3d, v3d, bias)


def _attn_step_kernel(q_ref, kn_ref, vn_ref, kt_ref, vt_ref, bc_ref, bn_ref, o_ref, *, nb, t):
    heads = range(ATT_HEADS)
    cols = [slice(h * ATT_DH, (h + 1) * ATT_DH) for h in heads]
    chains = [(bi, h) for bi in range(nb) for h in heads]
    rows = [pl.ds(bi * t, t) for bi in range(nb)]
    q = [q_ref[rows[bi], cols[h]] for bi, h in chains]
    s_c = [_dot(q[i], kt_ref[bi, h]) + bc_ref[h] for i, (bi, h) in enumerate(chains)]
    s_n = [_dot_nt(q[i], kn_ref[rows[bi], cols[h]]) + bn_ref[h] for i, (bi, h) in enumerate(chains)]
    idx = range(len(chains))
    mx = [jnp.maximum(jnp.max(s_c[i], axis=-1, keepdims=True), jnp.max(s_n[i], axis=-1, keepdims=True))
          for i in idx]
    p_c = [jnp.exp2(s_c[i] - mx[i]) for i in idx]
    p_n = [jnp.exp2(s_n[i] - mx[i]) for i in idx]
    l = [jnp.sum(p_c[i], axis=-1, keepdims=True) + jnp.sum(p_n[i], axis=-1, keepdims=True) for i in idx]
    pv = [_dot_nt(p_c[i], vt_ref[bi, h]) + _dot(p_n[i], vn_ref[rows[bi], cols[h]])
          for i, (bi, h) in enumerate(chains)]
    for bi in range(nb):
        o_ref[rows[bi], :] = jnp.concatenate(
            [pv[bi * ATT_HEADS + h] / l[bi * ATT_HEADS + h] for h in heads], axis=1).astype(BF16)


def _attn_step(q2d, kn2d, vn2d, kt, vt, bias, *, batch, t, nb):
    n_past = kt.shape[-1]
    assert batch % nb == 0
    rows = lambda b: (b, 0)
    cache = lambda b: (b, 0, 0, 0)
    const = lambda b: (0, 0, 0)
    return pl.pallas_call(
        functools.partial(_attn_step_kernel, nb=nb, t=t),
        grid=(batch // nb,),
        in_specs=[
            pl.BlockSpec((nb * t, ATT_W), rows),
            pl.BlockSpec((nb * t, ATT_W), rows),
            pl.BlockSpec((nb * t, ATT_W), rows),
            pl.BlockSpec((nb, ATT_HEADS, ATT_DH, n_past), cache),
            pl.BlockSpec((nb, ATT_HEADS, ATT_DH, n_past), cache),
            pl.BlockSpec((ATT_HEADS, t, n_past), const),
            pl.BlockSpec((ATT_HEADS, t, t), const),
        ],
        out_specs=pl.BlockSpec((nb * t, ATT_W), rows),
        out_shape=jax.ShapeDtypeStruct((batch * t, ATT_W), BF16),
        compiler_params=_cparams(("arbitrary",)),
        name="attn_step",
    )(q2d, kn2d, vn2d, kt, vt, bias[:, :, :n_past], bias[:, :, n_past:])


def _merge_kernel(xn_ref, oa_ref, ob_ref, ga_ref, gb_ref, wa_ref, wb_ref, wo_ref,
                  g1_ref, b1_ref, x1_ref, *, parts):
    sub = xn_ref.shape[0] // parts
    for s in range(parts):
        r = pl.ds(s * sub, sub)
        ya = jnp.dot(oa_ref[r, :], wa_ref[...], preferred_element_type=F32)
        yb = jnp.dot(ob_ref[r, :], wb_ref[...], preferred_element_type=F32)
        mixed = (_sigmoid(ga_ref[r, :].astype(F32)) * ya + _sigmoid(gb_ref[r, :].astype(F32)) * yb)
        y = DEEPNORM_ALPHA * xn_ref[r, :] + _dot(mixed, wo_ref[...])
        x1_ref[r, :] = _layernorm_rows(y, g1_ref[...], b1_ref[...])


def _merge(xn2d, oa, ob, h2d, wa, wb, wo, ln1_g, ln1_b, *, tm):
    rows = xn2d.shape[0]
    full = lambda i: (0, 0)
    return pl.pallas_call(
        functools.partial(_merge_kernel, parts=2 if tm >= 512 else 1),
        grid=(rows // tm,),
        in_specs=[
            pl.BlockSpec((tm, D_MODEL), lambda i: (i, 0)),
            pl.BlockSpec((tm, GDN_W), lambda i: (i, 0)),
            pl.BlockSpec((tm, ATT_W), lambda i: (i, 0)),
            pl.BlockSpec((tm, D_MODEL), lambda i: (i, COL["gate_a"] // D_MODEL)),
            pl.BlockSpec((tm, D_MODEL), lambda i: (i, COL["gate_b"] // D_MODEL)),
            pl.BlockSpec((GDN_W, D_MODEL), full),
            pl.BlockSpec((ATT_W, D_MODEL), full),
            pl.BlockSpec((D_MODEL, D_MODEL), full),
            pl.BlockSpec((1, D_MODEL), full),
            pl.BlockSpec((1, D_MODEL), full),
        ],
        out_specs=pl.BlockSpec((tm, D_MODEL), lambda i: (i, 0)),
        out_shape=jax.ShapeDtypeStruct((rows, D_MODEL), F32),
        compiler_params=_cparams(("arbitrary",)),
        name="merge",
    )(xn2d, oa, ob, h2d, h2d, wa, wb, wo, ln1_g, ln1_b)


def _gelu_tanh(x):
    return 0.5 * x * (1.0 + jnp.tanh(math.sqrt(2.0 / math.pi) * (x + 0.044715 * (x * x * x))))


def _ffn_kernel(x1_ref, pe_ref, wup_ref, cw_ref, wdn_ref, wple_ref, wpg_ref, g2_ref, b2_ref,
                finit_ref, y_ref, fstate_ref, gp_ref, act_ref, *, tm, fc, ns):
    n = pl.program_id(1)
    pad = SUBLANES
    streams = range(ns)

    @pl.when(n == 0)
    def _():
        gp_ref[:, 0:pad, :] = finit_ref[...]

    x1 = x1_ref[...]
    x1b = x1.astype(BF16)

    def up(k):
        c0 = k * fc
        gate = jnp.dot(x1b, wup_ref[:, c0:c0 + fc], preferred_element_type=F32)
        val = jnp.dot(x1b, wup_ref[:, D_FF + c0:D_FF + c0 + fc], preferred_element_type=F32)
        for s in streams:
            gp_ref[s, pad:pad + tm, c0:c0 + fc] = gate[s * tm:(s + 1) * tm]
        return gate, val

    nk = D_FF // fc
    nxt = up(0)
    ple = jnp.dot(pe_ref[...].astype(BF16), wple_ref[...], preferred_element_type=F32)
    for k in range(nk):
        c0 = k * fc
        gate, val = nxt
        if k + 1 < nk:
            nxt = up(k + 1)
        conv = gate * cw_ref[FFN_CONV - 1:FFN_CONV, c0:c0 + fc]
        for i in range(1, FFN_CONV):
            prev = [gp_ref[s, pl.ds(pad - i, tm), c0:c0 + fc] for s in streams]
            conv = conv + (prev[0] if ns == 1 else jnp.concatenate(prev, axis=0)) * cw_ref[
                FFN_CONV - 1 - i:FFN_CONV - i, c0:c0 + fc]
        act_ref[:, c0:c0 + fc] = (_gelu_tanh(conv) * val).astype(BF16)
    tail = gp_ref[:, tm:tm + pad, :]
    gp_ref[:, 0:pad, :] = tail
    fstate_ref[...] = tail

    down = jnp.dot(act_ref[...], wdn_ref[...], preferred_element_type=F32)
    r = DEEPNORM_ALPHA * x1 + down
    gate_p = _sigmoid(_dot(r, wpg_ref[...]))
    r = r + gate_p * ple
    y_ref[...] = _layernorm_rows(r, g2_ref[...], b2_ref[...])


def _ffn(x1, pe, wup, conv_w, wdn, wple, wpg, ln2_g, ln2_b, finit, *, batch, t, tm, fc, ns):
    nt = t // tm
    assert batch % ns == 0 and (ns == 1 or nt == 1)
    rows = batch * t
    tile = ns * tm
    full = lambda b, n: (0, 0)
    return pl.pallas_call(
        functools.partial(_ffn_kernel, tm=tm, fc=fc, ns=ns),
        grid=(batch // ns, nt),
        in_specs=[
            pl.BlockSpec((tile, D_MODEL), lambda b, n: (b * nt + n, 0)),
            pl.BlockSpec((tile, PLE_DIM), lambda b, n: (b * nt + n, 0)),
            pl.BlockSpec((D_MODEL, 2 * D_FF), full, pipeline_mode=pl.Buffered(1)),
            pl.BlockSpec((FFN_CONV, D_FF), full),
            pl.BlockSpec((D_FF, D_MODEL), full, pipeline_mode=pl.Buffered(1)),
            pl.BlockSpec((PLE_DIM, D_MODEL), full),
            pl.BlockSpec((D_MODEL, D_MODEL), full),
            pl.BlockSpec((1, D_MODEL), full),
            pl.BlockSpec((1, D_MODEL), full),
            pl.BlockSpec((ns, SUBLANES, D_FF), lambda b, n: (b, 0, 0)),
        ],
        out_specs=[
            pl.BlockSpec((tile, D_MODEL), lambda b, n: (b * nt + n, 0)),
            pl.BlockSpec((ns, SUBLANES, D_FF), lambda b, n: (b, 0, 0)),
        ],
        out_shape=[
            jax.ShapeDtypeStruct((rows, D_MODEL), F32),
            jax.ShapeDtypeStruct((batch, SUBLANES, D_FF), F32),
        ],
        scratch_shapes=[pltpu.VMEM((ns, SUBLANES + tm, D_FF), F32), pltpu.VMEM((tile, D_FF), BF16)],
        compiler_params=_cparams(("arbitrary", "arbitrary")),
        name="ffn",
    )(x1, pe, wup, conv_w, wdn, wple, wpg, ln2_g, ln2_b, finit)


def _rel_bias(table, n_q, n_past, n_k):
    span = n_q + n_k - 1
    rel = n_q - 1 + n_past - jnp.arange(span)
    rev = table[:, jnp.clip(rel, -REL_CLIP, REL_CLIP) + REL_CLIP].astype(F32)
    heads = table.shape[0]
    period = jnp.pad(rev, ((0, 0), (0, 1)))
    skew = jnp.tile(period, (1, n_q))[:, :n_q * span].reshape(heads, n_q, span)
    return skew[:, :, n_q - 1:n_q - 1 + n_k]


def _pad_state(state, rows):
    b, r, ch = state.shape
    return jnp.concatenate([jnp.zeros((b, rows - r, ch), F32), state.astype(F32)], axis=1)


def _layer(x, pe, gconv_prev, s_gdn, k_prev, v_prev, fconv_prev, wts, *, tm_proj, tm_merge, tm_ffn, gdn_nb,
           att_chunks=1, ffn_ns=1):
    (ln_in_g, ln_in_b, w_proj, gdn_conv_w, a_log, dt_bias, norm_w, rel_table, w_a, w_b, w_o,
     ln1_g, ln1_b, w_up, ffn_conv_w, w_dn, w_ple, w_pg, ln2_g, ln2_b) = wts
    b, t, _ = x.shape
    rows = b * t
    x2d = x.reshape(rows, D_MODEL)

    prompt = k_prev is None
    tm_proj = min(tm_proj, t if prompt else rows)
    pad_rows = tm_proj if prompt else 0
    assert pad_rows >= BAND or not prompt
    h2d, ab, q_b, k_b, v_b, gconv_new, xn2d = _in_proj(
        x2d, ln_in_g, ln_in_b, *w_proj, gdn_conv_w, _pad_state(gconv_prev, SUBLANES), tm=tm_proj,
        t=t if prompt else rows, pad_rows=pad_rows, ns=1 if prompt else b)

    o_a, s_new = _gdn(h2d, ab, a_log, dt_bias, norm_w, s_gdn.astype(F32), batch=b, t=t,
                      nb=math.gcd(gdn_nb, b))

    k_b = k_b.reshape(b, -1, ATT_W)
    v_b = v_b.reshape(b, -1, ATT_W)
    if prompt:
        tq = att_chunks * CHUNK
        width = BAND + tq
        own = (jnp.arange(tq) // CHUNK)[:, None] * CHUNK
        key = jnp.arange(width)[None, :]
        in_band = (key >= own) & (key < own + BAND + CHUNK)
        bias = jnp.where(in_band, _rel_bias(rel_table, tq, BAND, width) * LOG2E, -jnp.inf)
        o_b = _attn(q_b, k_b, v_b, bias, batch=b, t=t, tq=tq, width=width, pad=BAND,
                    koff=pad_rows - BAND)
        keep = min(BAND, t)
        k_new, v_new = k_b[:, pad_rows + t - keep:], v_b[:, pad_rows + t - keep:]
    else:
        n_past = k_prev.shape[1]
        bias = _rel_bias(rel_table, t, n_past, n_past + t) * LOG2E
        o_b = _attn_step(q_b, k_b.reshape(rows, ATT_W), v_b.reshape(rows, ATT_W),
                         jnp.transpose(k_prev, (0, 2, 3, 1)).astype(F32),
                         jnp.transpose(v_prev, (0, 2, 3, 1)).astype(F32), bias, batch=b, t=t,
                         nb=math.gcd(4, b))
        k_new, v_new = k_b, v_b

    x1 = _merge(xn2d, o_a, o_b, h2d, w_a, w_b, w_o, ln1_g, ln1_b,
                tm=min(tm_merge, rows))

    y, fconv_new = _ffn(x1, pe.reshape(rows, PLE_DIM), w_up, ffn_conv_w, w_dn, w_ple, w_pg,
                        ln2_g, ln2_b, _pad_state(fconv_prev, SUBLANES), batch=b, t=t,
                        tm=min(tm_ffn, t), fc=256, ns=math.gcd(ffn_ns, b))

    keep_k = k_new.shape[1]
    return (y.reshape(b, t, D_MODEL),
            gconv_new[:, SUBLANES - (GDN_CONV - 1):],
            s_new,
            k_new.astype(F32).reshape(b, keep_k, ATT_HEADS, ATT_DH),
            v_new.astype(F32).reshape(b, keep_k, ATT_HEADS, ATT_DH),
            fconv_new[:, SUBLANES - (FFN_CONV - 1):])


def _pad_lanes(v, width=LANES):
    return jnp.pad(v.astype(F32), (0, width - v.shape[0])).reshape(1, width)


def kernel(x_prompt, x_sample, state_gdn_conv, state_gdn, cache_attn_k, cache_attn_v, state_ffn_conv,
           p_prompt, p_sample, ln_in_g, ln_in_b, w_in, gdn_conv_w, gdn_a_log, gdn_dt_bias, gdn_norm_w,
           att_rel_bias, w_branch_a, w_branch_b, w_out, ln1_g, ln1_b, w_ffn_up, ffn_conv_w, w_ffn_down,
           w_ple, w_ple_gate, ln2_g, ln2_b):
    depth = w_in.shape[0]
    assert depth == 1
    bp = x_prompt.shape[0]
    row = lambda v: v.reshape(1, -1).astype(F32)

    xp, xs = x_prompt, x_sample
    new_p, new_s = [], []
    for i in range(depth):
        wi = w_in[i]
        o_ab = 4 * GDN_W
        o_att = o_ab + 2 * GDN_HEADS
        o_gate = o_att + 3 * ATT_W
        wt = wi.T.astype(BF16)
        src = {"q": 0, "k": GDN_W, "v": 2 * GDN_W, "z": 3 * GDN_W, "gate_a": o_gate, "gate_b": o_gate + D_MODEL}
        w_proj = (jnp.concatenate([wt[src[name]:src[name] + GDN_W] for name in H_SECTIONS], axis=0),
                  wt[o_att:o_gate],
                  jnp.pad(wt[o_ab:o_att], ((0, LANES - 2 * GDN_HEADS), (0, 0))))
        wts = (row(ln_in_g), row(ln_in_b), w_proj, gdn_conv_w[i].astype(F32),
               _pad_lanes(gdn_a_log[i]), _pad_lanes(gdn_dt_bias[i]), row(gdn_norm_w[i]), att_rel_bias[i],
               w_branch_a[i].astype(BF16), w_branch_b[i].astype(BF16), w_out[i].astype(BF16),
               row(ln1_g[i]), row(ln1_b[i]), w_ffn_up[i].astype(BF16), ffn_conv_w[i].astype(F32),
               w_ffn_down[i].astype(BF16), w_ple[i].astype(BF16), w_ple_gate[i].astype(BF16),
               row(ln2_g[i]), row(ln2_b[i]))
        xp, st_p = _unpack(_layer(xp, p_prompt[i],
                                  jnp.zeros((bp, GDN_CONV - 1, GDN_CONV_CH), F32),
                                  jnp.zeros((bp, GDN_HEADS, GDN_DK, GDN_DV), F32),
                                  None, None,
                                  jnp.zeros((bp, FFN_CONV - 1, D_FF), F32), wts,
                                  tm_proj=1024, tm_merge=1024, tm_ffn=512, gdn_nb=4, att_chunks=4))
        xs, st_s = _unpack(_layer(xs, p_sample[i], state_gdn_conv[i], state_gdn[i], cache_attn_k[i],
                                  cache_attn_v[i], state_ffn_conv[i], wts,
                                  tm_proj=256, tm_merge=256, tm_ffn=256, gdn_nb=4, ffn_ns=16))
        new_p.append(st_p)
        new_s.append(st_s)
    p_gconv, p_gdn, p_k, p_v, p_fconv = [jnp.stack(a) for a in zip(*new_p)]
    s_gconv, s_gdn, s_k, s_v, s_fconv = [jnp.stack(a) for a in zip(*new_s)]
    return (xp, xs, p_gconv, p_gdn, p_k, p_v, p_fconv, s_gconv, s_gdn, s_k, s_v, s_fconv)


def _unpack(res):
    return res[0], res[1:]
```

```python
import functools
import math

import jax
import jax.numpy as jnp
from jax import lax
from jax.experimental import pallas as pl
from jax.experimental.pallas import tpu as pltpu

F32 = jnp.float32
BF16 = jnp.bfloat16

D_MODEL = 1024
CHUNK = 64
BAND = 512
GDN_HEADS = 8
GDN_DK = 128
GDN_DV = 128
GDN_CONV = 4
ATT_HEADS = 8
ATT_DH = 64
REL_CLIP = 128
D_FF = 2816
FFN_CONV = 3
PLE_DIM = 256
DEEPNORM_ALPHA = 2.0 ** 0.25
LN_EPS = 1e-5
RMS_EPS = 1e-6
L2_EPS = 1e-6

GDN_W = GDN_HEADS * GDN_DK
GDN_CONV_CH = 3 * GDN_W
ATT_W = ATT_HEADS * ATT_DH
LANES = 128
SUBLANES = 8

H_SECTIONS = ("q", "gate_a", "k", "gate_b", "v", "z")
COL = {name: idx * GDN_W for idx, name in enumerate(H_SECTIONS)}
CONV_COL = {"q": 0, "k": GDN_W, "v": 2 * GDN_W}
H_MAIN = len(H_SECTIONS) * GDN_W
W_ROW = {"q": 0, "k": GDN_W, "v": 2 * GDN_W, "z": 3 * GDN_W, "ab": 4 * GDN_W}
W_ROW["att"] = W_ROW["ab"] + 2 * GDN_HEADS
W_ROW["gate_a"] = W_ROW["att"] + 3 * ATT_W
W_ROW["gate_b"] = W_ROW["gate_a"] + D_MODEL
D_IN = W_ROW["gate_b"] + D_MODEL

LOG2E = math.log2(math.e)
ATT_QSCALE = ATT_DH ** -0.5 * LOG2E

VMEM_LIMIT = 56 * 1024 * 1024


def _cparams(sem):
    return pltpu.CompilerParams(dimension_semantics=sem, vmem_limit_bytes=VMEM_LIMIT)


def _sigmoid(x):
    return 1.0 / (1.0 + jnp.exp(-x))


def _silu(x):
    return x * _sigmoid(x)


def _layernorm_rows(x, g, b):
    mu = jnp.mean(x, axis=-1, keepdims=True)
    xc = x - mu
    var = jnp.mean(xc * xc, axis=-1, keepdims=True)
    return xc * lax.rsqrt(var + LN_EPS) * g + b


def _dot(a, b):
    return jnp.dot(a.astype(BF16), b.astype(BF16), preferred_element_type=F32)


def _dot_nt(a, b):
    return lax.dot_general(a.astype(BF16), b.astype(BF16), (((1,), (1,)), ((), ())),
                           preferred_element_type=F32)


def _dot_tn(a, b):
    return lax.dot_general(a.astype(BF16), b.astype(BF16), (((0,), (0,)), ((), ())),
                           preferred_element_type=F32)


def _split3(x):
    x1 = x.astype(BF16)
    r1 = x - x1.astype(F32)
    x2 = r1.astype(BF16)
    x3 = (r1 - x2.astype(F32)).astype(BF16)
    return x1, x2, x3


CONV_COLS = 256


def _in_proj_kernel(x_ref, g_ref, b_ref, wt_ref, cw_ref, cinit_ref,
                    h_ref, ab_ref, q_ref, k_ref, v_ref, cs_ref, xnf_ref,
                    xn_ref, carry_ref, xp0_ref, xp1_ref, *, sub, n_main, ns, pads, period):
    i = pl.program_id(0)
    j = pl.program_id(1)
    row_step = lax.rem(i, period)
    is_data = row_step >= pads
    tm = x_ref.shape[0]
    tn = h_ref.shape[1]
    tr = tm // ns
    pad = SUBLANES
    streams = range(ns)

    @pl.when(jnp.logical_and(j == 0, is_data))
    def _():
        def body(r_i, c):
            r = pl.multiple_of(r_i * sub, sub)
            xn = _layernorm_rows(x_ref[pl.ds(r, sub), :], g_ref[...], b_ref[...])
            xnf_ref[pl.ds(r, sub), :] = xn
            xn_ref[pl.ds(r, sub), :] = xn.astype(BF16)
            return c
        lax.fori_loop(0, tm // sub, body, 0)
        ab_ref[...] = _dot_nt(xn_ref[...], wt_ref[W_ROW["ab"]:W_ROW["ab"] + LANES, :])

    def project(g0):
        r0 = W_ROW[H_SECTIONS[g0 // GDN_W]] + g0 % GDN_W
        return _dot_nt(xn_ref[...], wt_ref[r0:r0 + CONV_COLS, :])

    def conv_silu(pre, cc, buf):
        outs = []
        for s in streams:
            buf[s, 0:pad, :] = carry_ref[s, :, cc:cc + CONV_COLS]
            buf[s, pad:pad + tr, :] = pre[s * tr:(s + 1) * tr]
        for s in streams:
            acc = buf[s, pad:pad + tr, :] * cw_ref[GDN_CONV - 1:GDN_CONV, cc:cc + CONV_COLS]
            for t in range(1, GDN_CONV):
                acc = acc + (buf[s, pl.ds(pad - t, tr), :]
                             * cw_ref[GDN_CONV - 1 - t:GDN_CONV - t, cc:cc + CONV_COLS])
            outs.append(acc)
            carry_ref[s, :, cc:cc + CONV_COLS] = buf[s, tr:tr + pad, :]
        return _silu(outs[0] if ns == 1 else jnp.concatenate(outs, axis=0))

    def main_step(step):
        base = step * tn
        kinds = [H_SECTIONS[(base + c0) // GDN_W] for c0 in range(0, tn, GDN_W)]
        convs = [k for k in kinds if k in CONV_COL]

        @pl.when(row_step == pads)
        def _():
            for k in convs:
                cc = CONV_COL[k]
                carry_ref[:, :, cc:cc + GDN_W] = cinit_ref[:, :, cc:cc + GDN_W]

        order = [sec * GDN_W + c for c in range(0, GDN_W, CONV_COLS) for sec in range(tn // GDN_W)]
        nxt = project(base + order[0])
        for pos, c0 in enumerate(order):
            g0 = base + c0
            kind = H_SECTIONS[g0 // GDN_W]
            pre = nxt
            if pos + 1 < len(order):
                nxt = project(base + order[pos + 1])
            if kind in CONV_COL:
                y = conv_silu(pre, CONV_COL[kind] + g0 % GDN_W, xp0_ref if pos % 2 == 0 else xp1_ref)
            elif kind == "z":
                y = _silu(pre)
            else:
                y = pre
            h_ref[:, c0:c0 + CONV_COLS] = y.astype(BF16)
        for k in convs:
            cc = CONV_COL[k]
            cs_ref[:, :, cc:cc + GDN_W] = carry_ref[:, :, cc:cc + GDN_W]

    for step in range(n_main):
        pl.when(jnp.logical_and(j == step, is_data))(functools.partial(main_step, step))

    if pads:
        @pl.when(jnp.logical_and(j == n_main, jnp.logical_not(is_data)))
        def _():
            k_ref[...] = jnp.zeros_like(k_ref)
            v_ref[...] = jnp.zeros_like(v_ref)

    @pl.when(jnp.logical_and(j == n_main, is_data))
    def _():
        for o_ref, c0, mult in ((q_ref, 0, ATT_QSCALE), (k_ref, ATT_W, None), (v_ref, 2 * ATT_W, None)):
            proj = _dot_nt(xn_ref[...], wt_ref[W_ROW["att"] + c0:W_ROW["att"] + c0 + ATT_W, :])
            o_ref[...] = (proj if mult is None else proj * mult).astype(BF16)


def _in_proj(x2d, ln_g, ln_b, wt, conv_w, conv_init, *, tm, t, pad_rows, ns):
    rows = x2d.shape[0]
    tn = 2 * GDN_W
    n_main = H_MAIN // tn
    n_streams = conv_init.shape[0]
    assert rows % tm == 0 and t % tm == 0 and pad_rows % tm == 0
    assert wt.shape == (D_IN, D_MODEL)
    assert (ns == 1 and rows == n_streams * t) or (rows == n_streams * (tm // ns) and rows == tm)
    sub = min(tm, 128)
    per_stream = t // tm
    pads = pad_rows // tm
    period = per_stream + pads
    n_steps = (rows // t) * period
    data = lambda i: (i // period) * per_stream + jnp.maximum(i % period - pads, 0)
    tile = lambda i, j: (data(i), 0)
    stream_block = lambda i, j: (i // period, 0, 0)
    const = lambda i, j: (0, 0)
    return pl.pallas_call(
        functools.partial(_in_proj_kernel, sub=sub, n_main=n_main, ns=ns, pads=pads, period=period),
        grid=(n_steps, n_main + 1),
        in_specs=[
            pl.BlockSpec((tm, D_MODEL), tile),
            pl.BlockSpec((1, D_MODEL), const),
            pl.BlockSpec((1, D_MODEL), const),
            pl.BlockSpec((D_IN, D_MODEL), const, pipeline_mode=pl.Buffered(1)),
            pl.BlockSpec((GDN_CONV, GDN_CONV_CH), const),
            pl.BlockSpec((ns, SUBLANES, GDN_CONV_CH), stream_block),
        ],
        out_specs=[
            pl.BlockSpec((tm, tn), lambda i, j: (data(i), jnp.where(i % period < pads, 0, jnp.minimum(j, n_main - 1)))),
            pl.BlockSpec((tm, LANES), tile),
            pl.BlockSpec((tm, ATT_W), tile),
            pl.BlockSpec((tm, ATT_W), lambda i, j: (i, 0)),
            pl.BlockSpec((tm, ATT_W), lambda i, j: (i, 0)),
            pl.BlockSpec((ns, SUBLANES, GDN_CONV_CH), stream_block),
            pl.BlockSpec((tm, D_MODEL), tile),
        ],
        out_shape=[
            jax.ShapeDtypeStruct((rows, H_MAIN), BF16),
            jax.ShapeDtypeStruct((rows, LANES), F32),
            jax.ShapeDtypeStruct((rows, ATT_W), BF16),
            jax.ShapeDtypeStruct((n_steps * tm, ATT_W), BF16),
            jax.ShapeDtypeStruct((n_steps * tm, ATT_W), BF16),
            jax.ShapeDtypeStruct((n_streams, SUBLANES, GDN_CONV_CH), F32),
            jax.ShapeDtypeStruct((rows, D_MODEL), F32),
        ],
        scratch_shapes=[
            pltpu.VMEM((tm, D_MODEL), BF16),
            pltpu.VMEM((ns, SUBLANES, GDN_CONV_CH), F32),
            pltpu.VMEM((ns, SUBLANES + tm // ns, CONV_COLS), F32),
            pltpu.VMEM((ns, SUBLANES + tm // ns, CONV_COLS), F32),
        ],
        compiler_params=_cparams(("arbitrary", "arbitrary")),
        name="in_proj",
    )(x2d, ln_g, ln_b, wt, conv_w, conv_init)


def _gdn_kernel(q_ref, k_ref, v_ref, z_ref, ab_ref, alog_ref, dtb_ref, nw_ref, s0_ref,
                o_ref, s_ref, *, c, nb):
    n = pl.program_id(1)
    heads = range(GDN_HEADS)
    pairs = [(bi, h) for bi in range(nb) for h in heads]

    @pl.when(n == 0)
    def _():
        s_ref[...] = s0_ref[...]

    row = lax.broadcasted_iota(jnp.int32, (c, c), 0)
    col = lax.broadcasted_iota(jnp.int32, (c, c), 1)
    causal = row >= col
    strict = row > col
    eye = jnp.where(row == col, 1.0, 0.0).astype(F32)
    tri = jnp.where(causal, 1.0, 0.0).astype(BF16)

    dec_all, dec_t, beta_all = [], [], []
    for bi in range(nb):
        ab = ab_ref[bi]
        sp_in = ab + dtb_ref[...]
        softplus = jnp.maximum(sp_in, 0.0) + jnp.log(1.0 + jnp.exp(-jnp.abs(sp_in)))
        g_all = -jnp.exp(alog_ref[...]) * softplus
        g1, g2, g3 = _split3(g_all)
        dec = (jnp.dot(tri, g1, preferred_element_type=F32)
               + (jnp.dot(tri, g2, preferred_element_type=F32)
                  + jnp.dot(tri, g3, preferred_element_type=F32)))
        dec_all.append(dec)
        dec_t.append(dec.T)
        beta_all.append(_sigmoid(ab))

    def cols(ref, bi, h, width=GDN_DK):
        return ref[bi, :, h * width:(h + 1) * width].astype(F32)

    def l2norm(x, mult):
        return x * (lax.rsqrt(jnp.sum(x * x, axis=-1, keepdims=True) + L2_EPS) * mult)

    qn = [l2norm(cols(q_ref, bi, h), GDN_DK ** -0.5) for bi, h in pairs]
    kn = [l2norm(cols(k_ref, bi, h), 1.0) for bi, h in pairs]
    xv = [cols(v_ref, bi, h, GDN_DV) for bi, h in pairs]
    dcol = [dec_all[bi][:, h:h + 1] for bi, h in pairs]
    beta = [beta_all[bi][:, GDN_HEADS + h:GDN_HEADS + h + 1] for bi, h in pairs]
    lmask = [jnp.exp(jnp.where(causal, dec_all[bi][:, h:h + 1] - dec_t[bi][h:h + 1, :], -jnp.inf))
             for bi, h in pairs]
    idx = range(len(pairs))
    kb = [kn[i] * beta[i] for i in idx]
    e_d = [jnp.exp(d) for d in dcol]
    gram = [_dot_nt(jnp.concatenate([kb[i], qn[i]], axis=0), kn[i]) for i in idx]
    m = [jnp.where(strict, gram[i][:c] * lmask[i], 0.0) for i in idx]
    qk = [gram[i][c:] * lmask[i] for i in idx]
    p = [eye - m[i] for i in idx]
    mk = m
    for _ in range(int(math.log2(c)) - 1):
        mk = [_dot(x, x) for x in mk]
        p = [p[i] + _dot(p[i], mk[i]) for i in idx]
    sol = [_dot(p[i], jnp.concatenate([xv[i] * beta[i], kb[i] * e_d[i]], axis=1)) for i in idx]
    s_old = [s_ref[bi, h] for bi, h in pairs]
    ws_qs = [_dot(jnp.concatenate([sol[i][:, GDN_DV:], qn[i] * e_d[i]], axis=0), s_old[i])
             for i in idx]
    v_new = [sol[i][:, :GDN_DV] - ws_qs[i][:c] for i in idx]
    dl = [d[c - 1:c, :] for d in dcol]
    kd = [kn[i] * jnp.exp(dl[i] - dcol[i]) for i in idx]
    s_upd = [_dot_tn(kd[i], v_new[i]) for i in idx]
    o_in = [_dot(qk[i], v_new[i]) for i in idx]
    for i, (bi, h) in enumerate(pairs):
        s_ref[bi, h] = s_old[i] * jnp.exp(dl[i]) + s_upd[i]
        o = ws_qs[i][c:] + o_in[i]
        o = (o * lax.rsqrt(jnp.mean(o * o, axis=-1, keepdims=True) + RMS_EPS) * nw_ref[...]
             * cols(z_ref, bi, h, GDN_DV))
        o_ref[bi, :, pl.ds(h * GDN_DV, GDN_DV)] = o.astype(BF16)


def _gdn(h2d, ab, a_log, dt_bias, norm_w, s0, *, batch, t, nb):
    c = min(CHUNK, t)
    nc = t // c
    assert batch % nb == 0
    h3d = h2d.reshape(batch, t, H_MAIN)
    ab3d = ab.reshape(batch, t, LANES)
    chunk = lambda col: (lambda b, n: (b, n, col))
    o_a, s_new = pl.pallas_call(
        functools.partial(_gdn_kernel, c=c, nb=nb),
        grid=(batch // nb, nc),
        in_specs=[
            pl.BlockSpec((nb, c, GDN_W), chunk(COL["q"] // GDN_W)),
            pl.BlockSpec((nb, c, GDN_W), chunk(COL["k"] // GDN_W)),
            pl.BlockSpec((nb, c, GDN_W), chunk(COL["v"] // GDN_W)),
            pl.BlockSpec((nb, c, GDN_W), chunk(COL["z"] // GDN_W)),
            pl.BlockSpec((nb, c, LANES), chunk(0)),
            pl.BlockSpec((1, LANES), lambda b, n: (0, 0)),
            pl.BlockSpec((1, LANES), lambda b, n: (0, 0)),
            pl.BlockSpec((1, GDN_DV), lambda b, n: (0, 0)),
            pl.BlockSpec((nb, GDN_HEADS, GDN_DK, GDN_DV), lambda b, n: (b, 0, 0, 0)),
        ],
        out_specs=[
            pl.BlockSpec((nb, c, GDN_W), chunk(0)),
            pl.BlockSpec((nb, GDN_HEADS, GDN_DK, GDN_DV), lambda b, n: (b, 0, 0, 0)),
        ],
        out_shape=[
            jax.ShapeDtypeStruct((batch, t, GDN_W), BF16),
            jax.ShapeDtypeStruct((batch, GDN_HEADS, GDN_DK, GDN_DV), F32),
        ],
        compiler_params=_cparams(("arbitrary", "arbitrary")),
        name="gdn",
    )(h3d, h3d, h3d, h3d, ab3d, a_log, dt_bias, norm_w, s0)
    return o_a.reshape(batch * t, GDN_W), s_new


def _attn_kernel(q_ref, k_ref, v_ref, rev_ref, o_ref, bias_ref, *, tq, width, pad, koff):
    b = pl.program_id(0)
    n = pl.program_id(1)
    start = pl.multiple_of(koff + n * tq, tq)
    heads = range(ATT_HEADS)
    cols = [slice(h * ATT_DH, (h + 1) * ATT_DH) for h in heads]
    has_pad_keys = n * tq < pad

    @pl.when(jnp.logical_and(b == 0, n == 0))
    def _():
        row = lax.broadcasted_iota(jnp.int32, (tq, width), 0)
        key = lax.broadcasted_iota(jnp.int32, (tq, width), 1)
        own = row - lax.rem(row, CHUNK)
        in_band = jnp.logical_and(key >= own, key < own + pad + CHUNK)
        for h in heads:
            rows = pltpu.roll(jnp.broadcast_to(rev_ref[h], (tq, tq + width)), 0, axis=1,
                              stride=1, stride_axis=0)
            bias_ref[h] = jnp.where(in_band, rows[:, tq:], -jnp.inf)

    def tile(mask_pad_keys):
        kwin = k_ref[0, pl.ds(start, width), :]
        vwin = v_ref[0, pl.ds(start, width), :]
        s = [_dot_nt(q_ref[:, cols[h]], kwin[:, cols[h]]) + bias_ref[h] for h in heads]
        if mask_pad_keys:
            kpos = n * tq - pad + lax.broadcasted_iota(jnp.int32, (tq, width), 1)
            s = [jnp.where(kpos >= 0, x, -jnp.inf) for x in s]
        p = [jnp.exp2(x - jnp.max(x, axis=-1, keepdims=True)) for x in s]
        l = [jnp.sum(x, axis=-1, keepdims=True) for x in p]
        pv = [_dot(p[h], vwin[:, cols[h]]) for h in heads]
        o_ref[...] = jnp.concatenate([pv[h] / l[h] for h in heads], axis=1).astype(BF16)

    pl.when(has_pad_keys)(functools.partial(tile, True))
    pl.when(jnp.logical_not(has_pad_keys))(functools.partial(tile, False))


def _attn(q2d, k3d, v3d, rev, *, batch, t, tq, width, pad, koff):
    nq = t // tq
    tk = k3d.shape[1]
    assert koff % tq == 0 and koff + (nq - 1) * tq + width <= tk and rev.shape[-1] == tq + width
    return pl.pallas_call(
        functools.partial(_attn_kernel, tq=tq, width=width, pad=pad, koff=koff),
        grid=(batch, nq),
        in_specs=[
            pl.BlockSpec((tq, ATT_W), lambda b, n: (b * nq + n, 0)),
            pl.BlockSpec((1, tk, ATT_W), lambda b, n: (b, 0, 0)),
            pl.BlockSpec((1, tk, ATT_W), lambda b, n: (b, 0, 0)),
            pl.BlockSpec((ATT_HEADS, 1, tq + width), lambda b, n: (0, 0, 0)),
        ],
        out_specs=pl.BlockSpec((tq, ATT_W), lambda b, n: (b * nq + n, 0)),
        out_shape=jax.ShapeDtypeStruct((batch * t, ATT_W), BF16),
        scratch_shapes=[pltpu.VMEM((ATT_HEADS, tq, width), F32)],
        compiler_params=_cparams(("arbitrary", "arbitrary")),
        name="attn",
    )(q2d, k3d, v3d, rev)


def _attn_step_kernel(q_ref, kn_ref, vn_ref, kt_ref, vt_ref, bc_ref, bn_ref, o_ref, *, nb, t):
    heads = range(ATT_HEADS)
    cols = [slice(h * ATT_DH, (h + 1) * ATT_DH) for h in heads]
    chains = [(bi, h) for bi in range(nb) for h in heads]
    rows = [pl.ds(bi * t, t) for bi in range(nb)]
    q = [q_ref[rows[bi], cols[h]] for bi, h in chains]
    s_c = [_dot(q[i], kt_ref[bi, h]) + bc_ref[h] for i, (bi, h) in enumerate(chains)]
    s_n = [_dot_nt(q[i], kn_ref[rows[bi], cols[h]]) + bn_ref[h] for i, (bi, h) in enumerate(chains)]
    idx = range(len(chains))
    mx = [jnp.maximum(jnp.max(s_c[i], axis=-1, keepdims=True), jnp.max(s_n[i], axis=-1, keepdims=True))
          for i in idx]
    p_c = [jnp.exp2(s_c[i] - mx[i]) for i in idx]
    p_n = [jnp.exp2(s_n[i] - mx[i]) for i in idx]
    l = [jnp.sum(p_c[i], axis=-1, keepdims=True) + jnp.sum(p_n[i], axis=-1, keepdims=True) for i in idx]
    pv = [_dot_nt(p_c[i], vt_ref[bi, h]) + _dot(p_n[i], vn_ref[rows[bi], cols[h]])
          for i, (bi, h) in enumerate(chains)]
    for bi in range(nb):
        o_ref[rows[bi], :] = jnp.concatenate(
            [pv[bi * ATT_HEADS + h] / l[bi * ATT_HEADS + h] for h in heads], axis=1).astype(BF16)


def _attn_step(q2d, kn2d, vn2d, kt, vt, bias, *, batch, t, nb):
    n_past = kt.shape[-1]
    assert batch % nb == 0
    rows = lambda b: (b, 0)
    cache = lambda b: (b, 0, 0, 0)
    const = lambda b: (0, 0, 0)
    return pl.pallas_call(
        functools.partial(_attn_step_kernel, nb=nb, t=t),
        grid=(batch // nb,),
        in_specs=[
            pl.BlockSpec((nb * t, ATT_W), rows),
            pl.BlockSpec((nb * t, ATT_W), rows),
            pl.BlockSpec((nb * t, ATT_W), rows),
            pl.BlockSpec((nb, ATT_HEADS, ATT_DH, n_past), cache),
            pl.BlockSpec((nb, ATT_HEADS, ATT_DH, n_past), cache),
            pl.BlockSpec((ATT_HEADS, t, n_past), const),
            pl.BlockSpec((ATT_HEADS, t, t), const),
        ],
        out_specs=pl.BlockSpec((nb * t, ATT_W), rows),
        out_shape=jax.ShapeDtypeStruct((batch * t, ATT_W), BF16),
        compiler_params=_cparams(("arbitrary",)),
        name="attn_step",
    )(q2d, kn2d, vn2d, kt, vt, bias[:, :, :n_past], bias[:, :, n_past:])


def _merge_kernel(xn_ref, oa_ref, ob_ref, ga_ref, gb_ref, wa_ref, wb_ref, wo_ref,
                  g1_ref, b1_ref, x1_ref, *, parts):
    sub = xn_ref.shape[0] // parts
    for s in range(parts):
        r = pl.ds(s * sub, sub)
        ya = jnp.dot(oa_ref[r, :], wa_ref[...], preferred_element_type=F32)
        yb = jnp.dot(ob_ref[r, :], wb_ref[...], preferred_element_type=F32)
        mixed = (_sigmoid(ga_ref[r, :].astype(F32)) * ya + _sigmoid(gb_ref[r, :].astype(F32)) * yb)
        y = DEEPNORM_ALPHA * xn_ref[r, :] + _dot(mixed, wo_ref[...])
        x1_ref[r, :] = _layernorm_rows(y, g1_ref[...], b1_ref[...])


def _merge(xn2d, oa, ob, h2d, wa, wb, wo, ln1_g, ln1_b, *, tm):
    rows = xn2d.shape[0]
    full = lambda i: (0, 0)
    return pl.pallas_call(
        functools.partial(_merge_kernel, parts=2 if tm >= 512 else 1),
        grid=(rows // tm,),
        in_specs=[
            pl.BlockSpec((tm, D_MODEL), lambda i: (i, 0)),
            pl.BlockSpec((tm, GDN_W), lambda i: (i, 0)),
            pl.BlockSpec((tm, ATT_W), lambda i: (i, 0)),
            pl.BlockSpec((tm, D_MODEL), lambda i: (i, COL["gate_a"] // D_MODEL)),
            pl.BlockSpec((tm, D_MODEL), lambda i: (i, COL["gate_b"] // D_MODEL)),
            pl.BlockSpec((GDN_W, D_MODEL), full),
            pl.BlockSpec((ATT_W, D_MODEL), full),
            pl.BlockSpec((D_MODEL, D_MODEL), full),
            pl.BlockSpec((1, D_MODEL), full),
            pl.BlockSpec((1, D_MODEL), full),
        ],
        out_specs=pl.BlockSpec((tm, D_MODEL), lambda i: (i, 0)),
        out_shape=jax.ShapeDtypeStruct((rows, D_MODEL), F32),
        compiler_params=_cparams(("arbitrary",)),
        name="merge",
    )(xn2d, oa, ob, h2d, h2d, wa, wb, wo, ln1_g, ln1_b)


def _gelu_tanh(x):
    return 0.5 * x * (1.0 + jnp.tanh(math.sqrt(2.0 / math.pi) * (x + 0.044715 * (x * x * x))))


def _ffn_kernel(x1_ref, pe_ref, wup_ref, cw_ref, wdn_ref, wple_ref, wpg_ref, g2_ref, b2_ref,
                finit_ref, y_ref, fstate_ref, gp_ref, act_ref, *, tm, fc, ns):
    n = pl.program_id(1)
    pad = SUBLANES
    streams = range(ns)

    @pl.when(n == 0)
    def _():
        gp_ref[:, 0:pad, :] = finit_ref[...]

    x1 = x1_ref[...]
    x1b = x1.astype(BF16)

    def up(k):
        c0 = k * fc
        gate = jnp.dot(x1b, wup_ref[:, c0:c0 + fc], preferred_element_type=F32)
        val = jnp.dot(x1b, wup_ref[:, D_FF + c0:D_FF + c0 + fc], preferred_element_type=F32)
        for s in streams:
            gp_ref[s, pad:pad + tm, c0:c0 + fc] = gate[s * tm:(s + 1) * tm]
        return gate, val

    nk = D_FF // fc
    nxt = up(0)
    ple = jnp.dot(pe_ref[...].astype(BF16), wple_ref[...], preferred_element_type=F32)
    for k in range(nk):
        c0 = k * fc
        gate, val = nxt
        if k + 1 < nk:
            nxt = up(k + 1)
        conv = gate * cw_ref[FFN_CONV - 1:FFN_CONV, c0:c0 + fc]
        for i in range(1, FFN_CONV):
            prev = [gp_ref[s, pl.ds(pad - i, tm), c0:c0 + fc] for s in streams]
            conv = conv + (prev[0] if ns == 1 else jnp.concatenate(prev, axis=0)) * cw_ref[
                FFN_CONV - 1 - i:FFN_CONV - i, c0:c0 + fc]
        act_ref[:, c0:c0 + fc] = (_gelu_tanh(conv) * val).astype(BF16)
    tail = gp_ref[:, tm:tm + pad, :]
    gp_ref[:, 0:pad, :] = tail
    fstate_ref[...] = tail

    down = jnp.dot(act_ref[...], wdn_ref[...], preferred_element_type=F32)
    r = DEEPNORM_ALPHA * x1 + down
    gate_p = _sigmoid(_dot(r, wpg_ref[...]))
    r = r + gate_p * ple
    y_ref[...] = _layernorm_rows(r, g2_ref[...], b2_ref[...])


def _ffn(x1, pe, wup, conv_w, wdn, wple, wpg, ln2_g, ln2_b, finit, *, batch, t, tm, fc, ns):
    nt = t // tm
    assert batch % ns == 0 and (ns == 1 or nt == 1)
    rows = batch * t
    tile = ns * tm
    full = lambda b, n: (0, 0)
    return pl.pallas_call(
        functools.partial(_ffn_kernel, tm=tm, fc=fc, ns=ns),
        grid=(batch // ns, nt),
        in_specs=[
            pl.BlockSpec((tile, D_MODEL), lambda b, n: (b * nt + n, 0)),
            pl.BlockSpec((tile, PLE_DIM), lambda b, n: (b * nt + n, 0)),
            pl.BlockSpec((D_MODEL, 2 * D_FF), full, pipeline_mode=pl.Buffered(1)),
            pl.BlockSpec((FFN_CONV, D_FF), full),
            pl.BlockSpec((D_FF, D_MODEL), full, pipeline_mode=pl.Buffered(1)),
            pl.BlockSpec((PLE_DIM, D_MODEL), full),
            pl.BlockSpec((D_MODEL, D_MODEL), full),
            pl.BlockSpec((1, D_MODEL), full),
            pl.BlockSpec((1, D_MODEL), full),
            pl.BlockSpec((ns, SUBLANES, D_FF), lambda b, n: (b, 0, 0)),
        ],
        out_specs=[
            pl.BlockSpec((tile, D_MODEL), lambda b, n: (b * nt + n, 0)),
            pl.BlockSpec((ns, SUBLANES, D_FF), lambda b, n: (b, 0, 0)),
        ],
        out_shape=[
            jax.ShapeDtypeStruct((rows, D_MODEL), F32),
            jax.ShapeDtypeStruct((batch, SUBLANES, D_FF), F32),
        ],
        scratch_shapes=[pltpu.VMEM((ns, SUBLANES + tm, D_FF), F32), pltpu.VMEM((tile, D_FF), BF16)],
        compiler_params=_cparams(("arbitrary", "arbitrary")),
        name="ffn",
    )(x1, pe, wup, conv_w, wdn, wple, wpg, ln2_g, ln2_b, finit)


def _rel_bias(table, n_q, n_past, n_k):
    span = n_q + n_k - 1
    rel = n_q - 1 + n_past - jnp.arange(span)
    rev = table[:, jnp.clip(rel, -REL_CLIP, REL_CLIP) + REL_CLIP].astype(F32)
    heads = table.shape[0]
    period = jnp.pad(rev, ((0, 0), (0, 1)))
    skew = jnp.tile(period, (1, n_q))[:, :n_q * span].reshape(heads, n_q, span)
    return skew[:, :, n_q - 1:n_q - 1 + n_k]


def _pad_state(state, rows):
    b, r, ch = state.shape
    return jnp.concatenate([jnp.zeros((b, rows - r, ch), F32), state.astype(F32)], axis=1)


def _layer(x, pe, gconv_prev, s_gdn, k_prev, v_prev, fconv_prev, wts, *, tm_proj, tm_merge, tm_ffn, gdn_nb,
           att_chunks=1, ffn_ns=1):
    (ln_in_g, ln_in_b, w_proj, gdn_conv_w, a_log, dt_bias, norm_w, rel_table, w_a, w_b, w_o,
     ln1_g, ln1_b, w_up, ffn_conv_w, w_dn, w_ple, w_pg, ln2_g, ln2_b) = wts
    b, t, _ = x.shape
    rows = b * t
    x2d = x.reshape(rows, D_MODEL)

    prompt = k_prev is None
    tm_proj = min(tm_proj, t if prompt else rows)
    pad_rows = tm_proj if prompt else 0
    assert pad_rows >= BAND or not prompt
    h2d, ab, q_b, k_b, v_b, gconv_new, xn2d = _in_proj(
        x2d, ln_in_g, ln_in_b, w_proj, gdn_conv_w, _pad_state(gconv_prev, SUBLANES), tm=tm_proj,
        t=t if prompt else rows, pad_rows=pad_rows, ns=1 if prompt else b)

    o_a, s_new = _gdn(h2d, ab, a_log, dt_bias, norm_w, s_gdn.astype(F32), batch=b, t=t,
                      nb=math.gcd(gdn_nb, b))

    k_b = k_b.reshape(b, -1, ATT_W)
    v_b = v_b.reshape(b, -1, ATT_W)
    if prompt:
        tq = att_chunks * CHUNK
        width = BAND + tq
        dist = tq + BAND - jnp.arange(tq + width)
        rev = rel_table[:, jnp.clip(dist, -REL_CLIP, REL_CLIP) + REL_CLIP].astype(F32) * LOG2E
        o_b = _attn(q_b, k_b, v_b, rev[:, None, :], batch=b, t=t, tq=tq, width=width, pad=BAND,
                    koff=pad_rows - BAND)
        keep = min(BAND, t)
        k_new, v_new = k_b[:, pad_rows + t - keep:], v_b[:, pad_rows + t - keep:]
    else:
        n_past = k_prev.shape[1]
        bias = _rel_bias(rel_table, t, n_past, n_past + t) * LOG2E
        o_b = _attn_step(q_b, k_b.reshape(rows, ATT_W), v_b.reshape(rows, ATT_W),
                         jnp.transpose(k_prev, (0, 2, 3, 1)).astype(F32),
                         jnp.transpose(v_prev, (0, 2, 3, 1)).astype(F32), bias, batch=b, t=t,
                         nb=math.gcd(4, b))
        k_new, v_new = k_b, v_b

    x1 = _merge(xn2d, o_a, o_b, h2d, w_a, w_b, w_o, ln1_g, ln1_b,
                tm=min(tm_merge, rows))

    y, fconv_new = _ffn(x1, pe.reshape(rows, PLE_DIM), w_up, ffn_conv_w, w_dn, w_ple, w_pg,
                        ln2_g, ln2_b, _pad_state(fconv_prev, SUBLANES), batch=b, t=t,
                        tm=min(tm_ffn, t), fc=256, ns=math.gcd(ffn_ns, b))

    keep_k = k_new.shape[1]
    return (y.reshape(b, t, D_MODEL),
            gconv_new[:, SUBLANES - (GDN_CONV - 1):],
            s_new,
            k_new.astype(F32).reshape(b, keep_k, ATT_HEADS, ATT_DH),
            v_new.astype(F32).reshape(b, keep_k, ATT_HEADS, ATT_DH),
            fconv_new[:, SUBLANES - (FFN_CONV - 1):])


def _pad_lanes(v, width=LANES):
    return jnp.pad(v.astype(F32), (0, width - v.shape[0])).reshape(1, width)


def kernel(x_prompt, x_sample, state_gdn_conv, state_gdn, cache_attn_k, cache_attn_v, state_ffn_conv,
           p_prompt, p_sample, ln_in_g, ln_in_b, w_in, gdn_conv_w, gdn_a_log, gdn_dt_bias, gdn_norm_w,
           att_rel_bias, w_branch_a, w_branch_b, w_out, ln1_g, ln1_b, w_ffn_up, ffn_conv_w, w_ffn_down,
           w_ple, w_ple_gate, ln2_g, ln2_b):
    depth = w_in.shape[0]
    assert depth == 1
    bp = x_prompt.shape[0]
    row = lambda v: v.reshape(1, -1).astype(F32)

    xp, xs = x_prompt, x_sample
    new_p, new_s = [], []
    for i in range(depth):
        w_proj = w_in[i].T.astype(BF16)
        wts = (row(ln_in_g), row(ln_in_b), w_proj, gdn_conv_w[i].astype(F32),
               _pad_lanes(gdn_a_log[i]), _pad_lanes(gdn_dt_bias[i]), row(gdn_norm_w[i]), att_rel_bias[i],
               w_branch_a[i].astype(BF16), w_branch_b[i].astype(BF16), w_out[i].astype(BF16),
               row(ln1_g[i]), row(ln1_b[i]), w_ffn_up[i].astype(BF16), ffn_conv_w[i].astype(F32),
               w_ffn_down[i].astype(BF16), w_ple[i].astype(BF16), w_ple_gate[i].astype(BF16),
               row(ln2_g[i]), row(ln2_b[i]))
        xp, st_p = _unpack(_layer(xp, p_prompt[i],
                                  jnp.zeros((bp, GDN_CONV - 1, GDN_CONV_CH), F32),
                                  jnp.zeros((bp, GDN_HEADS, GDN_DK, GDN_DV), F32),
                                  None, None,
                                  jnp.zeros((bp, FFN_CONV - 1, D_FF), F32), wts,
                                  tm_proj=1024, tm_merge=1024, tm_ffn=512, gdn_nb=4, att_chunks=4))
        xs, st_s = _unpack(_layer(xs, p_sample[i], state_gdn_conv[i], state_gdn[i], cache_attn_k[i],
                                  cache_attn_v[i], state_ffn_conv[i], wts,
                                  tm_proj=256, tm_merge=256, tm_ffn=256, gdn_nb=4, ffn_ns=16))
        new_p.append(st_p)
        new_s.append(st_s)
    p_gconv, p_gdn, p_k, p_v, p_fconv = [jnp.stack(a) for a in zip(*new_p)]
    s_gconv, s_gdn, s_k, s_v, s_fconv = [jnp.stack(a) for a in zip(*new_s)]
    return (xp, xs, p_gconv, p_gdn, p_k, p_v, p_fconv, s_gconv, s_gdn, s_k, s_v, s_fconv)


def _unpack(res):
    return res[0], res[1:]
```
